```python
import math
import jax, jax.numpy as jnp
from jax import lax
import numpy as np

D_MODEL = 1024
BATCH = 2
SEQ = 8192
DEPTH = 2

N_MIXERS = 2
EPS = 1e-6

A_EXPAND = 2
A_WIDTH = A_EXPAND * D_MODEL
A_CHUNK = 128
A_HEADS = 16
A_HEAD_DIM = A_WIDTH // A_HEADS

B_GROUPS = ((128, 1), (512, 4), (2048, 16))
B_N_GROUPS = len(B_GROUPS)
B_HEADS = 16
B_HEAD_DIM = 64
B_WIDTH = B_HEADS * B_HEAD_DIM
B_QKV = B_N_GROUPS * 3 * B_WIDTH
B_TOTAL_HEADS = B_N_GROUPS * B_HEADS

REL_BUCKETS = 32
REL_EXACT = 8
REL_MAX_DISTANCE = 1024

N_A_LAYERS = (DEPTH + 1) // 2
N_B_LAYERS = DEPTH // 2
NEG_INF = -1e30

kernel_name = "hybrid_gmlp_dilated_attn_encoder"


def rms_norm(x, g):
    xf = x.astype(jnp.float32)
    y = xf * lax.rsqrt(jnp.mean(xf * xf, axis=-1, keepdims=True) + EPS)
    return (y * g.astype(jnp.float32)).astype(x.dtype)


def layer_norm(x, g, b):
    xf = x.astype(jnp.float32)
    mu = jnp.mean(xf, axis=-1, keepdims=True)
    xc = xf - mu
    y = xc * lax.rsqrt(jnp.mean(xc * xc, axis=-1, keepdims=True) + EPS)
    return (y * g.astype(jnp.float32) + b.astype(jnp.float32)).astype(x.dtype)


def t5_bucket(rel):
    half = REL_BUCKETS // 2
    ret = jnp.where(rel > 0, half, 0)
    n = jnp.abs(rel)
    nf = jnp.maximum(n, 1).astype(jnp.float32)
    large = REL_EXACT + (jnp.log(nf / REL_EXACT) / math.log(REL_MAX_DISTANCE / REL_EXACT)
                         * (half - REL_EXACT)).astype(jnp.int32)
    large = jnp.minimum(large, half - 1)
    return ret + jnp.where(n < REL_EXACT, n, large)


def gmlp_mixer(h, w_in, w_s, b_s, vn_g, vn_b, w_out):
    bsz, s, _ = h.shape
    z = h @ w_in
    u, v, g = jnp.split(z, 3, axis=-1)
    u = jax.nn.gelu(u)
    v = layer_norm(jax.nn.gelu(v), vn_g, vn_b)
    nc = s // A_CHUNK
    vc = v.reshape(bsz, nc, A_CHUNK, A_HEADS, A_HEAD_DIM)
    sg = jnp.einsum('hpq,bcqhd->bcphd', w_s, vc) + b_s.T[None, None, :, :, None]
    y = u * sg.reshape(bsz, s, A_WIDTH) * jax.nn.silu(g)
    return y @ w_out


def dilated_window_group(q, k, v, table, dilation, half_w):
    bsz, s, nh, dh = q.shape
    L = s // dilation
    W = half_w
    nb = -(-L // W)
    Lp = nb * W

    def strided(t):
        return t.reshape(bsz, L, dilation, nh, dh).transpose(0, 2, 1, 3, 4)

    qs = jnp.pad(strided(q), ((0, 0), (0, 0), (0, Lp - L), (0, 0), (0, 0)))
    pad_kv = ((0, 0), (0, 0), (W, Lp - L + W), (0, 0), (0, 0))
    kp = jnp.pad(strided(k), pad_kv).reshape(bsz, dilation, nb + 2, W, nh, dh)
    vp = jnp.pad(strided(v), pad_kv).reshape(bsz, dilation, nb + 2, W, nh, dh)

    def band(t):
        return jnp.concatenate([t[:, :, 0:nb], t[:, :, 1:nb + 1], t[:, :, 2:nb + 2]], axis=3)

    kb, vb = band(kp), band(vp)
    qb = qs.reshape(bsz, dilation, nb, W, nh, dh)

    logits = jnp.einsum('bnkqhd,bnkjhd->bnkhqj', qb, kb,
                        preferred_element_type=jnp.float32) * (dh ** -0.5)
    qi = jnp.arange(W, dtype=jnp.int32)[:, None]
    kj = jnp.arange(3 * W, dtype=jnp.int32)[None, :]
    rel = kj - W - qi
    bias = table.astype(jnp.float32)[t5_bucket(rel * dilation)].transpose(2, 0, 1)
    key_pos = jnp.arange(nb, dtype=jnp.int32)[:, None] * W + kj - W
    mask = (jnp.abs(rel) <= W)[None] & ((key_pos >= 0) & (key_pos < L))[:, None, :]
    logits = jnp.where(mask[None, None, :, None], logits + bias, NEG_INF)

    m = jnp.max(logits, axis=-1, keepdims=True)
    p = jnp.exp(logits - m)
    den = jnp.sum(p, axis=-1, keepdims=True)
    o = jnp.einsum('bnkhqj,bnkjhd->bnkqhd', p / den, vb.astype(jnp.float32))
    lse = (m + jnp.log(den))[..., 0]

    o = o.reshape(bsz, dilation, Lp, nh, dh)[:, :, :L].transpose(0, 2, 1, 3, 4).reshape(bsz, s, nh, dh)
    lse = lse.transpose(0, 1, 2, 4, 3).reshape(bsz, dilation, Lp, nh)[:, :, :L]
    lse = lse.transpose(0, 2, 1, 3).reshape(bsz, s, nh)
    return o, lse


def dilated_attention_mixer(h, w_in, w_out, rel_table):
    bsz, s, _ = h.shape
    z = h @ w_in
    qkv = z[..., :B_QKV].reshape(bsz, s, B_N_GROUPS, 3, B_HEADS, B_HEAD_DIM)
    gate = z[..., B_QKV:]
    outs, lses = [], []
    for gi, (window, dil) in enumerate(B_GROUPS):
        o, lse = dilated_window_group(qkv[:, :, gi, 0], qkv[:, :, gi, 1], qkv[:, :, gi, 2],
                                      rel_table[:, gi * B_HEADS:(gi + 1) * B_HEADS],
                                      dil, window // (2 * dil))
        outs.append(o)
        lses.append(lse)
    wts = jax.nn.softmax(jnp.stack(lses), axis=0)
    o = jnp.einsum('gbsh,gbshd->bshd', wts, jnp.stack(outs))
    y = o.reshape(bsz, s, B_WIDTH).astype(h.dtype) * jax.nn.silu(gate)
    return y @ w_out


def setup_inputs(seed: int = 0) -> dict:
    key = jax.random.key(seed)
    ks = jax.random.split(key, 16)
    f32 = jnp.float32
    x = jax.random.normal(ks[0], (BATCH, SEQ, D_MODEL), f32)
    norm_pre = 1.0 + 0.1 * jax.random.normal(ks[1], (DEPTH, D_MODEL), f32)
    norm_post = 1.0 + 0.1 * jax.random.normal(ks[2], (DEPTH, D_MODEL), f32)
    a_w_in = jax.random.normal(ks[3], (N_A_LAYERS, D_MODEL, 3 * A_WIDTH), f32) * D_MODEL ** -0.5
    a_w_s = jax.random.normal(ks[4], (N_A_LAYERS, A_HEADS, A_CHUNK, A_CHUNK), f32) * A_CHUNK ** -0.5
    a_b_s = 1.0 + 0.1 * jax.random.normal(ks[5], (N_A_LAYERS, A_HEADS, A_CHUNK), f32)
    a_vnorm_g = 1.0 + 0.1 * jax.random.normal(ks[6], (N_A_LAYERS, A_WIDTH), f32)
    a_vnorm_b = 0.1 * jax.random.normal(ks[7], (N_A_LAYERS, A_WIDTH), f32)
    a_w_out = jax.random.normal(ks[8], (N_A_LAYERS, A_WIDTH, D_MODEL), f32) * A_WIDTH ** -0.5
    b_w_in = jax.random.normal(ks[9], (N_B_LAYERS, D_MODEL, B_QKV + B_WIDTH), f32) * D_MODEL ** -0.5
    b_w_out = jax.random.normal(ks[10], (N_B_LAYERS, B_WIDTH, D_MODEL), f32) * B_WIDTH ** -0.5
    rel_bias = 0.5 * jax.random.normal(ks[11], (REL_BUCKETS, B_TOTAL_HEADS), f32)
    return {"x": x, "norm_pre": norm_pre, "norm_post": norm_post,
            "a_w_in": a_w_in, "a_w_s": a_w_s, "a_b_s": a_b_s,
            "a_vnorm_g": a_vnorm_g, "a_vnorm_b": a_vnorm_b, "a_w_out": a_w_out,
            "b_w_in": b_w_in, "b_w_out": b_w_out, "rel_bias": rel_bias}


def reference(x, norm_pre, norm_post, a_w_in, a_w_s, a_b_s, a_vnorm_g, a_vnorm_b, a_w_out,
              b_w_in, b_w_out, rel_bias):
    for i in range(DEPTH):
        h = rms_norm(x, norm_pre[i])
        j = i // N_MIXERS
        if i % N_MIXERS == 0:
            y = gmlp_mixer(h, a_w_in[j], a_w_s[j], a_b_s[j], a_vnorm_g[j], a_vnorm_b[j], a_w_out[j])
        else:
            y = dilated_attention_mixer(h, b_w_in[j], b_w_out[j], rel_bias)
        x = x + rms_norm(y, norm_post[i])
    return x
```

```python
import functools
import math

import jax
import jax.numpy as jnp
import numpy as np
from jax import lax
from jax.experimental import pallas as pl
from jax.experimental.pallas import tpu as pltpu

F32 = jnp.float32
BF16 = jnp.bfloat16

EPS = 1e-6
NEG_INF = -1e30

D_MODEL = 1024
A_WIDTH = 2048
A_CHUNK = 128
A_HEADS = 16
A_HEAD_DIM = 128

B_GROUPS = ((128, 1), (512, 4), (2048, 16))
B_HEADS = 16
B_HEAD_DIM = 64
B_WIDTH = B_HEADS * B_HEAD_DIM
B_GROUP_COLS = 3 * B_WIDTH
B_QKV = len(B_GROUPS) * B_GROUP_COLS
REL_BUCKETS = 32
REL_EXACT = 8
REL_MAX_DISTANCE = 1024

LANES = 128
HEAD_PAIRS = B_WIDTH // LANES
Q_BLOCK = 128
K_WINDOW = 256
VMEM_LIMIT = 58 * 1024 * 1024


def _rms(x, g):
    return x * lax.rsqrt(jnp.mean(x * x, axis=-1, keepdims=True) + EPS) * g


def _gmlp_kernel(x_ref, gpre_ref, gpost_ref, win_ref, ws_ref, bs_ref, vng_ref, vnb_ref, wout_ref,
                 o_ref, vn_scr, y_scr, *, tm):
    nchunk = tm // A_CHUNK
    x = x_ref[...]
    h = _rms(x, gpre_ref[...]).astype(BF16)

    v = jnp.dot(h, win_ref[:, A_WIDTH:2 * A_WIDTH], preferred_element_type=F32)
    v = jax.nn.gelu(v)
    mu = jnp.mean(v, axis=-1, keepdims=True)
    vc = v - mu
    vn = vc * lax.rsqrt(jnp.mean(vc * vc, axis=-1, keepdims=True) + EPS)
    vn_scr[...] = (vn * vng_ref[...] + vnb_ref[...]).astype(BF16)

    cb_w = 512
    heads_per_cb = cb_w // A_HEAD_DIM
    for cb in range(A_WIDTH // cb_w):
        u = jax.nn.gelu(jnp.dot(h, win_ref[:, cb * cb_w:(cb + 1) * cb_w], preferred_element_type=F32))
        g = jnp.dot(h, win_ref[:, 2 * A_WIDTH + cb * cb_w:2 * A_WIDTH + (cb + 1) * cb_w],
                    preferred_element_type=F32)
        ug = u * (g * jax.nn.sigmoid(g))
        for hh in range(heads_per_cb):
            hd = cb * heads_per_cb + hh
            c0 = hd * A_HEAD_DIM
            rhs = jnp.concatenate(
                [vn_scr[c * A_CHUNK:(c + 1) * A_CHUNK, c0:c0 + A_HEAD_DIM] for c in range(nchunk)], axis=1)
            sg = jnp.dot(ws_ref[hd], rhs, preferred_element_type=F32)
            for c in range(nchunk):
                sgc = sg[:, c * A_CHUNK:(c + 1) * A_CHUNK] + bs_ref[hd]
                yb = ug[c * A_CHUNK:(c + 1) * A_CHUNK, hh * A_HEAD_DIM:(hh + 1) * A_HEAD_DIM] * sgc
                y_scr[c * A_CHUNK:(c + 1) * A_CHUNK, c0:c0 + A_HEAD_DIM] = yb.astype(BF16)

    out = jnp.dot(y_scr[...], wout_ref[...], preferred_element_type=F32)
    o_ref[...] = x + _rms(out, gpost_ref[...])


def _gmlp_layer(x2, gpre, gpost, w_in, w_s, b_s, vn_g, vn_b, w_out, *, tm=256):
    t = x2.shape[0]
    const = lambda *shape: pl.BlockSpec(shape, lambda i: (0,) * len(shape))
    return pl.pallas_call(
        functools.partial(_gmlp_kernel, tm=tm),
        grid=(t // tm,),
        in_specs=[
            pl.BlockSpec((tm, D_MODEL), lambda i: (i, 0)),
            const(1, D_MODEL), const(1, D_MODEL),
            const(D_MODEL, 3 * A_WIDTH),
            const(A_HEADS, A_CHUNK, A_CHUNK),
            const(A_HEADS, A_CHUNK, A_HEAD_DIM),
            const(1, A_WIDTH), const(1, A_WIDTH),
            const(A_WIDTH, D_MODEL),
        ],
        out_specs=pl.BlockSpec((tm, D_MODEL), lambda i: (i, 0)),
        out_shape=jax.ShapeDtypeStruct((t, D_MODEL), F32),
        scratch_shapes=[pltpu.VMEM((tm, A_WIDTH), BF16), pltpu.VMEM((tm, A_WIDTH), BF16)],
        compiler_params=pltpu.CompilerParams(dimension_semantics=("arbitrary",),
                                             vmem_limit_bytes=VMEM_LIMIT),
        name="gmlp_layer",
    )(x2, gpre, gpost, w_in, w_s, b_s, vn_g, vn_b, w_out)


def _qkv_kernel(x_ref, gpre_ref, w_ref, *refs, tm, dils):
    out_refs, h_scr = refs[:len(dils)], refs[len(dils)]
    nslab = D_MODEL // LANES
    h = _rms(x_ref[...], gpre_ref[...])
    for k in range(nslab):
        h_scr[k] = h[:, k * LANES:(k + 1) * LANES]
    nc = 512
    for gi, d in enumerate(dils):
        rows = tm // d
        hp = jnp.concatenate(
            [jnp.concatenate([h_scr[k, pl.ds(r, rows, stride=d), :] for r in range(d)], axis=0)
             for k in range(nslab)], axis=1).astype(BF16)
        for c in range(B_GROUP_COLS // nc):
            c0 = gi * B_GROUP_COLS + c * nc
            z = jnp.dot(hp, w_ref[:, c0:c0 + nc], preferred_element_type=F32)
            out_refs[gi][:, :, c * nc:(c + 1) * nc] = z.reshape(d, rows, nc).astype(BF16)


def _qkv_projection(x3, gpre, w_qkv, *, tm=512):
    bsz, s, _ = x3.shape
    dils = tuple(d for _, d in B_GROUPS)
    return pl.pallas_call(
        functools.partial(_qkv_kernel, tm=tm, dils=dils),
        grid=(bsz, s // tm),
        in_specs=[
            pl.BlockSpec((None, tm, D_MODEL), lambda b, i: (b, i, 0)),
            pl.BlockSpec((1, D_MODEL), lambda b, i: (0, 0)),
            pl.BlockSpec((D_MODEL, B_QKV), lambda b, i: (0, 0), pipeline_mode=pl.Buffered(1)),
        ],
        out_specs=[pl.BlockSpec((None, d, tm // d, B_GROUP_COLS), lambda b, i: (b, 0, i, 0)) for d in dils],
        out_shape=[jax.ShapeDtypeStruct((bsz, d, s // d, B_GROUP_COLS), BF16) for d in dils],
        scratch_shapes=[pltpu.VMEM((D_MODEL // LANES, tm, LANES), F32)],
        compiler_params=pltpu.CompilerParams(dimension_semantics=("arbitrary", "arbitrary"),
                                             vmem_limit_bytes=VMEM_LIMIT),
        name="qkv_projection",
    )(x3, gpre, w_qkv)


def _t5_bucket_np(rel):
    half = REL_BUCKETS // 2
    ret = np.where(rel > 0, half, 0)
    n = np.abs(rel)
    nf = np.maximum(n, 1).astype(np.float32)
    large = REL_EXACT + (np.log(nf / np.float32(REL_EXACT)) / np.float32(math.log(REL_MAX_DISTANCE / REL_EXACT))
                         * np.float32(half - REL_EXACT)).astype(np.int32)
    large = np.minimum(large, half - 1)
    return (ret + np.where(n < REL_EXACT, n, large)).astype(np.int32)


def _bias_tiles(table, dilation, half_w):
    qi = np.arange(Q_BLOCK, dtype=np.int32)[:, None]
    kj = np.arange(K_WINDOW, dtype=np.int32)[None, :]
    rel = np.stack([kj - qi + off for off in (0, -half_w, -2 * half_w)])
    bucket = _t5_bucket_np(rel * dilation)
    mask = np.abs(rel) <= half_w
    return jnp.where(mask[None], table.astype(F32).T[:, bucket], NEG_INF)


def _attn_kernel(q_ref, k_ref, v_ref, bm_ref, o_ref, lse_ref, *, seq, half_w):
    nblk = seq // Q_BLOCK
    lane = lax.broadcasted_iota(jnp.int32, (Q_BLOCK, LANES), 1)
    first_head = lane < B_HEAD_DIM
    scale = B_HEAD_DIM ** -0.5

    def block(i, carry):
        q0 = pl.multiple_of(i * Q_BLOCK, Q_BLOCK)
        k0 = pl.multiple_of(jnp.clip(i * Q_BLOCK - half_w, 0, seq - K_WINDOW), half_w)
        var = jnp.where(i == 0, 0, jnp.where(i == nblk - 1, 2, 1))
        qb = q_ref[pl.ds(q0, Q_BLOCK), :]
        kw = k_ref[pl.ds(k0, K_WINDOW), :]
        vw = v_ref[pl.ds(k0, K_WINDOW), :]
        outs, lses = [], []
        for hh, sel in enumerate((first_head, jnp.logical_not(first_head))):
            qh = jnp.where(sel, qb, jnp.zeros_like(qb))
            s = lax.dot_general(qh, kw, (((1,), (1,)), ((), ())), preferred_element_type=F32)
            s = s * scale + bm_ref[hh, var]
            m = jnp.max(s, axis=-1, keepdims=True)
            p = jnp.exp(s - m)
            den = jnp.sum(p, axis=-1, keepdims=True)
            o = jnp.dot(p.astype(BF16), vw, preferred_element_type=F32)
            outs.append(o / den)
            lses.append(jnp.broadcast_to(m + jnp.log(den), (Q_BLOCK, LANES)))
        o_ref[pl.ds(q0, Q_BLOCK), :] = jnp.where(first_head, outs[0], outs[1])
        lse_ref[pl.ds(q0, Q_BLOCK), :] = jnp.where(first_head, lses[0], lses[1])
        return carry

    lax.fori_loop(0, nblk, block, 0)


def _window_attention(z, bias, *, half_w):
    bsz, d, seq, _ = z.shape
    assert seq >= K_WINDOW and seq % Q_BLOCK == 0 and K_WINDOW == Q_BLOCK + 2 * half_w
    col = lambda off: pl.BlockSpec((None, None, seq, LANES), lambda hp, b, r: (b, r, 0, off + hp))
    out = pl.BlockSpec((None, None, seq, LANES), lambda hp, b, r: (b, r, 0, hp))
    return pl.pallas_call(
        functools.partial(_attn_kernel, seq=seq, half_w=half_w),
        grid=(HEAD_PAIRS, bsz, d),
        in_specs=[col(0), col(HEAD_PAIRS), col(2 * HEAD_PAIRS),
                  pl.BlockSpec((2, 3, Q_BLOCK, K_WINDOW), lambda hp, b, r: (hp, 0, 0, 0))],
        out_specs=[out, out],
        out_shape=[jax.ShapeDtypeStruct((bsz, d, seq, B_WIDTH), F32)] * 2,
        compiler_params=pltpu.CompilerParams(dimension_semantics=("arbitrary",) * 3,
                                             vmem_limit_bytes=VMEM_LIMIT),
        name=f"window_attention_d{d}",
    )(z, z, z, bias)


def _merge_kernel(x_ref, gpre_ref, gpost_ref, wg_ref, wout_ref, *refs, tm, dils):
    ng = len(dils)
    o_refs, l_refs = refs[:ng], refs[ng:2 * ng]
    out_ref, o_scr, l_scr, y_scr = refs[2 * ng:]
    nslab = B_WIDTH // LANES
    x = x_ref[...]
    h = _rms(x, gpre_ref[...]).astype(BF16)
    gate = jnp.dot(h, wg_ref[...], preferred_element_type=F32)

    for gi, d in enumerate(dils):
        rows = tm // d
        for src, dst in ((o_refs[gi], o_scr), (l_refs[gi], l_scr)):
            for k in range(nslab):
                for r in range(d):
                    dst[gi, k, pl.ds(r, rows, stride=d), :] = src[r, :, k * LANES:(k + 1) * LANES]

    for k in range(nslab):
        ls = [l_scr[gi, k] for gi in range(ng)]
        m = functools.reduce(jnp.maximum, ls)
        es = [jnp.exp(l - m) for l in ls]
        num = functools.reduce(lambda a, b: a + b, [e * o_scr[gi, k] for gi, e in enumerate(es)])
        o = num / functools.reduce(lambda a, b: a + b, es)
        gk = gate[:, k * LANES:(k + 1) * LANES]
        y_scr[:, k * LANES:(k + 1) * LANES] = (o * (gk * jax.nn.sigmoid(gk))).astype(BF16)

    out = jnp.dot(y_scr[...], wout_ref[...], preferred_element_type=F32)
    out_ref[...] = x + _rms(out, gpost_ref[...])


def _merge_output(x3, gpre, gpost, w_gate, w_out, outs, lses, *, tm=256):
    bsz, s, _ = x3.shape
    dils = tuple(o.shape[1] for o in outs)
    grp = lambda d: pl.BlockSpec((None, d, tm // d, B_WIDTH), lambda b, i: (b, 0, i, 0))
    const = lambda *shape: pl.BlockSpec(shape, lambda b, i: (0,) * len(shape))
    return pl.pallas_call(
        functools.partial(_merge_kernel, tm=tm, dils=dils),
        grid=(bsz, s // tm),
        in_specs=[pl.BlockSpec((None, tm, D_MODEL), lambda b, i: (b, i, 0)),
                  const(1, D_MODEL), const(1, D_MODEL),
                  const(D_MODEL, B_WIDTH), const(B_WIDTH, D_MODEL)]
                 + [grp(d) for d in dils] * 2,
        out_specs=pl.BlockSpec((None, tm, D_MODEL), lambda b, i: (b, i, 0)),
        out_shape=jax.ShapeDtypeStruct((bsz, s, D_MODEL), F32),
        scratch_shapes=[pltpu.VMEM((len(dils), B_WIDTH // LANES, tm, LANES), F32),
                        pltpu.VMEM((len(dils), B_WIDTH // LANES, tm, LANES), F32),
                        pltpu.VMEM((tm, B_WIDTH), BF16)],
        compiler_params=pltpu.CompilerParams(dimension_semantics=("arbitrary", "arbitrary"),
                                             vmem_limit_bytes=VMEM_LIMIT),
        name="merge_output",
    )(x3, gpre, gpost, w_gate, w_out, *outs, *lses)


def kernel(x, norm_pre, norm_post, a_w_in, a_w_s, a_b_s, a_vnorm_g, a_vnorm_b, a_w_out, b_w_in, b_w_out, rel_bias):
    bsz, s, dm = x.shape
    assert dm == D_MODEL and norm_pre.shape[0] == 2 and a_w_in.shape[0] == 1 and b_w_in.shape[0] == 1

    b_s = jnp.broadcast_to(a_b_s[0][:, :, None], (A_HEADS, A_CHUNK, A_HEAD_DIM))
    x2 = _gmlp_layer(x.reshape(bsz * s, dm), norm_pre[0:1], norm_post[0:1],
                     a_w_in[0].astype(BF16), a_w_s[0].astype(BF16), b_s,
                     a_vnorm_g, a_vnorm_b, a_w_out[0].astype(BF16))
    x3 = x2.reshape(bsz, s, dm)

    w_in = b_w_in[0].astype(BF16)
    zs = _qkv_projection(x3, norm_pre[1:2], w_in[:, :B_QKV])
    outs, lses = [], []
    for gi, (window, dil) in enumerate(B_GROUPS):
        half_w = window // (2 * dil)
        bias = _bias_tiles(rel_bias[:, gi * B_HEADS:(gi + 1) * B_HEADS], dil, half_w)
        o, lse = _window_attention(zs[gi], bias, half_w=half_w)
        outs.append(o)
        lses.append(lse)
    return _merge_output(x3, norm_pre[1:2], norm_post[1:2], w_in[:, B_QKV:], b_w_out[0].astype(BF16), outs, lses)
```

```python
import functools
import math

import jax
import jax.numpy as jnp
import numpy as np
from jax import lax
from jax.experimental import pallas as pl
from jax.experimental.pallas import tpu as pltpu

F32 = jnp.float32
BF16 = jnp.bfloat16

EPS = 1e-6
NEG_INF = -1e30

D_MODEL = 1024
A_WIDTH = 2048
A_CHUNK = 128
A_HEADS = 16
A_HEAD_DIM = 128

B_GROUPS = ((128, 1), (512, 4), (2048, 16))
B_HEADS = 16
B_HEAD_DIM = 64
B_WIDTH = B_HEADS * B_HEAD_DIM
B_GROUP_COLS = 3 * B_WIDTH
B_QKV = len(B_GROUPS) * B_GROUP_COLS
REL_BUCKETS = 32
REL_EXACT = 8
REL_MAX_DISTANCE = 1024

LANES = 128
HEAD_PAIRS = B_WIDTH // LANES
Q_BLOCK = 128
K_WINDOW = 256
ATTN_ROWS = 2048
ATTN_UNROLL = 4
VMEM_LIMIT = 58 * 1024 * 1024


def _rms(x, g):
    return x * lax.rsqrt(jnp.mean(x * x, axis=-1, keepdims=True) + EPS) * g


def _gmlp_kernel(x_ref, gpre_ref, gpost_ref, win_ref, ws_ref, bs_ref, vng_ref, vnb_ref, wout_ref,
                 o_ref, vn_scr, y_scr, *, tm):
    nchunk = tm // A_CHUNK
    x = x_ref[...]
    h = _rms(x, gpre_ref[...]).astype(BF16)

    v = jnp.dot(h, win_ref[:, A_WIDTH:2 * A_WIDTH], preferred_element_type=F32)
    v = jax.nn.gelu(v)
    mu = jnp.mean(v, axis=-1, keepdims=True)
    vc = v - mu
    vn = vc * lax.rsqrt(jnp.mean(vc * vc, axis=-1, keepdims=True) + EPS)
    vn_scr[...] = (vn * vng_ref[...] + vnb_ref[...]).astype(BF16)

    cb_w = 512
    heads_per_cb = cb_w // A_HEAD_DIM
    for cb in range(A_WIDTH // cb_w):
        u = jax.nn.gelu(jnp.dot(h, win_ref[:, cb * cb_w:(cb + 1) * cb_w], preferred_element_type=F32))
        g = jnp.dot(h, win_ref[:, 2 * A_WIDTH + cb * cb_w:2 * A_WIDTH + (cb + 1) * cb_w],
                    preferred_element_type=F32)
        ug = u * (g * jax.nn.sigmoid(g))
        for hh in range(heads_per_cb):
            hd = cb * heads_per_cb + hh
            c0 = hd * A_HEAD_DIM
            rhs = jnp.concatenate(
                [vn_scr[c * A_CHUNK:(c + 1) * A_CHUNK, c0:c0 + A_HEAD_DIM] for c in range(nchunk)], axis=1)
            sg = jnp.dot(ws_ref[hd], rhs, preferred_element_type=F32)
            for c in range(nchunk):
                sgc = sg[:, c * A_CHUNK:(c + 1) * A_CHUNK] + bs_ref[hd]
                yb = ug[c * A_CHUNK:(c + 1) * A_CHUNK, hh * A_HEAD_DIM:(hh + 1) * A_HEAD_DIM] * sgc
                y_scr[c * A_CHUNK:(c + 1) * A_CHUNK, c0:c0 + A_HEAD_DIM] = yb.astype(BF16)

    out = jnp.dot(y_scr[...], wout_ref[...], preferred_element_type=F32)
    o_ref[...] = x + _rms(out, gpost_ref[...])


def _gmlp_layer(x2, gpre, gpost, w_in, w_s, b_s, vn_g, vn_b, w_out, *, tm=256):
    t = x2.shape[0]
    const = lambda *shape: pl.BlockSpec(shape, lambda i: (0,) * len(shape))
    return pl.pallas_call(
        functools.partial(_gmlp_kernel, tm=tm),
        grid=(t // tm,),
        in_specs=[
            pl.BlockSpec((tm, D_MODEL), lambda i: (i, 0)),
            const(1, D_MODEL), const(1, D_MODEL),
            const(D_MODEL, 3 * A_WIDTH),
            const(A_HEADS, A_CHUNK, A_CHUNK),
            const(A_HEADS, A_CHUNK, A_HEAD_DIM),
            const(1, A_WIDTH), const(1, A_WIDTH),
            const(A_WIDTH, D_MODEL),
        ],
        out_specs=pl.BlockSpec((tm, D_MODEL), lambda i: (i, 0)),
        out_shape=jax.ShapeDtypeStruct((t, D_MODEL), F32),
        scratch_shapes=[pltpu.VMEM((tm, A_WIDTH), BF16), pltpu.VMEM((tm, A_WIDTH), BF16)],
        compiler_params=pltpu.CompilerParams(dimension_semantics=("arbitrary",),
                                             vmem_limit_bytes=VMEM_LIMIT),
        name="gmlp_layer",
    )(x2, gpre, gpost, w_in, w_s, b_s, vn_g, vn_b, w_out)


def _qkv_kernel(x_ref, gpre_ref, w_ref, *refs, tm, dils):
    out_refs, h_scr = refs[:len(dils)], refs[len(dils)]
    nslab = D_MODEL // LANES
    h = _rms(x_ref[...], gpre_ref[...])
    for k in range(nslab):
        h_scr[k] = h[:, k * LANES:(k + 1) * LANES]
    nc = 512
    for gi, d in enumerate(dils):
        rows = tm // d
        hp = jnp.concatenate(
            [jnp.concatenate([h_scr[k, pl.ds(r, rows, stride=d), :] for r in range(d)], axis=0)
             for k in range(nslab)], axis=1).astype(BF16)
        for c in range(B_GROUP_COLS // nc):
            c0 = gi * B_GROUP_COLS + c * nc
            z = jnp.dot(hp, w_ref[:, c0:c0 + nc], preferred_element_type=F32)
            out_refs[gi][:, :, c * nc:(c + 1) * nc] = z.reshape(d, rows, nc).astype(BF16)


def _qkv_projection(x3, gpre, w_qkv, *, tm=512):
    bsz, s, _ = x3.shape
    dils = tuple(d for _, d in B_GROUPS)
    return pl.pallas_call(
        functools.partial(_qkv_kernel, tm=tm, dils=dils),
        grid=(bsz, s // tm),
        in_specs=[
            pl.BlockSpec((None, tm, D_MODEL), lambda b, i: (b, i, 0)),
            pl.BlockSpec((1, D_MODEL), lambda b, i: (0, 0)),
            pl.BlockSpec((D_MODEL, B_QKV), lambda b, i: (0, 0), pipeline_mode=pl.Buffered(1)),
        ],
        out_specs=[pl.BlockSpec((None, d, tm // d, B_GROUP_COLS), lambda b, i: (b, 0, i, 0)) for d in dils],
        out_shape=[jax.ShapeDtypeStruct((bsz, d, s // d, B_GROUP_COLS), BF16) for d in dils],
        scratch_shapes=[pltpu.VMEM((D_MODEL // LANES, tm, LANES), F32)],
        compiler_params=pltpu.CompilerParams(dimension_semantics=("arbitrary", "arbitrary"),
                                             vmem_limit_bytes=VMEM_LIMIT),
        name="qkv_projection",
    )(x3, gpre, w_qkv)


def _t5_bucket_np(rel):
    half = REL_BUCKETS // 2
    ret = np.where(rel > 0, half, 0)
    n = np.abs(rel)
    nf = np.maximum(n, 1).astype(np.float32)
    large = REL_EXACT + (np.log(nf / np.float32(REL_EXACT)) / np.float32(math.log(REL_MAX_DISTANCE / REL_EXACT))
                         * np.float32(half - REL_EXACT)).astype(np.int32)
    large = np.minimum(large, half - 1)
    return (ret + np.where(n < REL_EXACT, n, large)).astype(np.int32)


def _bucket_tiles(dilation, half_w):
    qi = np.arange(Q_BLOCK, dtype=np.int32)[:, None]
    kj = np.arange(K_WINDOW, dtype=np.int32)[None, :]
    rel = np.stack([kj - qi + off for off in (0, -half_w, -2 * half_w)])
    return np.where(np.abs(rel) <= half_w, _t5_bucket_np(rel * dilation), -1).astype(np.int32)


def _attn_kernel(tbl_ref, bkt_ref, q_ref, k_ref, v_ref, o_ref, lse_ref, bm_scr, *, seq, nres, half_w, unroll):
    nblk = seq // Q_BLOCK
    hp = pl.program_id(0)

    @pl.when((pl.program_id(1) == 0) & (pl.program_id(2) == 0))
    def _():
        for hh in range(2):
            for var in range(3):
                def fill(b, tile, hh=hh, var=var):
                    return jnp.where(bkt_ref[var] == b, tbl_ref[b, 2 * hp + hh], tile)
                bm_scr[hh, var] = lax.fori_loop(0, REL_BUCKETS, fill,
                                                jnp.full((Q_BLOCK, K_WINDOW), NEG_INF, F32))

    first_head = lax.broadcasted_iota(jnp.int32, (Q_BLOCK, LANES), 1) < B_HEAD_DIM
    ones = jnp.ones((K_WINDOW, LANES), BF16)
    chunks = nblk // unroll

    def trip(it, carry):
        r = it // chunks if nres > 1 else 0
        c = it - r * chunks if nres > 1 else it
        for u in range(unroll):
            i = c * unroll + u
            q0 = pl.multiple_of(i * Q_BLOCK, Q_BLOCK)
            k0 = pl.multiple_of(jnp.clip(i * Q_BLOCK - half_w, 0, seq - K_WINDOW), half_w)
            var = jnp.where(i == 0, 0, jnp.where(i == nblk - 1, 2, 1))
            qb = q_ref[r, pl.ds(q0, Q_BLOCK), :] * (B_HEAD_DIM ** -0.5)
            kw = k_ref[r, pl.ds(k0, K_WINDOW), :]
            vw = v_ref[r, pl.ds(k0, K_WINDOW), :]
            zero = jnp.zeros_like(qb)
            q2 = jnp.concatenate([jnp.where(first_head, qb, zero), jnp.where(first_head, zero, qb)], axis=0)
            s = lax.dot_general(q2, kw, (((1,), (1,)), ((), ())), preferred_element_type=F32)
            s = s + jnp.concatenate([bm_scr[0, var], bm_scr[1, var]], axis=0)
            m = jnp.max(s, axis=-1, keepdims=True)
            p = jnp.exp(s - m).astype(BF16)
            ov = jnp.dot(p, jnp.concatenate([vw, ones], axis=1), preferred_element_type=F32)
            den = ov[:, LANES:]
            o = ov[:, :LANES] / den
            lse = m + jnp.log(den)
            o_ref[r, pl.ds(q0, Q_BLOCK), :] = jnp.where(first_head, o[:Q_BLOCK], o[Q_BLOCK:])
            lse_ref[r, pl.ds(q0, Q_BLOCK), :] = jnp.where(first_head, lse[:Q_BLOCK], lse[Q_BLOCK:])
        return carry

    lax.fori_loop(0, nres * chunks, trip, 0)


def _window_attention(z, table, *, dilation, half_w):
    bsz, d, seq, _ = z.shape
    assert seq >= K_WINDOW and seq % (Q_BLOCK * ATTN_UNROLL) == 0 and K_WINDOW == Q_BLOCK + 2 * half_w
    nres = min(d, max(1, ATTN_ROWS // seq))
    col = lambda off: pl.BlockSpec((None, nres, seq, LANES), lambda hp, b, r: (b, r, 0, off + hp))
    out = pl.BlockSpec((None, nres, seq, LANES), lambda hp, b, r: (b, r, 0, hp))
    return pl.pallas_call(
        functools.partial(_attn_kernel, seq=seq, nres=nres, half_w=half_w, unroll=ATTN_UNROLL),
        grid=(HEAD_PAIRS, bsz, d // nres),
        in_specs=[pl.BlockSpec(memory_space=pltpu.SMEM),
                  pl.BlockSpec((3, Q_BLOCK, K_WINDOW), lambda hp, b, r: (0, 0, 0)),
                  col(0), col(HEAD_PAIRS), col(2 * HEAD_PAIRS)],
        out_specs=[out, out],
        out_shape=[jax.ShapeDtypeStruct((bsz, d, seq, B_WIDTH), F32)] * 2,
        scratch_shapes=[pltpu.VMEM((2, 3, Q_BLOCK, K_WINDOW), F32)],
        compiler_params=pltpu.CompilerParams(dimension_semantics=("arbitrary",) * 3,
                                             vmem_limit_bytes=VMEM_LIMIT),
        name=f"window_attention_d{d}",
    )(table.astype(F32), jnp.asarray(_bucket_tiles(dilation, half_w)), z, z, z)


def _merge_kernel(x_ref, gpre_ref, gpost_ref, wg_ref, wout_ref, *refs, tm, dils):
    ng = len(dils)
    o_refs, l_refs = refs[:ng], refs[ng:2 * ng]
    out_ref, o_scr, l_scr, y_scr = refs[2 * ng:]
    nslab = B_WIDTH // LANES
    x = x_ref[...]
    h = _rms(x, gpre_ref[...]).astype(BF16)
    gate = jnp.dot(h, wg_ref[...], preferred_element_type=F32)

    for gi, d in enumerate(dils):
        rows = tm // d
        for src, dst in ((o_refs[gi], o_scr), (l_refs[gi], l_scr)):
            for k in range(nslab):
                for r in range(d):
                    dst[gi, k, pl.ds(r, rows, stride=d), :] = src[r, :, k * LANES:(k + 1) * LANES]

    for k in range(nslab):
        ls = [l_scr[gi, k] for gi in range(ng)]
        m = functools.reduce(jnp.maximum, ls)
        es = [jnp.exp(l - m) for l in ls]
        num = functools.reduce(lambda a, b: a + b, [e * o_scr[gi, k] for gi, e in enumerate(es)])
        o = num / functools.reduce(lambda a, b: a + b, es)
        gk = gate[:, k * LANES:(k + 1) * LANES]
        y_scr[:, k * LANES:(k + 1) * LANES] = (o * (gk * jax.nn.sigmoid(gk))).astype(BF16)

    out = jnp.dot(y_scr[...], wout_ref[...], preferred_element_type=F32)
    out_ref[...] = x + _rms(out, gpost_ref[...])


def _merge_output(x3, gpre, gpost, w_gate, w_out, outs, lses, *, tm=256):
    bsz, s, _ = x3.shape
    dils = tuple(o.shape[1] for o in outs)
    grp = lambda d: pl.BlockSpec((None, d, tm // d, B_WIDTH), lambda b, i: (b, 0, i, 0))
    const = lambda *shape: pl.BlockSpec(shape, lambda b, i: (0,) * len(shape))
    return pl.pallas_call(
        functools.partial(_merge_kernel, tm=tm, dils=dils),
        grid=(bsz, s // tm),
        in_specs=[pl.BlockSpec((None, tm, D_MODEL), lambda b, i: (b, i, 0)),
                  const(1, D_MODEL), const(1, D_MODEL),
                  const(D_MODEL, B_WIDTH), const(B_WIDTH, D_MODEL)]
                 + [grp(d) for d in dils] * 2,
        out_specs=pl.BlockSpec((None, tm, D_MODEL), lambda b, i: (b, i, 0)),
        out_shape=jax.ShapeDtypeStruct((bsz, s, D_MODEL), F32),
        scratch_shapes=[pltpu.VMEM((len(dils), B_WIDTH // LANES, tm, LANES), F32),
                        pltpu.VMEM((len(dils), B_WIDTH // LANES, tm, LANES), F32),
                        pltpu.VMEM((tm, B_WIDTH), BF16)],
        compiler_params=pltpu.CompilerParams(dimension_semantics=("arbitrary", "arbitrary"),
                                             vmem_limit_bytes=VMEM_LIMIT),
        name="merge_output",
    )(x3, gpre, gpost, w_gate, w_out, *outs, *lses)


def kernel(x, norm_pre, norm_post, a_w_in, a_w_s, a_b_s, a_vnorm_g, a_vnorm_b, a_w_out, b_w_in, b_w_out, rel_bias):
    bsz, s, dm = x.shape
    assert dm == D_MODEL and norm_pre.shape[0] == 2 and a_w_in.shape[0] == 1 and b_w_in.shape[0] == 1

    b_s = jnp.broadcast_to(a_b_s[0][:, :, None], (A_HEADS, A_CHUNK, A_HEAD_DIM))
    x2 = _gmlp_layer(x.reshape(bsz * s, dm), norm_pre[0:1], norm_post[0:1],
                     a_w_in[0].astype(BF16), a_w_s[0].astype(BF16), b_s,
                     a_vnorm_g, a_vnorm_b, a_w_out[0].astype(BF16))
    x3 = x2.reshape(bsz, s, dm)

    w_in = b_w_in[0].astype(BF16)
    zs = _qkv_projection(x3, norm_pre[1:2], w_in[:, :B_QKV])
    outs, lses = [], []
    for gi, (window, dil) in enumerate(B_GROUPS):
        o, lse = _window_attention(zs[gi], rel_bias[:, gi * B_HEADS:(gi + 1) * B_HEADS],
                                   dilation=dil, half_w=window // (2 * dil))
        outs.append(o)
        lses.append(lse)
    return _merge_output(x3, norm_pre[1:2], norm_post[1:2], w_in[:, B_QKV:], b_w_out[0].astype(BF16), outs, lses)
```

```python
import functools
import math

import jax
import jax.numpy as jnp
import numpy as np
from jax import lax
from jax.experimental import pallas as pl
from jax.experimental.pallas import tpu as pltpu

F32 = jnp.float32
BF16 = jnp.bfloat16

EPS = 1e-6
NEG_INF = -1e30

D_MODEL = 1024
A_WIDTH = 2048
A_CHUNK = 128
A_HEADS = 16
A_HEAD_DIM = 128

B_GROUPS = ((128, 1), (512, 4), (2048, 16))
B_HEADS = 16
B_HEAD_DIM = 64
B_WIDTH = B_HEADS * B_HEAD_DIM
B_GROUP_COLS = 3 * B_WIDTH
B_QKV = len(B_GROUPS) * B_GROUP_COLS
REL_BUCKETS = 32
REL_EXACT = 8
REL_MAX_DISTANCE = 1024

LANES = 128
HEAD_PAIRS = B_WIDTH // LANES
Q_BLOCK = 128
K_WINDOW = 256
ATTN_UNROLL = 4
VMEM_LIMIT = 58 * 1024 * 1024


def _rms(x, g):
    return x * lax.rsqrt(jnp.mean(x * x, axis=-1, keepdims=True) + EPS) * g


def _gmlp_kernel(x_ref, gpre_ref, gpost_ref, win_ref, ws_ref, bs_ref, vng_ref, vnb_ref, wout_ref,
                 o_ref, vn_scr, y_scr, *, tm):
    nchunk = tm // A_CHUNK
    x = x_ref[...]
    h = _rms(x, gpre_ref[...]).astype(BF16)

    v = jnp.dot(h, win_ref[:, A_WIDTH:2 * A_WIDTH], preferred_element_type=F32)
    v = jax.nn.gelu(v)
    mu = jnp.mean(v, axis=-1, keepdims=True)
    vc = v - mu
    vn = vc * lax.rsqrt(jnp.mean(vc * vc, axis=-1, keepdims=True) + EPS)
    vn_scr[...] = (vn * vng_ref[...] + vnb_ref[...]).astype(BF16)

    cb_w = 512
    heads_per_cb = cb_w // A_HEAD_DIM
    for cb in range(A_WIDTH // cb_w):
        u = jax.nn.gelu(jnp.dot(h, win_ref[:, cb * cb_w:(cb + 1) * cb_w], preferred_element_type=F32))
        g = jnp.dot(h, win_ref[:, 2 * A_WIDTH + cb * cb_w:2 * A_WIDTH + (cb + 1) * cb_w],
                    preferred_element_type=F32)
        ug = u * (g * jax.nn.sigmoid(g))
        for hh in range(heads_per_cb):
            hd = cb * heads_per_cb + hh
            c0 = hd * A_HEAD_DIM
            rhs = jnp.concatenate(
                [vn_scr[c * A_CHUNK:(c + 1) * A_CHUNK, c0:c0 + A_HEAD_DIM] for c in range(nchunk)], axis=1)
            sg = jnp.dot(ws_ref[hd], rhs, preferred_element_type=F32)
            for c in range(nchunk):
                sgc = sg[:, c * A_CHUNK:(c + 1) * A_CHUNK] + bs_ref[hd]
                yb = ug[c * A_CHUNK:(c + 1) * A_CHUNK, hh * A_HEAD_DIM:(hh + 1) * A_HEAD_DIM] * sgc
                y_scr[c * A_CHUNK:(c + 1) * A_CHUNK, c0:c0 + A_HEAD_DIM] = yb.astype(BF16)

    out = jnp.dot(y_scr[...], wout_ref[...], preferred_element_type=F32)
    o_ref[...] = x + _rms(out, gpost_ref[...])


def _gmlp_layer(x2, gpre, gpost, w_in, w_s, b_s, vn_g, vn_b, w_out, *, tm=256):
    t = x2.shape[0]
    const = lambda *shape: pl.BlockSpec(shape, lambda i: (0,) * len(shape))
    return pl.pallas_call(
        functools.partial(_gmlp_kernel, tm=tm),
        grid=(t // tm,),
        in_specs=[
            pl.BlockSpec((tm, D_MODEL), lambda i: (i, 0)),
            const(1, D_MODEL), const(1, D_MODEL),
            const(D_MODEL, 3 * A_WIDTH),
            const(A_HEADS, A_CHUNK, A_CHUNK),
            const(A_HEADS, A_CHUNK, A_HEAD_DIM),
            const(1, A_WIDTH), const(1, A_WIDTH),
            const(A_WIDTH, D_MODEL),
        ],
        out_specs=pl.BlockSpec((tm, D_MODEL), lambda i: (i, 0)),
        out_shape=jax.ShapeDtypeStruct((t, D_MODEL), F32),
        scratch_shapes=[pltpu.VMEM((tm, A_WIDTH), BF16), pltpu.VMEM((tm, A_WIDTH), BF16)],
        compiler_params=pltpu.CompilerParams(dimension_semantics=("arbitrary",),
                                             vmem_limit_bytes=VMEM_LIMIT),
        name="gmlp_layer",
    )(x2, gpre, gpost, w_in, w_s, b_s, vn_g, vn_b, w_out)


def _qkv_kernel(x_ref, gpre_ref, w_ref, *refs, tm, dils):
    out_refs, h_scr = refs[:len(dils)], refs[len(dils)]
    nslab = D_MODEL // LANES
    h = _rms(x_ref[...], gpre_ref[...])
    for k in range(nslab):
        h_scr[k] = h[:, k * LANES:(k + 1) * LANES]
    nc = 512
    for gi, d in enumerate(dils):
        rows = tm // d
        hp = jnp.concatenate(
            [jnp.concatenate([h_scr[k, pl.ds(r, rows, stride=d), :] for r in range(d)], axis=0)
             for k in range(nslab)], axis=1).astype(BF16)
        for c in range(B_GROUP_COLS // nc):
            c0 = gi * B_GROUP_COLS + c * nc
            z = jnp.dot(hp, w_ref[:, c0:c0 + nc], preferred_element_type=F32)
            out_refs[gi][:, :, c * nc:(c + 1) * nc] = z.reshape(d, rows, nc).astype(BF16)


def _qkv_projection(x3, gpre, w_qkv, *, tm=512):
    bsz, s, _ = x3.shape
    dils = tuple(d for _, d in B_GROUPS)
    return pl.pallas_call(
        functools.partial(_qkv_kernel, tm=tm, dils=dils),
        grid=(bsz, s // tm),
        in_specs=[
            pl.BlockSpec((None, tm, D_MODEL), lambda b, i: (b, i, 0)),
            pl.BlockSpec((1, D_MODEL), lambda b, i: (0, 0)),
            pl.BlockSpec((D_MODEL, B_QKV), lambda b, i: (0, 0), pipeline_mode=pl.Buffered(1)),
        ],
        out_specs=[pl.BlockSpec((None, d, tm // d, B_GROUP_COLS), lambda b, i: (b, 0, i, 0)) for d in dils],
        out_shape=[jax.ShapeDtypeStruct((bsz, d, s // d, B_GROUP_COLS), BF16) for d in dils],
        scratch_shapes=[pltpu.VMEM((D_MODEL // LANES, tm, LANES), F32)],
        compiler_params=pltpu.CompilerParams(dimension_semantics=("arbitrary", "arbitrary"),
                                             vmem_limit_bytes=VMEM_LIMIT),
        name="qkv_projection",
    )(x3, gpre, w_qkv)


def _t5_bucket_np(rel):
    half = REL_BUCKETS // 2
    ret = np.where(rel > 0, half, 0)
    n = np.abs(rel)
    nf = np.maximum(n, 1).astype(np.float32)
    large = REL_EXACT + (np.log(nf / np.float32(REL_EXACT)) / np.float32(math.log(REL_MAX_DISTANCE / REL_EXACT))
                         * np.float32(half - REL_EXACT)).astype(np.int32)
    large = np.minimum(large, half - 1)
    return (ret + np.where(n < REL_EXACT, n, large)).astype(np.int32)


def _bucket_tiles(dilation, half_w):
    qi = np.arange(Q_BLOCK, dtype=np.int32)[:, None]
    kj = np.arange(K_WINDOW, dtype=np.int32)[None, :]
    rel = np.stack([kj - qi + off for off in (0, -half_w, -2 * half_w)])
    return np.where(np.abs(rel) <= half_w, _t5_bucket_np(rel * dilation), -1).astype(np.int32)


def _attn_kernel(tbl_ref, bkt_ref, *refs, seq, groups, unroll):
    ng = len(groups)
    z_refs, o_ref = refs[:ng], refs[ng]
    buf, sem, m_scr, den_scr, bm_scr, p_scr, pm_scr = refs[ng + 1:]
    hp, b = pl.program_id(0), pl.program_id(1)
    nb = pl.num_programs(1)
    step = hp * nb + b
    nsteps = pl.num_programs(0) * nb

    def slab_copies(g, hp_, b_):
        return [pltpu.make_async_copy(
            z_refs[g].at[b_, :, pl.ds(pl.multiple_of((j * HEAD_PAIRS + hp_) * LANES, LANES), LANES)],
            buf.at[g, j], sem.at[g, j]) for j in range(3)]

    @pl.when(step == 0)
    def _():
        for cp in slab_copies(0, hp, b):
            cp.start()

    @pl.when(b == 0)
    def _():
        for g in range(ng):
            for hh in range(2):
                for var in range(3):
                    def fill(bk, tile, g=g, hh=hh, var=var):
                        return jnp.where(bkt_ref[g, var] == bk, tbl_ref[bk, g * B_HEADS + 2 * hp + hh], tile)
                    bm_scr[g, hh, var] = lax.fori_loop(0, REL_BUCKETS, fill,
                                                       jnp.full((Q_BLOCK, K_WINDOW), NEG_INF, F32))

    first_head = lax.broadcasted_iota(jnp.int32, (Q_BLOCK, LANES), 1) < B_HEAD_DIM
    ones = jnp.ones((K_WINDOW, LANES), BF16)

    def run_group(g):
        d, half_w = groups[g]
        length = seq // d
        nblk = length // Q_BLOCK
        chunks = nblk // unroll
        ntrips = d * chunks
        q_ref, k_ref, v_ref = buf.at[g, 0], buf.at[g, 1], buf.at[g, 2]

        def blocks(it):
            r = it // chunks if d > 1 else 0
            c = it - r * chunks if d > 1 else it
            for u in range(unroll):
                i = c * unroll + u
                k0 = pl.multiple_of(r * length + jnp.clip(i * Q_BLOCK - half_w, 0, length - K_WINDOW), half_w)
                yield u, r, i, k0

        def logits_stage(it, slot):
            for u, r, i, k0 in blocks(it):
                q0 = pl.multiple_of(r * length + i * Q_BLOCK, Q_BLOCK)
                var = jnp.where(i == 0, 0, jnp.where(i == nblk - 1, 2, 1))
                qb = q_ref[pl.ds(q0, Q_BLOCK), :] * (B_HEAD_DIM ** -0.5)
                kw = k_ref[pl.ds(k0, K_WINDOW), :]
                zero = jnp.zeros_like(qb)
                q2 = jnp.concatenate([jnp.where(first_head, qb, zero), jnp.where(first_head, zero, qb)], axis=0)
                s = lax.dot_general(q2, kw, (((1,), (1,)), ((), ())), preferred_element_type=F32)
                s = s + jnp.concatenate([bm_scr[g, 0, var], bm_scr[g, 1, var]], axis=0)
                m2 = jnp.max(s, axis=-1, keepdims=True)
                p_scr[slot, u] = jnp.exp(s - m2).astype(BF16)
                pm_scr[slot, u] = jnp.where(first_head, m2[:Q_BLOCK], m2[Q_BLOCK:])

        def values_stage(it, slot):
            for u, r, i, k0 in blocks(it):
                vw = v_ref[pl.ds(k0, K_WINDOW), :]
                ov = jnp.dot(p_scr[slot, u], jnp.concatenate([vw, ones], axis=1), preferred_element_type=F32)
                acc = jnp.where(first_head, ov[:Q_BLOCK, :LANES], ov[Q_BLOCK:, :LANES])
                den = jnp.where(first_head, ov[:Q_BLOCK, LANES:], ov[Q_BLOCK:, LANES:])
                m = pm_scr[slot, u]
                nat = pl.ds(i * Q_BLOCK, Q_BLOCK) if d == 1 else pl.ds(i * (Q_BLOCK * d) + r, Q_BLOCK, stride=d)
                if g > 0:
                    m_old = m_scr[nat, :]
                    m_new = jnp.maximum(m_old, m)
                    w_old, w_cur = jnp.exp(m_old - m_new), jnp.exp(m - m_new)
                    acc = w_old * o_ref[nat, :] + w_cur * acc
                    den = w_old * den_scr[nat, :] + w_cur * den
                    m = m_new
                if g == ng - 1:
                    o_ref[nat, :] = acc / den
                else:
                    o_ref[nat, :] = acc
                    m_scr[nat, :] = m
                    den_scr[nat, :] = den

        logits_stage(0, 0)

        def trip(it, carry):
            values_stage(it - 1, (it - 1) % 2)
            logits_stage(it, it % 2)
            return carry

        lax.fori_loop(1, ntrips, trip, 0)
        values_stage(ntrips - 1, (ntrips - 1) % 2)

    for g in range(ng):
        if g + 1 < ng:
            for cp in slab_copies(g + 1, hp, b):
                cp.start()
        else:
            @pl.when(step + 1 < nsteps)
            def _():
                nxt = step + 1
                for cp in slab_copies(0, nxt // nb, nxt % nb):
                    cp.start()
        for cp in slab_copies(g, hp, b):
            cp.wait()
        run_group(g)


def _window_attention(zs, rel_bias):
    bsz, s, _ = zs[0].shape
    groups = tuple((dil, window // (2 * dil)) for window, dil in B_GROUPS)
    for d, half_w in groups:
        assert K_WINDOW == Q_BLOCK + 2 * half_w and (s // d) % (Q_BLOCK * ATTN_UNROLL) == 0
    bkt = jnp.asarray(np.stack([_bucket_tiles(d, half_w) for d, half_w in groups]))
    return pl.pallas_call(
        functools.partial(_attn_kernel, seq=s, groups=groups, unroll=ATTN_UNROLL),
        grid=(HEAD_PAIRS, bsz),
        in_specs=[pl.BlockSpec(memory_space=pltpu.SMEM),
                  pl.BlockSpec(bkt.shape, lambda hp, b: (0, 0, 0, 0), pipeline_mode=pl.Buffered(1))]
                 + [pl.BlockSpec(memory_space=pl.ANY)] * len(groups),
        out_specs=pl.BlockSpec((None, s, LANES), lambda hp, b: (b, 0, hp)),
        out_shape=jax.ShapeDtypeStruct((bsz, s, B_WIDTH), F32),
        scratch_shapes=[pltpu.VMEM((len(groups), 3, s, LANES), BF16),
                        pltpu.SemaphoreType.DMA((len(groups), 3)),
                        pltpu.VMEM((s, LANES), F32),
                        pltpu.VMEM((s, LANES), F32),
                        pltpu.VMEM((len(groups), 2, 3, Q_BLOCK, K_WINDOW), F32),
                        pltpu.VMEM((2, ATTN_UNROLL, 2 * Q_BLOCK, K_WINDOW), BF16),
                        pltpu.VMEM((2, ATTN_UNROLL, Q_BLOCK, LANES), F32)],
        compiler_params=pltpu.CompilerParams(dimension_semantics=("arbitrary", "arbitrary"),
                                             vmem_limit_bytes=VMEM_LIMIT),
        name="window_attention",
    )(rel_bias.astype(F32), bkt, *zs)


def _out_kernel(x_ref, o_ref, gpre_ref, gpost_ref, wg_ref, wout_ref, out_ref):
    x = x_ref[...]
    h = _rms(x, gpre_ref[...]).astype(BF16)
    gate = jnp.dot(h, wg_ref[...], preferred_element_type=F32)
    y = (o_ref[...] * (gate * jax.nn.sigmoid(gate))).astype(BF16)
    out = jnp.dot(y, wout_ref[...], preferred_element_type=F32)
    out_ref[...] = x + _rms(out, gpost_ref[...])


def _gate_output(x2, o2, gpre, gpost, w_gate, w_out, *, tm=512):
    t = x2.shape[0]
    const = lambda *shape: pl.BlockSpec(shape, lambda i: (0,) * len(shape))
    row = pl.BlockSpec((tm, D_MODEL), lambda i: (i, 0))
    return pl.pallas_call(
        _out_kernel,
        grid=(t // tm,),
        in_specs=[row, row, const(1, D_MODEL), const(1, D_MODEL),
                  const(D_MODEL, B_WIDTH), const(B_WIDTH, D_MODEL)],
        out_specs=row,
        out_shape=jax.ShapeDtypeStruct((t, D_MODEL), F32),
        compiler_params=pltpu.CompilerParams(dimension_semantics=("arbitrary",),
                                             vmem_limit_bytes=VMEM_LIMIT),
        name="gate_output",
    )(x2, o2, gpre, gpost, w_gate, w_out)


def kernel(x, norm_pre, norm_post, a_w_in, a_w_s, a_b_s, a_vnorm_g, a_vnorm_b, a_w_out, b_w_in, b_w_out, rel_bias):
    bsz, s, dm = x.shape
    assert dm == D_MODEL and norm_pre.shape[0] == 2 and a_w_in.shape[0] == 1 and b_w_in.shape[0] == 1

    b_s = jnp.broadcast_to(a_b_s[0][:, :, None], (A_HEADS, A_CHUNK, A_HEAD_DIM))
    x2 = _gmlp_layer(x.reshape(bsz * s, dm), norm_pre[0:1], norm_post[0:1],
                     a_w_in[0].astype(BF16), a_w_s[0].astype(BF16), b_s,
                     a_vnorm_g, a_vnorm_b, a_w_out[0].astype(BF16))

    w_in = b_w_in[0].astype(BF16)
    zs = _qkv_projection(x2.reshape(bsz, s, dm), norm_pre[1:2], w_in[:, :B_QKV])
    o = _window_attention([z.reshape(bsz, s, B_GROUP_COLS) for z in zs], rel_bias)
    out = _gate_output(x2, o.reshape(bsz * s, B_WIDTH), norm_pre[1:2], norm_post[1:2],
                       w_in[:, B_QKV:], b_w_out[0].astype(BF16))
    return out.reshape(bsz, s, dm)
```

```python
import functools
import math

import jax
import jax.numpy as jnp
import numpy as np
from jax import lax
from jax.experimental import pallas as pl
from jax.experimental.pallas import tpu as pltpu

F32 = jnp.float32
BF16 = jnp.bfloat16

EPS = 1e-6
NEG_INF = -1e30

D_MODEL = 1024
A_WIDTH = 2048
A_CHUNK = 128
A_HEADS = 16
A_HEAD_DIM = 128

B_GROUPS = ((128, 1), (512, 4), (2048, 16))
B_HEADS = 16
B_HEAD_DIM = 64
B_WIDTH = B_HEADS * B_HEAD_DIM
B_GROUP_COLS = 3 * B_WIDTH
B_QKV = len(B_GROUPS) * B_GROUP_COLS
REL_BUCKETS = 32
REL_EXACT = 8
REL_MAX_DISTANCE = 1024

LANES = 128
HEAD_PAIRS = B_WIDTH // LANES
Q_BLOCK = 128
K_WINDOW = 256
ATTN_UNROLL = 4
VMEM_LIMIT = 58 * 1024 * 1024


def _rms(x, g):
    return x * lax.rsqrt(jnp.mean(x * x, axis=-1, keepdims=True) + EPS) * g


def _gmlp_kernel(x_ref, gpre_ref, gpost_ref, win_ref, ws_ref, bs_ref, vng_ref, vnb_ref, wout_ref,
                 o_ref, vn_scr, y_scr, *, tm):
    nchunk = tm // A_CHUNK
    x = x_ref[...]
    h = _rms(x, gpre_ref[...]).astype(BF16)

    v = jnp.dot(h, win_ref[:, A_WIDTH:2 * A_WIDTH], preferred_element_type=F32)
    v = jax.nn.gelu(v)
    mu = jnp.mean(v, axis=-1, keepdims=True)
    vc = v - mu
    vn = vc * lax.rsqrt(jnp.mean(vc * vc, axis=-1, keepdims=True) + EPS)
    vn_scr[...] = (vn * vng_ref[...] + vnb_ref[...]).astype(BF16)

    cb_w = 512
    heads_per_cb = cb_w // A_HEAD_DIM
    for cb in range(A_WIDTH // cb_w):
        u = jax.nn.gelu(jnp.dot(h, win_ref[:, cb * cb_w:(cb + 1) * cb_w], preferred_element_type=F32))
        g = jnp.dot(h, win_ref[:, 2 * A_WIDTH + cb * cb_w:2 * A_WIDTH + (cb + 1) * cb_w],
                    preferred_element_type=F32)
        ug = u * (g * jax.nn.sigmoid(g))
        for hh in range(heads_per_cb):
            hd = cb * heads_per_cb + hh
            c0 = hd * A_HEAD_DIM
            rhs = jnp.concatenate(
                [vn_scr[c * A_CHUNK:(c + 1) * A_CHUNK, c0:c0 + A_HEAD_DIM] for c in range(nchunk)], axis=1)
            sg = jnp.dot(ws_ref[hd], rhs, preferred_element_type=F32)
            for c in range(nchunk):
                sgc = sg[:, c * A_CHUNK:(c + 1) * A_CHUNK] + bs_ref[hd]
                yb = ug[c * A_CHUNK:(c + 1) * A_CHUNK, hh * A_HEAD_DIM:(hh + 1) * A_HEAD_DIM] * sgc
                y_scr[c * A_CHUNK:(c + 1) * A_CHUNK, c0:c0 + A_HEAD_DIM] = yb.astype(BF16)

    out = jnp.dot(y_scr[...], wout_ref[...], preferred_element_type=F32)
    o_ref[...] = x + _rms(out, gpost_ref[...])


def _gmlp_layer(x2, gpre, gpost, w_in, w_s, b_s, vn_g, vn_b, w_out, *, tm=256):
    t = x2.shape[0]
    const = lambda *shape: pl.BlockSpec(shape, lambda i: (0,) * len(shape))
    return pl.pallas_call(
        functools.partial(_gmlp_kernel, tm=tm),
        grid=(t // tm,),
        in_specs=[
            pl.BlockSpec((tm, D_MODEL), lambda i: (i, 0)),
            const(1, D_MODEL), const(1, D_MODEL),
            const(D_MODEL, 3 * A_WIDTH),
            const(A_HEADS, A_CHUNK, A_CHUNK),
            const(A_HEADS, A_CHUNK, A_HEAD_DIM),
            const(1, A_WIDTH), const(1, A_WIDTH),
            const(A_WIDTH, D_MODEL),
        ],
        out_specs=pl.BlockSpec((tm, D_MODEL), lambda i: (i, 0)),
        out_shape=jax.ShapeDtypeStruct((t, D_MODEL), F32),
        scratch_shapes=[pltpu.VMEM((tm, A_WIDTH), BF16), pltpu.VMEM((tm, A_WIDTH), BF16)],
        compiler_params=pltpu.CompilerParams(dimension_semantics=("arbitrary",),
                                             vmem_limit_bytes=VMEM_LIMIT),
        name="gmlp_layer",
    )(x2, gpre, gpost, w_in, w_s, b_s, vn_g, vn_b, w_out)


def _qkv_kernel(x_ref, gpre_ref, w_ref, *refs, tm, dils):
    out_refs, h_scr = refs[:len(dils)], refs[len(dils)]
    nslab = D_MODEL // LANES
    h = _rms(x_ref[...], gpre_ref[...])
    for k in range(nslab):
        h_scr[k] = h[:, k * LANES:(k + 1) * LANES]
    nc = 512
    for gi, d in enumerate(dils):
        rows = tm // d
        hp = jnp.concatenate(
            [jnp.concatenate([h_scr[k, pl.ds(r, rows, stride=d), :] for r in range(d)], axis=0)
             for k in range(nslab)], axis=1).astype(BF16)
        for c in range(B_GROUP_COLS // nc):
            c0 = gi * B_GROUP_COLS + c * nc
            z = jnp.dot(hp, w_ref[:, c0:c0 + nc], preferred_element_type=F32)
            out_refs[gi][:, :, c * nc:(c + 1) * nc] = z.reshape(d, rows, nc).astype(BF16)


def _qkv_projection(x3, gpre, w_qkv, *, tm=512):
    bsz, s, _ = x3.shape
    dils = tuple(d for _, d in B_GROUPS)
    return pl.pallas_call(
        functools.partial(_qkv_kernel, tm=tm, dils=dils),
        grid=(bsz, s // tm),
        in_specs=[
            pl.BlockSpec((None, tm, D_MODEL), lambda b, i: (b, i, 0)),
            pl.BlockSpec((1, D_MODEL), lambda b, i: (0, 0)),
            pl.BlockSpec((D_MODEL, B_QKV), lambda b, i: (0, 0), pipeline_mode=pl.Buffered(1)),
        ],
        out_specs=[pl.BlockSpec((None, d, tm // d, B_GROUP_COLS), lambda b, i: (b, 0, i, 0)) for d in dils],
        out_shape=[jax.ShapeDtypeStruct((bsz, d, s // d, B_GROUP_COLS), BF16) for d in dils],
        scratch_shapes=[pltpu.VMEM((D_MODEL // LANES, tm, LANES), F32)],
        compiler_params=pltpu.CompilerParams(dimension_semantics=("arbitrary", "arbitrary"),
                                             vmem_limit_bytes=VMEM_LIMIT),
        name="qkv_projection",
    )(x3, gpre, w_qkv)


def _t5_bucket_np(rel):
    half = REL_BUCKETS // 2
    ret = np.where(rel > 0, half, 0)
    n = np.abs(rel)
    nf = np.maximum(n, 1).astype(np.float32)
    large = REL_EXACT + (np.log(nf / np.float32(REL_EXACT)) / np.float32(math.log(REL_MAX_DISTANCE / REL_EXACT))
                         * np.float32(half - REL_EXACT)).astype(np.int32)
    large = np.minimum(large, half - 1)
    return (ret + np.where(n < REL_EXACT, n, large)).astype(np.int32)


def _bucket_row(dilation, half_w):
    rel = np.arange(K_WINDOW, dtype=np.int32) - half_w
    row = np.where(np.abs(rel) <= half_w, _t5_bucket_np(rel * dilation), -1).astype(np.int32)
    return np.broadcast_to(row, (8, K_WINDOW))


def _attn_kernel(tbl_ref, bkt_ref, *refs, seq, groups, unroll):
    ng = len(groups)
    z_refs, o_ref = refs[:ng], refs[ng]
    buf, sem, m_scr, den_scr, bm_scr, p_scr, pm_scr = refs[ng + 1:]
    hp, b = pl.program_id(0), pl.program_id(1)
    nb = pl.num_programs(1)
    step = hp * nb + b
    nsteps = pl.num_programs(0) * nb

    def slab_copies(g, hp_, b_):
        return [pltpu.make_async_copy(
            z_refs[g].at[b_, :, pl.ds(pl.multiple_of((j * HEAD_PAIRS + hp_) * LANES, LANES), LANES)],
            buf.at[g, j], sem.at[g, j]) for j in range(3)]

    @pl.when(step == 0)
    def _():
        for cp in slab_copies(0, hp, b):
            cp.start()

    @pl.when(b == 0)
    def _():
        key = lax.broadcasted_iota(jnp.int32, (Q_BLOCK, K_WINDOW), 1)
        for g, (_, half_w) in enumerate(groups):
            for hh in range(2):
                def fill(bk, row, g=g, hh=hh):
                    return jnp.where(bkt_ref[g] == bk, tbl_ref[bk, g * B_HEADS + 2 * hp + hh], row)
                row = lax.fori_loop(0, REL_BUCKETS, fill, jnp.full((8, K_WINDOW), NEG_INF, F32))
                mid = pltpu.roll(jnp.broadcast_to(row[0:1], (Q_BLOCK, K_WINDOW)), 0, 1, stride=1, stride_axis=0)
                bm_scr[g, hh, 1] = mid
                bm_scr[g, hh, 0] = jnp.where(key < K_WINDOW - half_w,
                                             pltpu.roll(mid, K_WINDOW - half_w, 1), NEG_INF)
                bm_scr[g, hh, 2] = jnp.where(key >= half_w, pltpu.roll(mid, half_w, 1), NEG_INF)

    first_head = lax.broadcasted_iota(jnp.int32, (Q_BLOCK, LANES), 1) < B_HEAD_DIM
    ones = jnp.ones((K_WINDOW, LANES), BF16)

    def run_group(g):
        d, half_w = groups[g]
        length = seq // d
        nblk = length // Q_BLOCK
        chunks = nblk // unroll
        ntrips = d * chunks
        q_ref, k_ref, v_ref = buf.at[g, 0], buf.at[g, 1], buf.at[g, 2]

        def blocks(it):
            r = it // chunks if d > 1 else 0
            c = it - r * chunks if d > 1 else it
            for u in range(unroll):
                i = c * unroll + u
                k0 = pl.multiple_of(r * length + jnp.clip(i * Q_BLOCK - half_w, 0, length - K_WINDOW), half_w)
                yield u, r, i, k0

        def logits_stage(it, slot):
            for u, r, i, k0 in blocks(it):
                q0 = pl.multiple_of(r * length + i * Q_BLOCK, Q_BLOCK)
                var = jnp.where(i == 0, 0, jnp.where(i == nblk - 1, 2, 1))
                qb = q_ref[pl.ds(q0, Q_BLOCK), :] * (B_HEAD_DIM ** -0.5)
                kw = k_ref[pl.ds(k0, K_WINDOW), :]
                zero = jnp.zeros_like(qb)
                q2 = jnp.concatenate([jnp.where(first_head, qb, zero), jnp.where(first_head, zero, qb)], axis=0)
                s = lax.dot_general(q2, kw, (((1,), (1,)), ((), ())), preferred_element_type=F32)
                s = s + jnp.concatenate([bm_scr[g, 0, var], bm_scr[g, 1, var]], axis=0)
                m2 = jnp.max(s, axis=-1, keepdims=True)
                p_scr[slot, u] = jnp.exp(s - m2).astype(BF16)
                pm_scr[slot, u] = jnp.where(first_head, m2[:Q_BLOCK], m2[Q_BLOCK:])

        def values_stage(it, slot):
            for u, r, i, k0 in blocks(it):
                vw = v_ref[pl.ds(k0, K_WINDOW), :]
                ov = jnp.dot(p_scr[slot, u], jnp.concatenate([vw, ones], axis=1), preferred_element_type=F32)
                acc = jnp.where(first_head, ov[:Q_BLOCK, :LANES], ov[Q_BLOCK:, :LANES])
                den = jnp.where(first_head, ov[:Q_BLOCK, LANES:], ov[Q_BLOCK:, LANES:])
                m = pm_scr[slot, u]
                nat = pl.ds(i * Q_BLOCK, Q_BLOCK) if d == 1 else pl.ds(i * (Q_BLOCK * d) + r, Q_BLOCK, stride=d)
                if g > 0:
                    m_old = m_scr[nat, :]
                    m_new = jnp.maximum(m_old, m)
                    w_old, w_cur = jnp.exp(m_old - m_new), jnp.exp(m - m_new)
                    acc = w_old * o_ref[nat, :] + w_cur * acc
                    den = w_old * den_scr[nat, :] + w_cur * den
                    m = m_new
                if g == ng - 1:
                    o_ref[nat, :] = acc / den
                else:
                    o_ref[nat, :] = acc
                    m_scr[nat, :] = m
                    den_scr[nat, :] = den

        logits_stage(0, 0)

        def trip(it, carry):
            values_stage(it - 1, (it - 1) % 2)
            logits_stage(it, it % 2)
            return carry

        lax.fori_loop(1, ntrips, trip, 0)
        values_stage(ntrips - 1, (ntrips - 1) % 2)

    for g in range(ng):
        if g + 1 < ng:
            for cp in slab_copies(g + 1, hp, b):
                cp.start()
        else:
            @pl.when(step + 1 < nsteps)
            def _():
                nxt = step + 1
                for cp in slab_copies(0, nxt // nb, nxt % nb):
                    cp.start()
        for cp in slab_copies(g, hp, b):
            cp.wait()
        run_group(g)


def _window_attention(zs, rel_bias):
    bsz, s, _ = zs[0].shape
    groups = tuple((dil, window // (2 * dil)) for window, dil in B_GROUPS)
    for d, half_w in groups:
        assert K_WINDOW == Q_BLOCK + 2 * half_w and (s // d) % (Q_BLOCK * ATTN_UNROLL) == 0
    bkt = jnp.asarray(np.stack([_bucket_row(d, half_w) for d, half_w in groups]))
    return pl.pallas_call(
        functools.partial(_attn_kernel, seq=s, groups=groups, unroll=ATTN_UNROLL),
        grid=(HEAD_PAIRS, bsz),
        in_specs=[pl.BlockSpec(memory_space=pltpu.SMEM),
                  pl.BlockSpec(bkt.shape, lambda hp, b: (0, 0, 0))]
                 + [pl.BlockSpec(memory_space=pl.ANY)] * len(groups),
        out_specs=pl.BlockSpec((None, s, LANES), lambda hp, b: (b, 0, hp)),
        out_shape=jax.ShapeDtypeStruct((bsz, s, B_WIDTH), F32),
        scratch_shapes=[pltpu.VMEM((len(groups), 3, s, LANES), BF16),
                        pltpu.SemaphoreType.DMA((len(groups), 3)),
                        pltpu.VMEM((s, LANES), F32),
                        pltpu.VMEM((s, LANES), F32),
                        pltpu.VMEM((len(groups), 2, 3, Q_BLOCK, K_WINDOW), F32),
                        pltpu.VMEM((2, ATTN_UNROLL, 2 * Q_BLOCK, K_WINDOW), BF16),
                        pltpu.VMEM((2, ATTN_UNROLL, Q_BLOCK, LANES), F32)],
        compiler_params=pltpu.CompilerParams(dimension_semantics=("arbitrary", "arbitrary"),
                                             vmem_limit_bytes=VMEM_LIMIT),
        name="window_attention",
    )(rel_bias.astype(F32), bkt, *zs)


def _out_kernel(x_ref, o_ref, gpre_ref, gpost_ref, wg_ref, wout_ref, out_ref):
    x = x_ref[...]
    h = _rms(x, gpre_ref[...]).astype(BF16)
    gate = jnp.dot(h, wg_ref[...], preferred_element_type=F32)
    y = (o_ref[...] * (gate * jax.nn.sigmoid(gate))).astype(BF16)
    out = jnp.dot(y, wout_ref[...], preferred_element_type=F32)
    out_ref[...] = x + _rms(out, gpost_ref[...])


def _gate_output(x2, o2, gpre, gpost, w_gate, w_out, *, tm=512):
    t = x2.shape[0]
    const = lambda *shape: pl.BlockSpec(shape, lambda i: (0,) * len(shape))
    row = pl.BlockSpec((tm, D_MODEL), lambda i: (i, 0))
    return pl.pallas_call(
        _out_kernel,
        grid=(t // tm,),
        in_specs=[row, row, const(1, D_MODEL), const(1, D_MODEL),
                  const(D_MODEL, B_WIDTH), const(B_WIDTH, D_MODEL)],
        out_specs=row,
        out_shape=jax.ShapeDtypeStruct((t, D_MODEL), F32),
        compiler_params=pltpu.CompilerParams(dimension_semantics=("arbitrary",),
                                             vmem_limit_bytes=VMEM_LIMIT),
        name="gate_output",
    )(x2, o2, gpre, gpost, w_gate, w_out)


def kernel(x, norm_pre, norm_post, a_w_in, a_w_s, a_b_s, a_vnorm_g, a_vnorm_b, a_w_out, b_w_in, b_w_out, rel_bias):
    bsz, s, dm = x.shape
    assert dm == D_MODEL and norm_pre.shape[0] == 2 and a_w_in.shape[0] == 1 and b_w_in.shape[0] == 1

    b_s = jnp.broadcast_to(a_b_s[0][:, :, None], (A_HEADS, A_CHUNK, A_HEAD_DIM))
    x2 = _gmlp_layer(x.reshape(bsz * s, dm), norm_pre[0:1], norm_post[0:1],
                     a_w_in[0].astype(BF16), a_w_s[0].astype(BF16), b_s,
                     a_vnorm_g, a_vnorm_b, a_w_out[0].astype(BF16))

    w_in = b_w_in[0].astype(BF16)
    zs = _qkv_projection(x2.reshape(bsz, s, dm), norm_pre[1:2], w_in[:, :B_QKV])
    o = _window_attention([z.reshape(bsz, s, B_GROUP_COLS) for z in zs], rel_bias)
    out = _gate_output(x2, o.reshape(bsz * s, B_WIDTH), norm_pre[1:2], norm_post[1:2],
                       w_in[:, B_QKV:], b_w_out[0].astype(BF16))
    return out.reshape(bsz, s, dm)
```

```python
import functools
import math

import jax
import jax.numpy as jnp
import numpy as np
from jax import lax
from jax.experimental import pallas as pl
from jax.experimental.pallas import tpu as pltpu

F32 = jnp.float32
BF16 = jnp.bfloat16

EPS = 1e-6
NEG_INF = -1e30

D_MODEL = 1024
A_WIDTH = 2048
A_CHUNK = 128
A_HEADS = 16
A_HEAD_DIM = 128

B_GROUPS = ((128, 1), (512, 4), (2048, 16))
B_HEADS = 16
B_HEAD_DIM = 64
B_WIDTH = B_HEADS * B_HEAD_DIM
B_GROUP_COLS = 3 * B_WIDTH
B_QKV = len(B_GROUPS) * B_GROUP_COLS
REL_BUCKETS = 32
REL_EXACT = 8
REL_MAX_DISTANCE = 1024

LANES = 128
HEAD_PAIRS = B_WIDTH // LANES
Q_BLOCK = 128
K_WINDOW = 256
STATE_INTERLEAVE = 4
ATTN_UNROLL = 8
VMEM_LIMIT = 58 * 1024 * 1024


def _rms(x, g):
    return x * lax.rsqrt(jnp.mean(x * x, axis=-1, keepdims=True) + EPS) * g


def _gmlp_kernel(x_ref, gpre_ref, gpost_ref, win_ref, ws_ref, bs_ref, vng_ref, vnb_ref, wout_ref,
                 o_ref, vn_scr, y_scr, *, tm):
    nchunk = tm // A_CHUNK
    x = x_ref[...]
    h = _rms(x, gpre_ref[...]).astype(BF16)

    v = jnp.dot(h, win_ref[:, A_WIDTH:2 * A_WIDTH], preferred_element_type=F32)
    v = jax.nn.gelu(v)
    mu = jnp.mean(v, axis=-1, keepdims=True)
    vc = v - mu
    vn = vc * lax.rsqrt(jnp.mean(vc * vc, axis=-1, keepdims=True) + EPS)
    vn_scr[...] = (vn * vng_ref[...] + vnb_ref[...]).astype(BF16)

    cb_w = 512
    heads_per_cb = cb_w // A_HEAD_DIM
    for cb in range(A_WIDTH // cb_w):
        u = jax.nn.gelu(jnp.dot(h, win_ref[:, cb * cb_w:(cb + 1) * cb_w], preferred_element_type=F32))
        g = jnp.dot(h, win_ref[:, 2 * A_WIDTH + cb * cb_w:2 * A_WIDTH + (cb + 1) * cb_w],
                    preferred_element_type=F32)
        ug = u * (g * jax.nn.sigmoid(g))
        for hh in range(heads_per_cb):
            hd = cb * heads_per_cb + hh
            c0 = hd * A_HEAD_DIM
            rhs = jnp.concatenate(
                [vn_scr[c * A_CHUNK:(c + 1) * A_CHUNK, c0:c0 + A_HEAD_DIM] for c in range(nchunk)], axis=1)
            sg = jnp.dot(ws_ref[hd], rhs, preferred_element_type=F32)
            for c in range(nchunk):
                sgc = sg[:, c * A_CHUNK:(c + 1) * A_CHUNK] + bs_ref[hd]
                yb = ug[c * A_CHUNK:(c + 1) * A_CHUNK, hh * A_HEAD_DIM:(hh + 1) * A_HEAD_DIM] * sgc
                y_scr[c * A_CHUNK:(c + 1) * A_CHUNK, c0:c0 + A_HEAD_DIM] = yb.astype(BF16)

    out = jnp.dot(y_scr[...], wout_ref[...], preferred_element_type=F32)
    o_ref[...] = x + _rms(out, gpost_ref[...])


def _gmlp_layer(x2, gpre, gpost, w_in, w_s, b_s, vn_g, vn_b, w_out, *, tm=256):
    t = x2.shape[0]
    const = lambda *shape: pl.BlockSpec(shape, lambda i: (0,) * len(shape))
    return pl.pallas_call(
        functools.partial(_gmlp_kernel, tm=tm),
        grid=(t // tm,),
        in_specs=[
            pl.BlockSpec((tm, D_MODEL), lambda i: (i, 0)),
            const(1, D_MODEL), const(1, D_MODEL),
            const(D_MODEL, 3 * A_WIDTH),
            const(A_HEADS, A_CHUNK, A_CHUNK),
            const(A_HEADS, A_CHUNK, A_HEAD_DIM),
            const(1, A_WIDTH), const(1, A_WIDTH),
            const(A_WIDTH, D_MODEL),
        ],
        out_specs=pl.BlockSpec((tm, D_MODEL), lambda i: (i, 0)),
        out_shape=jax.ShapeDtypeStruct((t, D_MODEL), F32),
        scratch_shapes=[pltpu.VMEM((tm, A_WIDTH), BF16), pltpu.VMEM((tm, A_WIDTH), BF16)],
        compiler_params=pltpu.CompilerParams(dimension_semantics=("arbitrary",),
                                             vmem_limit_bytes=VMEM_LIMIT),
        name="gmlp_layer",
    )(x2, gpre, gpost, w_in, w_s, b_s, vn_g, vn_b, w_out)


def _qkv_kernel(x_ref, gpre_ref, w_ref, *refs, tm, dils):
    out_refs, h_scr = refs[:len(dils)], refs[len(dils)]
    nslab = D_MODEL // LANES
    h = _rms(x_ref[...], gpre_ref[...])
    for k in range(nslab):
        h_scr[k] = h[:, k * LANES:(k + 1) * LANES]
    nc = 512

    def gathered(starts, rows, stride):
        return jnp.concatenate(
            [jnp.concatenate([h_scr[k, pl.ds(st, rows, stride=stride), :] for st in starts], axis=0)
             for k in range(nslab)], axis=1).astype(BF16)

    for gi, d in enumerate(dils):
        rows = tm // d
        hp = gathered(range(d), rows, d)
        if d == 1:
            per = Q_BLOCK // STATE_INTERLEAVE
            hq = gathered([blk * Q_BLOCK + c for blk in range(tm // Q_BLOCK) for c in range(STATE_INTERLEAVE)],
                          per, STATE_INTERLEAVE)
        else:
            hq = hp
        for c in range(B_GROUP_COLS // nc):
            c0 = gi * B_GROUP_COLS + c * nc
            lhs = hq if (c + 1) * nc <= B_WIDTH else hp
            z = jnp.dot(lhs, w_ref[:, c0:c0 + nc], preferred_element_type=F32)
            out_refs[gi][:, :, c * nc:(c + 1) * nc] = z.reshape(d, rows, nc).astype(BF16)


def _qkv_projection(x3, gpre, w_qkv, *, tm=512):
    bsz, s, _ = x3.shape
    dils = tuple(d for _, d in B_GROUPS)
    return pl.pallas_call(
        functools.partial(_qkv_kernel, tm=tm, dils=dils),
        grid=(bsz, s // tm),
        in_specs=[
            pl.BlockSpec((None, tm, D_MODEL), lambda b, i: (b, i, 0)),
            pl.BlockSpec((1, D_MODEL), lambda b, i: (0, 0)),
            pl.BlockSpec((D_MODEL, B_QKV), lambda b, i: (0, 0), pipeline_mode=pl.Buffered(1)),
        ],
        out_specs=[pl.BlockSpec((None, d, tm // d, B_GROUP_COLS), lambda b, i: (b, 0, i, 0)) for d in dils],
        out_shape=[jax.ShapeDtypeStruct((bsz, d, s // d, B_GROUP_COLS), BF16) for d in dils],
        scratch_shapes=[pltpu.VMEM((D_MODEL // LANES, tm, LANES), F32)],
        compiler_params=pltpu.CompilerParams(dimension_semantics=("arbitrary", "arbitrary"),
                                             vmem_limit_bytes=VMEM_LIMIT),
        name="qkv_projection",
    )(x3, gpre, w_qkv)


def _t5_bucket_np(rel):
    half = REL_BUCKETS // 2
    ret = np.where(rel > 0, half, 0)
    n = np.abs(rel)
    nf = np.maximum(n, 1).astype(np.float32)
    large = REL_EXACT + (np.log(nf / np.float32(REL_EXACT)) / np.float32(math.log(REL_MAX_DISTANCE / REL_EXACT))
                         * np.float32(half - REL_EXACT)).astype(np.int32)
    large = np.minimum(large, half - 1)
    return (ret + np.where(n < REL_EXACT, n, large)).astype(np.int32)


def _bucket_row(dilation, half_w):
    rel = np.arange(K_WINDOW, dtype=np.int32) - half_w
    row = np.where(np.abs(rel) <= half_w, _t5_bucket_np(rel * dilation), -1).astype(np.int32)
    return np.broadcast_to(row, (8, K_WINDOW))


def _attn_kernel(tbl_ref, bkt_ref, *refs, seq, groups, unroll):
    ng = len(groups)
    z_refs, o_ref = refs[:ng], refs[ng]
    buf, sem, acc_scr, m_scr, den_scr, bm_scr, p_scr, pm_scr = refs[ng + 1:]
    per = Q_BLOCK // STATE_INTERLEAVE
    hp, b = pl.program_id(0), pl.program_id(1)
    nb = pl.num_programs(1)
    step = hp * nb + b
    nsteps = pl.num_programs(0) * nb

    def slab_copies(g, hp_, b_):
        return [pltpu.make_async_copy(
            z_refs[g].at[b_, :, pl.ds(pl.multiple_of((j * HEAD_PAIRS + hp_) * LANES, LANES), LANES)],
            buf.at[g, j], sem.at[g, j]) for j in range(3)]

    @pl.when(step == 0)
    def _():
        for cp in slab_copies(0, hp, b):
            cp.start()

    @pl.when(b == 0)
    def _():
        key = lax.broadcasted_iota(jnp.int32, (Q_BLOCK, K_WINDOW), 1)
        for g, (d, half_w, col) in enumerate(groups):
            for hh in range(2):
                def fill(bk, row, g=g, hh=hh, col=col):
                    return jnp.where(bkt_ref[g] == bk, tbl_ref[bk, col + 2 * hp + hh], row)
                row = lax.fori_loop(0, REL_BUCKETS, fill, jnp.full((8, K_WINDOW), NEG_INF, F32))
                if d == 1:
                    mid = jnp.concatenate(
                        [pltpu.roll(jnp.broadcast_to(row[0:1], (per, K_WINDOW)), c, 1,
                                    stride=STATE_INTERLEAVE, stride_axis=0) for c in range(STATE_INTERLEAVE)], axis=0)
                else:
                    mid = pltpu.roll(jnp.broadcast_to(row[0:1], (Q_BLOCK, K_WINDOW)), 0, 1, stride=1, stride_axis=0)
                bm_scr[g, hh, 1] = mid
                bm_scr[g, hh, 0] = jnp.where(key < K_WINDOW - half_w,
                                             pltpu.roll(mid, K_WINDOW - half_w, 1), NEG_INF)
                bm_scr[g, hh, 2] = jnp.where(key >= half_w, pltpu.roll(mid, half_w, 1), NEG_INF)

    first_head = lax.broadcasted_iota(jnp.int32, (Q_BLOCK, LANES), 1) < B_HEAD_DIM
    ones = jnp.ones((K_WINDOW, LANES), BF16)

    def run_group(g):
        d, half_w, _ = groups[g]
        length = seq // d
        nblk = length // Q_BLOCK
        ntrips = d * nblk // unroll
        q_ref, k_ref, v_ref = buf.at[g, 0], buf.at[g, 1], buf.at[g, 2]

        def blocks(it):
            for u in range(unroll):
                if unroll >= nblk:
                    r, i = it * (unroll // nblk) + u // nblk, u % nblk
                else:
                    chunks = nblk // unroll
                    r = it // chunks if d > 1 else 0
                    i = (it - r * chunks) * unroll + u
                k0 = pl.multiple_of(r * length + jnp.clip(i * Q_BLOCK - half_w, 0, length - K_WINDOW), half_w)
                yield u, r, i, k0

        def logits_stage(it, slot):
            for u, r, i, k0 in blocks(it):
                q0 = pl.multiple_of(r * length + i * Q_BLOCK, Q_BLOCK)
                var = jnp.where(i == 0, 0, jnp.where(i == nblk - 1, 2, 1))
                qb = q_ref[pl.ds(q0, Q_BLOCK), :] * (B_HEAD_DIM ** -0.5)
                kw = k_ref[pl.ds(k0, K_WINDOW), :]
                zero = jnp.zeros_like(qb)
                q2 = jnp.concatenate([jnp.where(first_head, qb, zero), jnp.where(first_head, zero, qb)], axis=0)
                s = lax.dot_general(q2, kw, (((1,), (1,)), ((), ())), preferred_element_type=F32)
                s = s + jnp.concatenate([bm_scr[g, 0, var], bm_scr[g, 1, var]], axis=0)
                m2 = jnp.max(s, axis=-1, keepdims=True)
                p_scr[slot, u] = jnp.exp(s - m2).astype(BF16)
                pm_scr[slot, u] = jnp.where(first_head, m2[:Q_BLOCK], m2[Q_BLOCK:])

        def values_stage(it, slot):
            for u, r, i, k0 in blocks(it):
                vw = v_ref[pl.ds(k0, K_WINDOW), :]
                ov = jnp.dot(p_scr[slot, u], jnp.concatenate([vw, ones], axis=1), preferred_element_type=F32)
                acc = jnp.where(first_head, ov[:Q_BLOCK, :LANES], ov[Q_BLOCK:, :LANES])
                den = jnp.where(first_head, ov[:Q_BLOCK, LANES:], ov[Q_BLOCK:, LANES:])
                m = pm_scr[slot, u]
                if d == 1:
                    pieces = [(0, Q_BLOCK, pl.ds(i * Q_BLOCK, Q_BLOCK))]
                else:
                    n = Q_BLOCK // d
                    base = r // STATE_INTERLEAVE + per * (r % STATE_INTERLEAVE)
                    pieces = [(k * n, n, pl.ds((i * d + k) * Q_BLOCK + base, n, stride=d // STATE_INTERLEAVE))
                              for k in range(d)]
                gather = lambda ref: jnp.concatenate([ref[idx, :] for _, _, idx in pieces], axis=0)
                if g > 0:
                    m_old = gather(m_scr)
                    m_new = jnp.maximum(m_old, m)
                    w_old, w_cur = jnp.exp(m_old - m_new), jnp.exp(m - m_new)
                    acc = w_old * gather(acc_scr) + w_cur * acc
                    den = w_old * gather(den_scr) + w_cur * den
                    m = m_new
                if g == ng - 1:
                    o = acc / den
                    for c in range(STATE_INTERLEAVE):
                        o_ref[pl.ds(i * Q_BLOCK + c, per, stride=STATE_INTERLEAVE), :] = o[c * per:(c + 1) * per]
                else:
                    for ref, val in ((acc_scr, acc), (m_scr, m), (den_scr, den)):
                        for r0, n, idx in pieces:
                            ref[idx, :] = val[r0:r0 + n]

        logits_stage(0, 0)

        def trip(it, carry):
            values_stage(it - 1, (it - 1) % 2)
            logits_stage(it, it % 2)
            return carry

        lax.fori_loop(1, ntrips, trip, 0)
        values_stage(ntrips - 1, (ntrips - 1) % 2)

    for g in range(ng):
        if g + 1 < ng:
            for cp in slab_copies(g + 1, hp, b):
                cp.start()
        else:
            @pl.when(step + 1 < nsteps)
            def _():
                nxt = step + 1
                for cp in slab_copies(0, nxt // nb, nxt % nb):
                    cp.start()
        for cp in slab_copies(g, hp, b):
            cp.wait()
        run_group(g)


def _window_attention(zs, rel_bias):
    bsz, s, _ = zs[0].shape
    order = sorted(range(len(B_GROUPS)), key=lambda gi: -B_GROUPS[gi][1])
    groups = tuple((B_GROUPS[gi][1], B_GROUPS[gi][0] // (2 * B_GROUPS[gi][1]), gi * B_HEADS) for gi in order)
    zs = [zs[gi] for gi in order]
    assert groups[-1][0] == 1
    for d, half_w, _ in groups:
        nblk = s // d // Q_BLOCK
        assert K_WINDOW == Q_BLOCK + 2 * half_w and s // d >= K_WINDOW and s % (d * Q_BLOCK) == 0
        assert (d * nblk) % ATTN_UNROLL == 0 and (ATTN_UNROLL % nblk == 0 or nblk % ATTN_UNROLL == 0)
        assert d == 1 or (d % STATE_INTERLEAVE == 0 and Q_BLOCK % d == 0)
    bkt = jnp.asarray(np.stack([_bucket_row(d, half_w) for d, half_w, _ in groups]))
    return pl.pallas_call(
        functools.partial(_attn_kernel, seq=s, groups=groups, unroll=ATTN_UNROLL),
        grid=(HEAD_PAIRS, bsz),
        in_specs=[pl.BlockSpec(memory_space=pltpu.SMEM),
                  pl.BlockSpec(bkt.shape, lambda hp, b: (0, 0, 0))]
                 + [pl.BlockSpec(memory_space=pl.ANY)] * len(groups),
        out_specs=pl.BlockSpec((None, s, LANES), lambda hp, b: (b, 0, hp)),
        out_shape=jax.ShapeDtypeStruct((bsz, s, B_WIDTH), F32),
        scratch_shapes=[pltpu.VMEM((len(groups), 3, s, LANES), BF16),
                        pltpu.SemaphoreType.DMA((len(groups), 3)),
                        pltpu.VMEM((s, LANES), F32),
                        pltpu.VMEM((s, LANES), F32),
                        pltpu.VMEM((s, LANES), F32),
                        pltpu.VMEM((len(groups), 2, 3, Q_BLOCK, K_WINDOW), F32),
                        pltpu.VMEM((2, ATTN_UNROLL, 2 * Q_BLOCK, K_WINDOW), BF16),
                        pltpu.VMEM((2, ATTN_UNROLL, Q_BLOCK, LANES), F32)],
        compiler_params=pltpu.CompilerParams(dimension_semantics=("arbitrary", "arbitrary"),
                                             vmem_limit_bytes=VMEM_LIMIT),
        name="window_attention",
    )(rel_bias.astype(F32), bkt, *zs)


def _out_kernel(x_ref, o_ref, gpre_ref, gpost_ref, wg_ref, wout_ref, out_ref):
    x = x_ref[...]
    h = _rms(x, gpre_ref[...]).astype(BF16)
    gate = jnp.dot(h, wg_ref[...], preferred_element_type=F32)
    y = (o_ref[...] * (gate * jax.nn.sigmoid(gate))).astype(BF16)
    out = jnp.dot(y, wout_ref[...], preferred_element_type=F32)
    out_ref[...] = x + _rms(out, gpost_ref[...])


def _gate_output(x2, o2, gpre, gpost, w_gate, w_out, *, tm=512):
    t = x2.shape[0]
    const = lambda *shape: pl.BlockSpec(shape, lambda i: (0,) * len(shape))
    row = pl.BlockSpec((tm, D_MODEL), lambda i: (i, 0))
    return pl.pallas_call(
        _out_kernel,
        grid=(t // tm,),
        in_specs=[row, row, const(1, D_MODEL), const(1, D_MODEL),
                  const(D_MODEL, B_WIDTH), const(B_WIDTH, D_MODEL)],
        out_specs=row,
        out_shape=jax.ShapeDtypeStruct((t, D_MODEL), F32),
        compiler_params=pltpu.CompilerParams(dimension_semantics=("arbitrary",),
                                             vmem_limit_bytes=VMEM_LIMIT),
        name="gate_output",
    )(x2, o2, gpre, gpost, w_gate, w_out)


def kernel(x, norm_pre, norm_post, a_w_in, a_w_s, a_b_s, a_vnorm_g, a_vnorm_b, a_w_out, b_w_in, b_w_out, rel_bias):
    bsz, s, dm = x.shape
    assert dm == D_MODEL and norm_pre.shape[0] == 2 and a_w_in.shape[0] == 1 and b_w_in.shape[0] == 1

    b_s = jnp.broadcast_to(a_b_s[0][:, :, None], (A_HEADS, A_CHUNK, A_HEAD_DIM))
    x2 = _gmlp_layer(x.reshape(bsz * s, dm), norm_pre[0:1], norm_post[0:1],
                     a_w_in[0].astype(BF16), a_w_s[0].astype(BF16), b_s,
                     a_vnorm_g, a_vnorm_b, a_w_out[0].astype(BF16))

    w_in = b_w_in[0].astype(BF16)
    zs = _qkv_projection(x2.reshape(bsz, s, dm), norm_pre[1:2], w_in[:, :B_QKV])
    o = _window_attention([z.reshape(bsz, s, B_GROUP_COLS) for z in zs], rel_bias)
    out = _gate_output(x2, o.reshape(bsz * s, B_WIDTH), norm_pre[1:2], norm_post[1:2],
                       w_in[:, B_QKV:], b_w_out[0].astype(BF16))
    return out.reshape(bsz, s, dm)
```

```python
import functools
import math

import jax
import jax.numpy as jnp
import numpy as np
from jax import lax
from jax.experimental import pallas as pl
from jax.experimental.pallas import tpu as pltpu

F32 = jnp.float32
BF16 = jnp.bfloat16

EPS = 1e-6
NEG_INF = -1e30

D_MODEL = 1024
A_WIDTH = 2048
A_CHUNK = 128
A_HEADS = 16
A_HEAD_DIM = 128

B_GROUPS = ((128, 1), (512, 4), (2048, 16))
B_HEADS = 16
B_HEAD_DIM = 64
B_WIDTH = B_HEADS * B_HEAD_DIM
B_GROUP_COLS = 3 * B_WIDTH
B_QKV = len(B_GROUPS) * B_GROUP_COLS
REL_BUCKETS = 32
REL_EXACT = 8
REL_MAX_DISTANCE = 1024

LOG2E = math.log2(math.e)
LOGIT_SCALE = B_HEAD_DIM ** -0.5 * LOG2E

LANES = 128
HEAD_PAIRS = B_WIDTH // LANES
Q_BLOCK = 128
K_WINDOW = 256
STATE_INTERLEAVE = 4
ATTN_UNROLL = 8
VMEM_LIMIT = 58 * 1024 * 1024


def _rms(x, g):
    return x * lax.rsqrt(jnp.mean(x * x, axis=-1, keepdims=True) + EPS) * g


def _gmlp_kernel(x_ref, gpre_ref, gpost_ref, win_ref, ws_ref, bs_ref, vng_ref, vnb_ref, wout_ref,
                 o_ref, vn_scr, y_scr, *, tm):
    nchunk = tm // A_CHUNK
    x = x_ref[...]
    h = _rms(x, gpre_ref[...]).astype(BF16)

    v = jnp.dot(h, win_ref[:, A_WIDTH:2 * A_WIDTH], preferred_element_type=F32)
    v = jax.nn.gelu(v)
    mu = jnp.mean(v, axis=-1, keepdims=True)
    vc = v - mu
    vn = vc * lax.rsqrt(jnp.mean(vc * vc, axis=-1, keepdims=True) + EPS)
    vn_scr[...] = (vn * vng_ref[...] + vnb_ref[...]).astype(BF16)

    cb_w = 512
    heads_per_cb = cb_w // A_HEAD_DIM
    for cb in range(A_WIDTH // cb_w):
        u = jax.nn.gelu(jnp.dot(h, win_ref[:, cb * cb_w:(cb + 1) * cb_w], preferred_element_type=F32))
        g = jnp.dot(h, win_ref[:, 2 * A_WIDTH + cb * cb_w:2 * A_WIDTH + (cb + 1) * cb_w],
                    preferred_element_type=F32)
        ug = u * (g * jax.nn.sigmoid(g))
        for hh in range(heads_per_cb):
            hd = cb * heads_per_cb + hh
            c0 = hd * A_HEAD_DIM
            rhs = jnp.concatenate(
                [vn_scr[c * A_CHUNK:(c + 1) * A_CHUNK, c0:c0 + A_HEAD_DIM] for c in range(nchunk)], axis=1)
            sg = jnp.dot(ws_ref[hd], rhs, preferred_element_type=F32)
            for c in range(nchunk):
                sgc = sg[:, c * A_CHUNK:(c + 1) * A_CHUNK] + bs_ref[hd]
                yb = ug[c * A_CHUNK:(c + 1) * A_CHUNK, hh * A_HEAD_DIM:(hh + 1) * A_HEAD_DIM] * sgc
                y_scr[c * A_CHUNK:(c + 1) * A_CHUNK, c0:c0 + A_HEAD_DIM] = yb.astype(BF16)

    out = jnp.dot(y_scr[...], wout_ref[...], preferred_element_type=F32)
    o_ref[...] = x + _rms(out, gpost_ref[...])


def _gmlp_layer(x2, gpre, gpost, w_in, w_s, b_s, vn_g, vn_b, w_out, *, tm=512):
    t = x2.shape[0]
    const = lambda *shape: pl.BlockSpec(shape, lambda i: (0,) * len(shape), pipeline_mode=pl.Buffered(1))
    return pl.pallas_call(
        functools.partial(_gmlp_kernel, tm=tm),
        grid=(t // tm,),
        in_specs=[
            pl.BlockSpec((tm, D_MODEL), lambda i: (i, 0)),
            const(1, D_MODEL), const(1, D_MODEL),
            const(D_MODEL, 3 * A_WIDTH),
            const(A_HEADS, A_CHUNK, A_CHUNK),
            const(A_HEADS, A_CHUNK, A_HEAD_DIM),
            const(1, A_WIDTH), const(1, A_WIDTH),
            const(A_WIDTH, D_MODEL),
        ],
        out_specs=pl.BlockSpec((tm, D_MODEL), lambda i: (i, 0)),
        out_shape=jax.ShapeDtypeStruct((t, D_MODEL), F32),
        scratch_shapes=[pltpu.VMEM((tm, A_WIDTH), BF16), pltpu.VMEM((tm, A_WIDTH), BF16)],
        compiler_params=pltpu.CompilerParams(dimension_semantics=("arbitrary",),
                                             vmem_limit_bytes=VMEM_LIMIT),
        name="gmlp_layer",
    )(x2, gpre, gpost, w_in, w_s, b_s, vn_g, vn_b, w_out)


def _qkv_kernel(x_ref, gpre_ref, w_ref, *refs, tm, dils):
    out_refs, h_scr = refs[:len(dils)], refs[len(dils)]
    nslab = D_MODEL // LANES
    h = _rms(x_ref[...], gpre_ref[...])
    for k in range(nslab):
        h_scr[k] = h[:, k * LANES:(k + 1) * LANES]
    nc = 512

    def gathered(starts, rows, stride):
        return jnp.concatenate(
            [jnp.concatenate([h_scr[k, pl.ds(st, rows, stride=stride), :] for st in starts], axis=0)
             for k in range(nslab)], axis=1).astype(BF16)

    for gi, d in enumerate(dils):
        rows = tm // d
        hp = gathered(range(d), rows, d)
        if d == 1:
            per = Q_BLOCK // STATE_INTERLEAVE
            hq = gathered([blk * Q_BLOCK + c for blk in range(tm // Q_BLOCK) for c in range(STATE_INTERLEAVE)],
                          per, STATE_INTERLEAVE)
        else:
            hq = hp
        for c in range(B_GROUP_COLS // nc):
            c0 = gi * B_GROUP_COLS + c * nc
            is_q = (c + 1) * nc <= B_WIDTH
            z = jnp.dot(hq if is_q else hp, w_ref[:, c0:c0 + nc], preferred_element_type=F32)
            if is_q:
                z = z * LOGIT_SCALE
            out_refs[gi][:, :, c * nc:(c + 1) * nc] = z.reshape(d, rows, nc).astype(BF16)


def _qkv_projection(x3, gpre, w_qkv, *, tm=512):
    bsz, s, _ = x3.shape
    dils = tuple(d for _, d in B_GROUPS)
    return pl.pallas_call(
        functools.partial(_qkv_kernel, tm=tm, dils=dils),
        grid=(bsz, s // tm),
        in_specs=[
            pl.BlockSpec((None, tm, D_MODEL), lambda b, i: (b, i, 0)),
            pl.BlockSpec((1, D_MODEL), lambda b, i: (0, 0)),
            pl.BlockSpec((D_MODEL, B_QKV), lambda b, i: (0, 0), pipeline_mode=pl.Buffered(1)),
        ],
        out_specs=[pl.BlockSpec((None, d, tm // d, B_GROUP_COLS), lambda b, i: (b, 0, i, 0)) for d in dils],
        out_shape=[jax.ShapeDtypeStruct((bsz, d, s // d, B_GROUP_COLS), BF16) for d in dils],
        scratch_shapes=[pltpu.VMEM((D_MODEL // LANES, tm, LANES), F32)],
        compiler_params=pltpu.CompilerParams(dimension_semantics=("arbitrary", "arbitrary"),
                                             vmem_limit_bytes=VMEM_LIMIT),
        name="qkv_projection",
    )(x3, gpre, w_qkv)


def _t5_bucket_np(rel):
    half = REL_BUCKETS // 2
    ret = np.where(rel > 0, half, 0)
    n = np.abs(rel)
    nf = np.maximum(n, 1).astype(np.float32)
    large = REL_EXACT + (np.log(nf / np.float32(REL_EXACT)) / np.float32(math.log(REL_MAX_DISTANCE / REL_EXACT))
                         * np.float32(half - REL_EXACT)).astype(np.int32)
    large = np.minimum(large, half - 1)
    return (ret + np.where(n < REL_EXACT, n, large)).astype(np.int32)


def _bucket_row(dilation, half_w):
    rel = np.arange(K_WINDOW, dtype=np.int32) - half_w
    row = np.where(np.abs(rel) <= half_w, _t5_bucket_np(rel * dilation), -1).astype(np.int32)
    return np.broadcast_to(row, (8, K_WINDOW))


def _attn_kernel(tbl_ref, bkt_ref, *refs, seq, groups, unroll):
    ng = len(groups)
    z_refs, o_ref = refs[:ng], refs[ng]
    buf, sem, acc_scr, m_scr, den_scr, bm_scr, p_scr, pm_scr = refs[ng + 1:]
    per = Q_BLOCK // STATE_INTERLEAVE
    hp, b = pl.program_id(0), pl.program_id(1)
    nb = pl.num_programs(1)
    step = hp * nb + b
    nsteps = pl.num_programs(0) * nb

    def slab_copies(g, hp_, b_):
        return [pltpu.make_async_copy(
            z_refs[g].at[b_, :, pl.ds(pl.multiple_of((j * HEAD_PAIRS + hp_) * LANES, LANES), LANES)],
            buf.at[g, j], sem.at[g, j]) for j in range(3)]

    @pl.when(step == 0)
    def _():
        for cp in slab_copies(0, hp, b):
            cp.start()

    @pl.when(b == 0)
    def _():
        key = lax.broadcasted_iota(jnp.int32, (Q_BLOCK, K_WINDOW), 1)
        for g, (d, half_w, col) in enumerate(groups):
            for hh in range(2):
                def fill(bk, row, g=g, hh=hh, col=col):
                    return jnp.where(bkt_ref[g] == bk, tbl_ref[bk, col + 2 * hp + hh], row)
                row = lax.fori_loop(0, REL_BUCKETS, fill, jnp.full((8, K_WINDOW), NEG_INF, F32)) * LOG2E
                if d == 1:
                    mid = jnp.concatenate(
                        [pltpu.roll(jnp.broadcast_to(row[0:1], (per, K_WINDOW)), c, 1,
                                    stride=STATE_INTERLEAVE, stride_axis=0) for c in range(STATE_INTERLEAVE)], axis=0)
                else:
                    mid = pltpu.roll(jnp.broadcast_to(row[0:1], (Q_BLOCK, K_WINDOW)), 0, 1, stride=1, stride_axis=0)
                bm_scr[g, hh, 1] = mid
                bm_scr[g, hh, 0] = jnp.where(key < K_WINDOW - half_w,
                                             pltpu.roll(mid, K_WINDOW - half_w, 1), NEG_INF)
                bm_scr[g, hh, 2] = jnp.where(key >= half_w, pltpu.roll(mid, half_w, 1), NEG_INF)

    first_head = lax.broadcasted_iota(jnp.int32, (Q_BLOCK, LANES), 1) < B_HEAD_DIM
    ones = jnp.ones((K_WINDOW, LANES), BF16)

    def run_group(g):
        d, half_w, _ = groups[g]
        length = seq // d
        nblk = length // Q_BLOCK
        ntrips = d * nblk // unroll
        q_ref, k_ref, v_ref = buf.at[g, 0], buf.at[g, 1], buf.at[g, 2]

        def blocks(it):
            for u in range(unroll):
                if unroll >= nblk:
                    r, i = it * (unroll // nblk) + u // nblk, u % nblk
                else:
                    chunks = nblk // unroll
                    r = it // chunks if d > 1 else 0
                    i = (it - r * chunks) * unroll + u
                k0 = pl.multiple_of(r * length + jnp.clip(i * Q_BLOCK - half_w, 0, length - K_WINDOW), half_w)
                yield u, r, i, k0

        def logits_stage(it, slot):
            for u, r, i, k0 in blocks(it):
                q0 = pl.multiple_of(r * length + i * Q_BLOCK, Q_BLOCK)
                var = jnp.where(i == 0, 0, jnp.where(i == nblk - 1, 2, 1))
                qb = q_ref[pl.ds(q0, Q_BLOCK), :]
                kw = k_ref[pl.ds(k0, K_WINDOW), :]
                zero = jnp.zeros_like(qb)
                q2 = jnp.concatenate([jnp.where(first_head, qb, zero), jnp.where(first_head, zero, qb)], axis=0)
                s = lax.dot_general(q2, kw, (((1,), (1,)), ((), ())), preferred_element_type=F32)
                s = s + jnp.concatenate([bm_scr[g, 0, var], bm_scr[g, 1, var]], axis=0)
                m2 = jnp.max(s, axis=-1, keepdims=True)
                p_scr[slot, u] = jnp.exp2(s - m2).astype(BF16)
                pm_scr[slot, u] = jnp.where(first_head, m2[:Q_BLOCK], m2[Q_BLOCK:])

        def values_stage(it, slot):
            for u, r, i, k0 in blocks(it):
                vw = v_ref[pl.ds(k0, K_WINDOW), :]
                ov = jnp.dot(p_scr[slot, u], jnp.concatenate([vw, ones], axis=1), preferred_element_type=F32)
                acc = jnp.where(first_head, ov[:Q_BLOCK, :LANES], ov[Q_BLOCK:, :LANES])
                den = jnp.where(first_head, ov[:Q_BLOCK, LANES:], ov[Q_BLOCK:, LANES:])
                m = pm_scr[slot, u]
                if d == 1:
                    pieces = [(0, Q_BLOCK, pl.ds(i * Q_BLOCK, Q_BLOCK))]
                else:
                    n = Q_BLOCK // d
                    base = r // STATE_INTERLEAVE + per * (r % STATE_INTERLEAVE)
                    pieces = [(k * n, n, pl.ds((i * d + k) * Q_BLOCK + base, n, stride=d // STATE_INTERLEAVE))
                              for k in range(d)]
                gather = lambda ref: jnp.concatenate([ref[idx, :] for _, _, idx in pieces], axis=0)
                if g > 0:
                    m_old = gather(m_scr)
                    m_new = jnp.maximum(m_old, m)
                    w_old, w_cur = jnp.exp2(m_old - m_new), jnp.exp2(m - m_new)
                    acc = w_old * gather(acc_scr) + w_cur * acc
                    den = w_old * gather(den_scr) + w_cur * den
                    m = m_new
                if g == ng - 1:
                    o = acc / den
                    for c in range(STATE_INTERLEAVE):
                        o_ref[pl.ds(i * Q_BLOCK + c, per, stride=STATE_INTERLEAVE), :] = o[c * per:(c + 1) * per]
                else:
                    for ref, val in ((acc_scr, acc), (m_scr, m), (den_scr, den)):
                        for r0, n, idx in pieces:
                            ref[idx, :] = val[r0:r0 + n]

        logits_stage(0, 0)

        def pair(j, carry):
            logits_stage(2 * j + 1, 1)
            values_stage(2 * j, 0)
            logits_stage(2 * j + 2, 0)
            values_stage(2 * j + 1, 1)
            return carry

        lax.fori_loop(0, ntrips // 2 - 1, pair, 0)
        logits_stage(ntrips - 1, 1)
        values_stage(ntrips - 2, 0)
        values_stage(ntrips - 1, 1)

    for g in range(ng):
        if g + 1 < ng:
            for cp in slab_copies(g + 1, hp, b):
                cp.start()
        else:
            @pl.when(step + 1 < nsteps)
            def _():
                nxt = step + 1
                for cp in slab_copies(0, nxt // nb, nxt % nb):
                    cp.start()
        for cp in slab_copies(g, hp, b):
            cp.wait()
        run_group(g)


def _window_attention(zs, rel_bias):
    bsz, s, _ = zs[0].shape
    order = sorted(range(len(B_GROUPS)), key=lambda gi: -B_GROUPS[gi][1])
    groups = tuple((B_GROUPS[gi][1], B_GROUPS[gi][0] // (2 * B_GROUPS[gi][1]), gi * B_HEADS) for gi in order)
    zs = [zs[gi] for gi in order]
    assert groups[-1][0] == 1
    for d, half_w, _ in groups:
        nblk = s // d // Q_BLOCK
        assert K_WINDOW == Q_BLOCK + 2 * half_w and s // d >= K_WINDOW and s % (d * Q_BLOCK) == 0
        assert (d * nblk) % (2 * ATTN_UNROLL) == 0 and (ATTN_UNROLL % nblk == 0 or nblk % ATTN_UNROLL == 0)
        assert d == 1 or (d % STATE_INTERLEAVE == 0 and Q_BLOCK % d == 0)
    bkt = jnp.asarray(np.stack([_bucket_row(d, half_w) for d, half_w, _ in groups]))
    return pl.pallas_call(
        functools.partial(_attn_kernel, seq=s, groups=groups, unroll=ATTN_UNROLL),
        grid=(HEAD_PAIRS, bsz),
        in_specs=[pl.BlockSpec(memory_space=pltpu.SMEM),
                  pl.BlockSpec(bkt.shape, lambda hp, b: (0, 0, 0))]
                 + [pl.BlockSpec(memory_space=pl.ANY)] * len(groups),
        out_specs=pl.BlockSpec((None, s, LANES), lambda hp, b: (b, 0, hp)),
        out_shape=jax.ShapeDtypeStruct((bsz, s, B_WIDTH), F32),
        scratch_shapes=[pltpu.VMEM((len(groups), 3, s, LANES), BF16),
                        pltpu.SemaphoreType.DMA((len(groups), 3)),
                        pltpu.VMEM((s, LANES), F32),
                        pltpu.VMEM((s, LANES), F32),
                        pltpu.VMEM((s, LANES), F32),
                        pltpu.VMEM((len(groups), 2, 3, Q_BLOCK, K_WINDOW), F32),
                        pltpu.VMEM((2, ATTN_UNROLL, 2 * Q_BLOCK, K_WINDOW), BF16),
                        pltpu.VMEM((2, ATTN_UNROLL, Q_BLOCK, LANES), F32)],
        compiler_params=pltpu.CompilerParams(dimension_semantics=("arbitrary", "arbitrary"),
                                             vmem_limit_bytes=VMEM_LIMIT),
        name="window_attention",
    )(rel_bias.astype(F32), bkt, *zs)


def _out_kernel(x_ref, o_ref, gpre_ref, gpost_ref, wg_ref, wout_ref, out_ref):
    x = x_ref[...]
    h = _rms(x, gpre_ref[...]).astype(BF16)
    gate = jnp.dot(h, wg_ref[...], preferred_element_type=F32)
    y = (o_ref[...] * (gate * jax.nn.sigmoid(gate))).astype(BF16)
    out = jnp.dot(y, wout_ref[...], preferred_element_type=F32)
    out_ref[...] = x + _rms(out, gpost_ref[...])


def _gate_output(x2, o2, gpre, gpost, w_gate, w_out, *, tm=512):
    t = x2.shape[0]
    const = lambda *shape: pl.BlockSpec(shape, lambda i: (0,) * len(shape), pipeline_mode=pl.Buffered(1))
    row = pl.BlockSpec((tm, D_MODEL), lambda i: (i, 0))
    return pl.pallas_call(
        _out_kernel,
        grid=(t // tm,),
        in_specs=[row, row, const(1, D_MODEL), const(1, D_MODEL),
                  const(D_MODEL, B_WIDTH), const(B_WIDTH, D_MODEL)],
        out_specs=row,
        out_shape=jax.ShapeDtypeStruct((t, D_MODEL), F32),
        compiler_params=pltpu.CompilerParams(dimension_semantics=("arbitrary",),
                                             vmem_limit_bytes=VMEM_LIMIT),
        name="gate_output",
    )(x2, o2, gpre, gpost, w_gate, w_out)


def kernel(x, norm_pre, norm_post, a_w_in, a_w_s, a_b_s, a_vnorm_g, a_vnorm_b, a_w_out, b_w_in, b_w_out, rel_bias):
    bsz, s, dm = x.shape
    assert dm == D_MODEL and norm_pre.shape[0] == 2 and a_w_in.shape[0] == 1 and b_w_in.shape[0] == 1

    b_s = jnp.broadcast_to(a_b_s[0][:, :, None], (A_HEADS, A_CHUNK, A_HEAD_DIM))
    x2 = _gmlp_layer(x.reshape(bsz * s, dm), norm_pre[0:1], norm_post[0:1],
                     a_w_in[0].astype(BF16), a_w_s[0].astype(BF16), b_s,
                     a_vnorm_g, a_vnorm_b, a_w_out[0].astype(BF16))

    w_in = b_w_in[0].astype(BF16)
    zs = _qkv_projection(x2.reshape(bsz, s, dm), norm_pre[1:2], w_in[:, :B_QKV])
    o = _window_attention([z.reshape(bsz, s, B_GROUP_COLS) for z in zs], rel_bias)
    out = _gate_output(x2, o.reshape(bsz * s, B_WIDTH), norm_pre[1:2], norm_post[1:2],
                       w_in[:, B_QKV:], b_w_out[0].astype(BF16))
    return out.reshape(bsz, s, dm)
```

```python
import functools
import math

import jax
import jax.numpy as jnp
import numpy as np
from jax import lax
from jax.experimental import pallas as pl
from jax.experimental.pallas import tpu as pltpu

F32 = jnp.float32
BF16 = jnp.bfloat16

EPS = 1e-6
NEG_INF = -1e30

D_MODEL = 1024
A_WIDTH = 2048
A_CHUNK = 128
A_HEADS = 16
A_HEAD_DIM = 128

B_GROUPS = ((128, 1), (512, 4), (2048, 16))
B_HEADS = 16
B_HEAD_DIM = 64
B_WIDTH = B_HEADS * B_HEAD_DIM
B_GROUP_COLS = 3 * B_WIDTH
B_QKV = len(B_GROUPS) * B_GROUP_COLS
REL_BUCKETS = 32
REL_EXACT = 8
REL_MAX_DISTANCE = 1024

LOG2E = math.log2(math.e)
LOGIT_SCALE = B_HEAD_DIM ** -0.5 * LOG2E

LANES = 128
COL_CHUNK = 512
HEAD_PAIRS = B_WIDTH // LANES
Q_BLOCK = 128
K_WINDOW = 256
STATE_INTERLEAVE = 4
ATTN_UNROLL = 8
VMEM_LIMIT = 58 * 1024 * 1024


def _rms(x, g):
    return x * lax.rsqrt(jnp.mean(x * x, axis=-1, keepdims=True) + EPS) * g


_GELU_B = -2.0 * math.sqrt(2.0 / math.pi) * LOG2E
_GELU_A = 0.044715 * _GELU_B


def _gelu_denominator(x):
    return 1.0 + jnp.exp2(x * (_GELU_A * (x * x) + _GELU_B))


def _silu_denominator(x):
    return 1.0 + jnp.exp2(x * -LOG2E)


def _gmlp_kernel(x_ref, gpre_ref, gpost_ref, win_ref, ws_ref, bs_ref, vng_ref, vnb_ref, wout_ref,
                 o_ref, vn_scr, ug_scr, y_scr, *, tm):
    nchunk = tm // A_CHUNK
    x = x_ref[...]
    h = _rms(x, gpre_ref[...]).astype(BF16)

    v = jnp.dot(h, win_ref[:, A_WIDTH:2 * A_WIDTH], preferred_element_type=F32)
    v = v / _gelu_denominator(v)
    mu = jnp.mean(v, axis=-1, keepdims=True)
    vc = v - mu
    vn = vc * lax.rsqrt(jnp.mean(vc * vc, axis=-1, keepdims=True) + EPS)
    vn_scr[...] = (vn * vng_ref[...] + vnb_ref[...]).astype(BF16)

    for cb in range(A_WIDTH // COL_CHUNK):
        lo, hi = cb * COL_CHUNK, (cb + 1) * COL_CHUNK
        u = jnp.dot(h, win_ref[:, lo:hi], preferred_element_type=F32)
        g = jnp.dot(h, win_ref[:, 2 * A_WIDTH + lo:2 * A_WIDTH + hi], preferred_element_type=F32)
        ug_scr[:, lo:hi] = (u * g) / (_gelu_denominator(u) * _silu_denominator(g))

    for hd in range(A_HEADS):
        c0 = hd * A_HEAD_DIM
        rhs = jnp.concatenate(
            [vn_scr[c * A_CHUNK:(c + 1) * A_CHUNK, c0:c0 + A_HEAD_DIM] for c in range(nchunk)], axis=1)
        sg = jnp.dot(ws_ref[hd], rhs, preferred_element_type=F32)
        for c in range(nchunk):
            rows = slice(c * A_CHUNK, (c + 1) * A_CHUNK)
            sgc = sg[:, c * A_CHUNK:(c + 1) * A_CHUNK] + bs_ref[hd]
            y_scr[rows, c0:c0 + A_HEAD_DIM] = (ug_scr[rows, c0:c0 + A_HEAD_DIM] * sgc).astype(BF16)

    out = jnp.dot(y_scr[...], wout_ref[...], preferred_element_type=F32)
    o_ref[...] = x + _rms(out, gpost_ref[...])


def _gmlp_layer(x2, gpre, gpost, w_in, w_s, b_s, vn_g, vn_b, w_out, *, tm=512):
    t = x2.shape[0]
    const = lambda *shape: pl.BlockSpec(shape, lambda i: (0,) * len(shape), pipeline_mode=pl.Buffered(1))
    return pl.pallas_call(
        functools.partial(_gmlp_kernel, tm=tm),
        grid=(t // tm,),
        in_specs=[
            pl.BlockSpec((tm, D_MODEL), lambda i: (i, 0)),
            const(1, D_MODEL), const(1, D_MODEL),
            const(D_MODEL, 3 * A_WIDTH),
            const(A_HEADS, A_CHUNK, A_CHUNK),
            const(A_HEADS, A_CHUNK, A_HEAD_DIM),
            const(1, A_WIDTH), const(1, A_WIDTH),
            const(A_WIDTH, D_MODEL),
        ],
        out_specs=pl.BlockSpec((tm, D_MODEL), lambda i: (i, 0)),
        out_shape=jax.ShapeDtypeStruct((t, D_MODEL), F32),
        scratch_shapes=[pltpu.VMEM((tm, A_WIDTH), BF16), pltpu.VMEM((tm, A_WIDTH), F32),
                        pltpu.VMEM((tm, A_WIDTH), BF16)],
        compiler_params=pltpu.CompilerParams(dimension_semantics=("arbitrary",),
                                             vmem_limit_bytes=VMEM_LIMIT),
        name="gmlp_layer",
    )(x2, gpre, gpost, w_in, w_s, b_s, vn_g, vn_b, w_out)


def _qkv_kernel(x_ref, gpre_ref, w_ref, *refs, tm, dils):
    out_refs, h_scr = refs[:len(dils)], refs[len(dils)]
    nslab = D_MODEL // LANES
    h = _rms(x_ref[...], gpre_ref[...])
    for k in range(nslab):
        h_scr[k] = h[:, k * LANES:(k + 1) * LANES]
    nc = 512

    def gathered(starts, rows, stride):
        return jnp.concatenate(
            [jnp.concatenate([h_scr[k, pl.ds(st, rows, stride=stride), :] for st in starts], axis=0)
             for k in range(nslab)], axis=1).astype(BF16)

    for gi, d in enumerate(dils):
        rows = tm // d
        hp = gathered(range(d), rows, d)
        if d == 1:
            per = Q_BLOCK // STATE_INTERLEAVE
            hq = gathered([blk * Q_BLOCK + c for blk in range(tm // Q_BLOCK) for c in range(STATE_INTERLEAVE)],
                          per, STATE_INTERLEAVE)
        else:
            hq = hp
        for c in range(B_GROUP_COLS // nc):
            c0 = gi * B_GROUP_COLS + c * nc
            is_q = (c + 1) * nc <= B_WIDTH
            z = jnp.dot(hq if is_q else hp, w_ref[:, c0:c0 + nc], preferred_element_type=F32)
            if is_q:
                z = z * LOGIT_SCALE
            out_refs[gi][:, :, c * nc:(c + 1) * nc] = z.reshape(d, rows, nc).astype(BF16)


def _qkv_projection(x3, gpre, w_qkv, *, tm=512):
    bsz, s, _ = x3.shape
    dils = tuple(d for _, d in B_GROUPS)
    return pl.pallas_call(
        functools.partial(_qkv_kernel, tm=tm, dils=dils),
        grid=(bsz, s // tm),
        in_specs=[
            pl.BlockSpec((None, tm, D_MODEL), lambda b, i: (b, i, 0)),
            pl.BlockSpec((1, D_MODEL), lambda b, i: (0, 0)),
            pl.BlockSpec((D_MODEL, B_QKV), lambda b, i: (0, 0), pipeline_mode=pl.Buffered(1)),
        ],
        out_specs=[pl.BlockSpec((None, d, tm // d, B_GROUP_COLS), lambda b, i: (b, 0, i, 0)) for d in dils],
        out_shape=[jax.ShapeDtypeStruct((bsz, d, s // d, B_GROUP_COLS), BF16) for d in dils],
        scratch_shapes=[pltpu.VMEM((D_MODEL // LANES, tm, LANES), F32)],
        compiler_params=pltpu.CompilerParams(dimension_semantics=("arbitrary", "arbitrary"),
                                             vmem_limit_bytes=VMEM_LIMIT),
        name="qkv_projection",
    )(x3, gpre, w_qkv)


def _t5_bucket_np(rel):
    half = REL_BUCKETS // 2
    ret = np.where(rel > 0, half, 0)
    n = np.abs(rel)
    nf = np.maximum(n, 1).astype(np.float32)
    large = REL_EXACT + (np.log(nf / np.float32(REL_EXACT)) / np.float32(math.log(REL_MAX_DISTANCE / REL_EXACT))
                         * np.float32(half - REL_EXACT)).astype(np.int32)
    large = np.minimum(large, half - 1)
    return (ret + np.where(n < REL_EXACT, n, large)).astype(np.int32)


def _bucket_row(dilation, half_w):
    rel = np.arange(K_WINDOW, dtype=np.int32) - half_w
    row = np.where(np.abs(rel) <= half_w, _t5_bucket_np(rel * dilation), -1).astype(np.int32)
    return np.broadcast_to(row, (8, K_WINDOW))


def _attn_kernel(tbl_ref, bkt_ref, *refs, seq, groups, unroll):
    ng = len(groups)
    z_refs, o_ref = refs[:ng], refs[ng]
    buf, sem, acc_scr, m_scr, den_scr, bm_scr, p_scr, pm_scr = refs[ng + 1:]
    per = Q_BLOCK // STATE_INTERLEAVE
    hp, b = pl.program_id(0), pl.program_id(1)
    nb = pl.num_programs(1)
    step = hp * nb + b
    nsteps = pl.num_programs(0) * nb

    def slab_copies(g, hp_, b_):
        return [pltpu.make_async_copy(
            z_refs[g].at[b_, :, pl.ds(pl.multiple_of((j * HEAD_PAIRS + hp_) * LANES, LANES), LANES)],
            buf.at[g, j], sem.at[g, j]) for j in range(3)]

    @pl.when(step == 0)
    def _():
        for cp in slab_copies(0, hp, b):
            cp.start()

    @pl.when(b == 0)
    def _():
        key = lax.broadcasted_iota(jnp.int32, (Q_BLOCK, K_WINDOW), 1)
        for g, (d, half_w, col) in enumerate(groups):
            for hh in range(2):
                def fill(bk, row, g=g, hh=hh, col=col):
                    return jnp.where(bkt_ref[g] == bk, tbl_ref[bk, col + 2 * hp + hh], row)
                row = lax.fori_loop(0, REL_BUCKETS, fill, jnp.full((8, K_WINDOW), NEG_INF, F32)) * LOG2E
                if d == 1:
                    mid = jnp.concatenate(
                        [pltpu.roll(jnp.broadcast_to(row[0:1], (per, K_WINDOW)), c, 1,
                                    stride=STATE_INTERLEAVE, stride_axis=0) for c in range(STATE_INTERLEAVE)], axis=0)
                else:
                    mid = pltpu.roll(jnp.broadcast_to(row[0:1], (Q_BLOCK, K_WINDOW)), 0, 1, stride=1, stride_axis=0)
                bm_scr[g, hh, 1] = mid
                bm_scr[g, hh, 0] = jnp.where(key < K_WINDOW - half_w,
                                             pltpu.roll(mid, K_WINDOW - half_w, 1), NEG_INF)
                bm_scr[g, hh, 2] = jnp.where(key >= half_w, pltpu.roll(mid, half_w, 1), NEG_INF)

    first_head = lax.broadcasted_iota(jnp.int32, (Q_BLOCK, LANES), 1) < B_HEAD_DIM
    ones = jnp.ones((K_WINDOW, LANES), BF16)

    def run_group(g):
        d, half_w, _ = groups[g]
        length = seq // d
        nblk = length // Q_BLOCK
        ntrips = d * nblk // unroll
        q_ref, k_ref, v_ref = buf.at[g, 0], buf.at[g, 1], buf.at[g, 2]

        def blocks(it):
            for u in range(unroll):
                if unroll >= nblk:
                    r, i = it * (unroll // nblk) + u // nblk, u % nblk
                else:
                    chunks = nblk // unroll
                    r = it // chunks if d > 1 else 0
                    i = (it - r * chunks) * unroll + u
                k0 = pl.multiple_of(r * length + jnp.clip(i * Q_BLOCK - half_w, 0, length - K_WINDOW), half_w)
                yield u, r, i, k0

        def logits_stage(it, slot):
            for u, r, i, k0 in blocks(it):
                q0 = pl.multiple_of(r * length + i * Q_BLOCK, Q_BLOCK)
                var = jnp.where(i == 0, 0, jnp.where(i == nblk - 1, 2, 1))
                qb = q_ref[pl.ds(q0, Q_BLOCK), :]
                kw = k_ref[pl.ds(k0, K_WINDOW), :]
                zero = jnp.zeros_like(qb)
                q2 = jnp.concatenate([jnp.where(first_head, qb, zero), jnp.where(first_head, zero, qb)], axis=0)
                s = lax.dot_general(q2, kw, (((1,), (1,)), ((), ())), preferred_element_type=F32)
                s = s + jnp.concatenate([bm_scr[g, 0, var], bm_scr[g, 1, var]], axis=0)
                m2 = jnp.max(s, axis=-1, keepdims=True)
                p_scr[slot, u] = jnp.exp2(s - m2).astype(BF16)
                pm_scr[slot, u] = jnp.where(first_head, m2[:Q_BLOCK], m2[Q_BLOCK:])

        def values_stage(it, slot):
            for u, r, i, k0 in blocks(it):
                vw = v_ref[pl.ds(k0, K_WINDOW), :]
                ov = jnp.dot(p_scr[slot, u], jnp.concatenate([vw, ones], axis=1), preferred_element_type=F32)
                acc = jnp.where(first_head, ov[:Q_BLOCK, :LANES], ov[Q_BLOCK:, :LANES])
                den = jnp.where(first_head, ov[:Q_BLOCK, LANES:], ov[Q_BLOCK:, LANES:])
                m = pm_scr[slot, u]
                if d == 1:
                    pieces = [(0, Q_BLOCK, pl.ds(i * Q_BLOCK, Q_BLOCK))]
                else:
                    n = Q_BLOCK // d
                    base = r // STATE_INTERLEAVE + per * (r % STATE_INTERLEAVE)
                    pieces = [(k * n, n, pl.ds((i * d + k) * Q_BLOCK + base, n, stride=d // STATE_INTERLEAVE))
                              for k in range(d)]
                gather = lambda ref: jnp.concatenate([ref[idx, :] for _, _, idx in pieces], axis=0)
                if g > 0:
                    m_old = gather(m_scr)
                    m_new = jnp.maximum(m_old, m)
                    w_old, w_cur = jnp.exp2(m_old - m_new), jnp.exp2(m - m_new)
                    acc = w_old * gather(acc_scr) + w_cur * acc
                    den = w_old * gather(den_scr) + w_cur * den
                    m = m_new
                if g == ng - 1:
                    o = acc / den
                    for c in range(STATE_INTERLEAVE):
                        o_ref[pl.ds(i * Q_BLOCK + c, per, stride=STATE_INTERLEAVE), :] = o[c * per:(c + 1) * per]
                else:
                    for ref, val in ((acc_scr, acc), (m_scr, m), (den_scr, den)):
                        for r0, n, idx in pieces:
                            ref[idx, :] = val[r0:r0 + n]

        logits_stage(0, 0)

        def pair(j, carry):
            logits_stage(2 * j + 1, 1)
            values_stage(2 * j, 0)
            logits_stage(2 * j + 2, 0)
            values_stage(2 * j + 1, 1)
            return carry

        lax.fori_loop(0, ntrips // 2 - 1, pair, 0)
        logits_stage(ntrips - 1, 1)
        values_stage(ntrips - 2, 0)
        values_stage(ntrips - 1, 1)

    for g in range(ng):
        if g + 1 < ng:
            for cp in slab_copies(g + 1, hp, b):
                cp.start()
        else:
            @pl.when(step + 1 < nsteps)
            def _():
                nxt = step + 1
                for cp in slab_copies(0, nxt // nb, nxt % nb):
                    cp.start()
        for cp in slab_copies(g, hp, b):
            cp.wait()
        run_group(g)


def _window_attention(zs, rel_bias):
    bsz, s, _ = zs[0].shape
    order = sorted(range(len(B_GROUPS)), key=lambda gi: -B_GROUPS[gi][1])
    groups = tuple((B_GROUPS[gi][1], B_GROUPS[gi][0] // (2 * B_GROUPS[gi][1]), gi * B_HEADS) for gi in order)
    zs = [zs[gi] for gi in order]
    assert groups[-1][0] == 1
    for d, half_w, _ in groups:
        nblk = s // d // Q_BLOCK
        assert K_WINDOW == Q_BLOCK + 2 * half_w and s // d >= K_WINDOW and s % (d * Q_BLOCK) == 0
        assert (d * nblk) % (2 * ATTN_UNROLL) == 0 and (ATTN_UNROLL % nblk == 0 or nblk % ATTN_UNROLL == 0)
        assert d == 1 or (d % STATE_INTERLEAVE == 0 and Q_BLOCK % d == 0)
    bkt = jnp.asarray(np.stack([_bucket_row(d, half_w) for d, half_w, _ in groups]))
    return pl.pallas_call(
        functools.partial(_attn_kernel, seq=s, groups=groups, unroll=ATTN_UNROLL),
        grid=(HEAD_PAIRS, bsz),
        in_specs=[pl.BlockSpec(memory_space=pltpu.SMEM),
                  pl.BlockSpec(bkt.shape, lambda hp, b: (0, 0, 0))]
                 + [pl.BlockSpec(memory_space=pl.ANY)] * len(groups),
        out_specs=pl.BlockSpec((None, s, LANES), lambda hp, b: (b, 0, hp)),
        out_shape=jax.ShapeDtypeStruct((bsz, s, B_WIDTH), F32),
        scratch_shapes=[pltpu.VMEM((len(groups), 3, s, LANES), BF16),
                        pltpu.SemaphoreType.DMA((len(groups), 3)),
                        pltpu.VMEM((s, LANES), F32),
                        pltpu.VMEM((s, LANES), F32),
                        pltpu.VMEM((s, LANES), F32),
                        pltpu.VMEM((len(groups), 2, 3, Q_BLOCK, K_WINDOW), F32),
                        pltpu.VMEM((2, ATTN_UNROLL, 2 * Q_BLOCK, K_WINDOW), BF16),
                        pltpu.VMEM((2, ATTN_UNROLL, Q_BLOCK, LANES), F32)],
        compiler_params=pltpu.CompilerParams(dimension_semantics=("arbitrary", "arbitrary"),
                                             vmem_limit_bytes=VMEM_LIMIT),
        name="window_attention",
    )(rel_bias.astype(F32), bkt, *zs)


def _out_kernel(x_ref, o_ref, gpre_ref, gpost_ref, wg_ref, wout_ref, out_ref):
    x = x_ref[...]
    h = _rms(x, gpre_ref[...]).astype(BF16)
    gate = jnp.dot(h, wg_ref[...], preferred_element_type=F32)
    y = (o_ref[...] * gate / _silu_denominator(gate)).astype(BF16)
    out = jnp.dot(y, wout_ref[...], preferred_element_type=F32)
    out_ref[...] = x + _rms(out, gpost_ref[...])


def _gate_output(x2, o2, gpre, gpost, w_gate, w_out, *, tm=512):
    t = x2.shape[0]
    const = lambda *shape: pl.BlockSpec(shape, lambda i: (0,) * len(shape), pipeline_mode=pl.Buffered(1))
    row = pl.BlockSpec((tm, D_MODEL), lambda i: (i, 0))
    return pl.pallas_call(
        _out_kernel,
        grid=(t // tm,),
        in_specs=[row, row, const(1, D_MODEL), const(1, D_MODEL),
                  const(D_MODEL, B_WIDTH), const(B_WIDTH, D_MODEL)],
        out_specs=row,
        out_shape=jax.ShapeDtypeStruct((t, D_MODEL), F32),
        compiler_params=pltpu.CompilerParams(dimension_semantics=("arbitrary",),
                                             vmem_limit_bytes=VMEM_LIMIT),
        name="gate_output",
    )(x2, o2, gpre, gpost, w_gate, w_out)


def kernel(x, norm_pre, norm_post, a_w_in, a_w_s, a_b_s, a_vnorm_g, a_vnorm_b, a_w_out, b_w_in, b_w_out, rel_bias):
    bsz, s, dm = x.shape
    assert dm == D_MODEL and norm_pre.shape[0] == 2 and a_w_in.shape[0] == 1 and b_w_in.shape[0] == 1

    b_s = jnp.broadcast_to(a_b_s[0][:, :, None], (A_HEADS, A_CHUNK, A_HEAD_DIM))
    x2 = _gmlp_layer(x.reshape(bsz * s, dm), norm_pre[0:1], norm_post[0:1],
                     a_w_in[0].astype(BF16), a_w_s[0].astype(BF16), b_s,
                     a_vnorm_g, a_vnorm_b, a_w_out[0].astype(BF16))

    w_in = b_w_in[0].astype(BF16)
    zs = _qkv_projection(x2.reshape(bsz, s, dm), norm_pre[1:2], w_in[:, :B_QKV])
    o = _window_attention([z.reshape(bsz, s, B_GROUP_COLS) for z in zs], rel_bias)
    out = _gate_output(x2, o.reshape(bsz * s, B_WIDTH), norm_pre[1:2], norm_post[1:2],
                       w_in[:, B_QKV:], b_w_out[0].astype(BF16))
    return out.reshape(bsz, s, dm)
```

```python
import functools
import math

import jax
import jax.numpy as jnp
import numpy as np
from jax import lax
from jax.experimental import pallas as pl
from jax.experimental.pallas import tpu as pltpu

F32 = jnp.float32
BF16 = jnp.bfloat16

EPS = 1e-6
NEG_INF = -1e30

D_MODEL = 1024
A_WIDTH = 2048
A_CHUNK = 128
A_HEADS = 16
A_HEAD_DIM = 128

B_GROUPS = ((128, 1), (512, 4), (2048, 16))
B_HEADS = 16
B_HEAD_DIM = 64
B_WIDTH = B_HEADS * B_HEAD_DIM
B_GROUP_COLS = 3 * B_WIDTH
B_QKV = len(B_GROUPS) * B_GROUP_COLS
REL_BUCKETS = 32
REL_EXACT = 8
REL_MAX_DISTANCE = 1024

LOG2E = math.log2(math.e)
LOGIT_SCALE = B_HEAD_DIM ** -0.5 * LOG2E

LANES = 128
COL_CHUNK = 512
CAST_ROWS = 512
CAST_ROWS_WIDE = 64
HEAD_PAIRS = B_WIDTH // LANES
Q_BLOCK = 128
K_WINDOW = 256
STATE_INTERLEAVE = 4
ATTN_UNROLL = 8
VMEM_LIMIT = 58 * 1024 * 1024


def _rms(x, g):
    return x * lax.rsqrt(jnp.mean(x * x, axis=-1, keepdims=True) + EPS) * g


def _cast_rows_once(src, dst, stage, sem, *, chunk):
    nchunk = dst.shape[0] // chunk
    assert dst.shape[0] % chunk == 0 and stage.shape[0] == 2

    def copy(i):
        return pltpu.make_async_copy(src.at[pl.ds(i * chunk, chunk), :], stage.at[i % 2], sem.at[i % 2])

    copy(0).start()
    for i in range(nchunk):
        if i + 1 < nchunk:
            copy(i + 1).start()
        copy(i).wait()
        dst[pl.ds(i * chunk, chunk), :] = stage[i % 2].astype(BF16)


_GELU_B = -2.0 * math.sqrt(2.0 / math.pi) * LOG2E
_GELU_A = 0.044715 * _GELU_B


def _gelu_denominator(x):
    return 1.0 + jnp.exp2(x * (_GELU_A * (x * x) + _GELU_B))


def _silu_denominator(x):
    return 1.0 + jnp.exp2(x * -LOG2E)


def _gmlp_kernel(x_ref, gpre_ref, gpost_ref, win_hbm, ws_ref, bs_ref, vng_ref, vnb_ref, wout_hbm,
                 o_ref, win_ref, wout_ref, stage_in, stage_out, sem, vn_scr, ug_scr, y_scr, *, tm):
    nchunk = tm // A_CHUNK

    @pl.when(pl.program_id(0) == 0)
    def _():
        _cast_rows_once(win_hbm.at[0], win_ref, stage_in, sem.at[0], chunk=stage_in.shape[1])
        _cast_rows_once(wout_hbm.at[0], wout_ref, stage_out, sem.at[1], chunk=stage_out.shape[1])

    x = x_ref[...]
    h = _rms(x, gpre_ref[...]).astype(BF16)

    v = jnp.dot(h, win_ref[:, A_WIDTH:2 * A_WIDTH], preferred_element_type=F32)
    v = v / _gelu_denominator(v)
    mu = jnp.mean(v, axis=-1, keepdims=True)
    vc = v - mu
    vn = vc * lax.rsqrt(jnp.mean(vc * vc, axis=-1, keepdims=True) + EPS)
    vn_scr[...] = (vn * vng_ref[...] + vnb_ref[...]).astype(BF16)

    for cb in range(A_WIDTH // COL_CHUNK):
        lo, hi = cb * COL_CHUNK, (cb + 1) * COL_CHUNK
        u = jnp.dot(h, win_ref[:, lo:hi], preferred_element_type=F32)
        g = jnp.dot(h, win_ref[:, 2 * A_WIDTH + lo:2 * A_WIDTH + hi], preferred_element_type=F32)
        ug_scr[:, lo:hi] = (u * g) / (_gelu_denominator(u) * _silu_denominator(g))

    for hd in range(A_HEADS):
        c0 = hd * A_HEAD_DIM
        rhs = jnp.concatenate(
            [vn_scr[c * A_CHUNK:(c + 1) * A_CHUNK, c0:c0 + A_HEAD_DIM] for c in range(nchunk)], axis=1)
        sg = jnp.dot(ws_ref[hd].astype(BF16), rhs, preferred_element_type=F32)
        for c in range(nchunk):
            rows = slice(c * A_CHUNK, (c + 1) * A_CHUNK)
            sgc = sg[:, c * A_CHUNK:(c + 1) * A_CHUNK] + bs_ref[hd]
            y_scr[rows, c0:c0 + A_HEAD_DIM] = (ug_scr[rows, c0:c0 + A_HEAD_DIM] * sgc).astype(BF16)

    out = jnp.dot(y_scr[...], wout_ref[...], preferred_element_type=F32)
    o_ref[...] = x + _rms(out, gpost_ref[...])


def _gmlp_layer(x2, gpre, gpost, w_in, w_s, b_s, vn_g, vn_b, w_out, *, tm=512):
    t = x2.shape[0]
    const = lambda *shape: pl.BlockSpec(shape, lambda i: (0,) * len(shape), pipeline_mode=pl.Buffered(1))
    return pl.pallas_call(
        functools.partial(_gmlp_kernel, tm=tm),
        grid=(t // tm,),
        in_specs=[
            pl.BlockSpec((tm, D_MODEL), lambda i: (i, 0)),
            const(1, D_MODEL), const(1, D_MODEL),
            pl.BlockSpec(memory_space=pl.ANY),
            const(A_HEADS, A_CHUNK, A_CHUNK),
            const(A_HEADS, A_CHUNK, A_HEAD_DIM),
            const(1, A_WIDTH), const(1, A_WIDTH),
            pl.BlockSpec(memory_space=pl.ANY),
        ],
        out_specs=pl.BlockSpec((tm, D_MODEL), lambda i: (i, 0)),
        out_shape=jax.ShapeDtypeStruct((t, D_MODEL), F32),
        scratch_shapes=[pltpu.VMEM((D_MODEL, 3 * A_WIDTH), BF16), pltpu.VMEM((A_WIDTH, D_MODEL), BF16),
                        pltpu.VMEM((2, CAST_ROWS_WIDE, 3 * A_WIDTH), F32),
                        pltpu.VMEM((2, CAST_ROWS, D_MODEL), F32),
                        pltpu.SemaphoreType.DMA((2, 2)),
                        pltpu.VMEM((tm, A_WIDTH), BF16), pltpu.VMEM((tm, A_WIDTH), F32),
                        pltpu.VMEM((tm, A_WIDTH), BF16)],
        compiler_params=pltpu.CompilerParams(dimension_semantics=("arbitrary",),
                                             vmem_limit_bytes=VMEM_LIMIT),
        name="gmlp_layer",
    )(x2, gpre, gpost, w_in, w_s, b_s, vn_g, vn_b, w_out)


def _qkv_kernel(x_ref, gpre_ref, w_hbm, *refs, tm, dils):
    out_refs = refs[:len(dils)]
    h_scr, w_ref, stage, sem = refs[len(dils):]
    nslab = D_MODEL // LANES

    @pl.when((pl.program_id(0) == 0) & (pl.program_id(1) == 0))
    def _():
        _cast_rows_once(w_hbm.at[0, :, pl.ds(0, B_QKV)], w_ref, stage, sem, chunk=stage.shape[1])

    h = _rms(x_ref[...], gpre_ref[...])
    for k in range(nslab):
        h_scr[k] = h[:, k * LANES:(k + 1) * LANES]
    nc = 512

    def gathered(starts, rows, stride):
        return jnp.concatenate(
            [jnp.concatenate([h_scr[k, pl.ds(st, rows, stride=stride), :] for st in starts], axis=0)
             for k in range(nslab)], axis=1).astype(BF16)

    for gi, d in enumerate(dils):
        rows = tm // d
        hp = gathered(range(d), rows, d)
        if d == 1:
            per = Q_BLOCK // STATE_INTERLEAVE
            hq = gathered([blk * Q_BLOCK + c for blk in range(tm // Q_BLOCK) for c in range(STATE_INTERLEAVE)],
                          per, STATE_INTERLEAVE)
        else:
            hq = hp
        for c in range(B_GROUP_COLS // nc):
            c0 = gi * B_GROUP_COLS + c * nc
            is_q = (c + 1) * nc <= B_WIDTH
            z = jnp.dot(hq if is_q else hp, w_ref[:, c0:c0 + nc], preferred_element_type=F32)
            if is_q:
                z = z * LOGIT_SCALE
            out_refs[gi][:, :, c * nc:(c + 1) * nc] = z.reshape(d, rows, nc).astype(BF16)


def _qkv_projection(x3, gpre, w_qkv, *, tm=512):
    bsz, s, _ = x3.shape
    dils = tuple(d for _, d in B_GROUPS)
    return pl.pallas_call(
        functools.partial(_qkv_kernel, tm=tm, dils=dils),
        grid=(bsz, s // tm),
        in_specs=[
            pl.BlockSpec((None, tm, D_MODEL), lambda b, i: (b, i, 0)),
            pl.BlockSpec((1, D_MODEL), lambda b, i: (0, 0)),
            pl.BlockSpec(memory_space=pl.ANY),
        ],
        out_specs=[pl.BlockSpec((None, d, tm // d, B_GROUP_COLS), lambda b, i: (b, 0, i, 0)) for d in dils],
        out_shape=[jax.ShapeDtypeStruct((bsz, d, s // d, B_GROUP_COLS), BF16) for d in dils],
        scratch_shapes=[pltpu.VMEM((D_MODEL // LANES, tm, LANES), F32),
                        pltpu.VMEM((D_MODEL, B_QKV), BF16),
                        pltpu.VMEM((2, CAST_ROWS_WIDE, B_QKV), F32),
                        pltpu.SemaphoreType.DMA((2,))],
        compiler_params=pltpu.CompilerParams(dimension_semantics=("arbitrary", "arbitrary"),
                                             vmem_limit_bytes=VMEM_LIMIT),
        name="qkv_projection",
    )(x3, gpre, w_qkv)


def _t5_bucket_np(rel):
    half = REL_BUCKETS // 2
    ret = np.where(rel > 0, half, 0)
    n = np.abs(rel)
    nf = np.maximum(n, 1).astype(np.float32)
    large = REL_EXACT + (np.log(nf / np.float32(REL_EXACT)) / np.float32(math.log(REL_MAX_DISTANCE / REL_EXACT))
                         * np.float32(half - REL_EXACT)).astype(np.int32)
    large = np.minimum(large, half - 1)
    return (ret + np.where(n < REL_EXACT, n, large)).astype(np.int32)


def _bucket_row(dilation, half_w):
    rel = np.arange(K_WINDOW, dtype=np.int32) - half_w
    row = np.where(np.abs(rel) <= half_w, _t5_bucket_np(rel * dilation), -1).astype(np.int32)
    return np.broadcast_to(row, (8, K_WINDOW))


def _attn_kernel(tbl_ref, bkt_ref, *refs, seq, groups, unroll):
    ng = len(groups)
    z_refs, o_ref = refs[:ng], refs[ng]
    buf, sem, acc_scr, m_scr, den_scr, bm_scr, p_scr, pm_scr = refs[ng + 1:]
    per = Q_BLOCK // STATE_INTERLEAVE
    hp, b = pl.program_id(0), pl.program_id(1)
    nb = pl.num_programs(1)
    step = hp * nb + b
    nsteps = pl.num_programs(0) * nb

    def slab_copies(g, hp_, b_):
        return [pltpu.make_async_copy(
            z_refs[g].at[b_, :, pl.ds(pl.multiple_of((j * HEAD_PAIRS + hp_) * LANES, LANES), LANES)],
            buf.at[g, j], sem.at[g, j]) for j in range(3)]

    @pl.when(step == 0)
    def _():
        for cp in slab_copies(0, hp, b):
            cp.start()

    @pl.when(b == 0)
    def _():
        key = lax.broadcasted_iota(jnp.int32, (Q_BLOCK, K_WINDOW), 1)
        for g, (d, half_w, col) in enumerate(groups):
            for hh in range(2):
                def fill(bk, row, g=g, hh=hh, col=col):
                    return jnp.where(bkt_ref[g] == bk, tbl_ref[bk, col + 2 * hp + hh], row)
                row = lax.fori_loop(0, REL_BUCKETS, fill, jnp.full((8, K_WINDOW), NEG_INF, F32)) * LOG2E
                if d == 1:
                    mid = jnp.concatenate(
                        [pltpu.roll(jnp.broadcast_to(row[0:1], (per, K_WINDOW)), c, 1,
                                    stride=STATE_INTERLEAVE, stride_axis=0) for c in range(STATE_INTERLEAVE)], axis=0)
                else:
                    mid = pltpu.roll(jnp.broadcast_to(row[0:1], (Q_BLOCK, K_WINDOW)), 0, 1, stride=1, stride_axis=0)
                bm_scr[g, hh, 1] = mid
                bm_scr[g, hh, 0] = jnp.where(key < K_WINDOW - half_w,
                                             pltpu.roll(mid, K_WINDOW - half_w, 1), NEG_INF)
                bm_scr[g, hh, 2] = jnp.where(key >= half_w, pltpu.roll(mid, half_w, 1), NEG_INF)

    first_head = lax.broadcasted_iota(jnp.int32, (Q_BLOCK, LANES), 1) < B_HEAD_DIM
    ones = jnp.ones((K_WINDOW, LANES), BF16)

    def run_group(g):
        d, half_w, _ = groups[g]
        length = seq // d
        nblk = length // Q_BLOCK
        ntrips = d * nblk // unroll
        q_ref, k_ref, v_ref = buf.at[g, 0], buf.at[g, 1], buf.at[g, 2]

        def blocks(it):
            for u in range(unroll):
                if unroll >= nblk:
                    r, i = it * (unroll // nblk) + u // nblk, u % nblk
                else:
                    chunks = nblk // unroll
                    r = it // chunks if d > 1 else 0
                    i = (it - r * chunks) * unroll + u
                k0 = pl.multiple_of(r * length + jnp.clip(i * Q_BLOCK - half_w, 0, length - K_WINDOW), half_w)
                yield u, r, i, k0

        def logits_stage(it, slot):
            for u, r, i, k0 in blocks(it):
                q0 = pl.multiple_of(r * length + i * Q_BLOCK, Q_BLOCK)
                var = jnp.where(i == 0, 0, jnp.where(i == nblk - 1, 2, 1))
                qb = q_ref[pl.ds(q0, Q_BLOCK), :]
                kw = k_ref[pl.ds(k0, K_WINDOW), :]
                zero = jnp.zeros_like(qb)
                q2 = jnp.concatenate([jnp.where(first_head, qb, zero), jnp.where(first_head, zero, qb)], axis=0)
                s = lax.dot_general(q2, kw, (((1,), (1,)), ((), ())), preferred_element_type=F32)
                s = s + jnp.concatenate([bm_scr[g, 0, var], bm_scr[g, 1, var]], axis=0)
                m2 = jnp.max(s, axis=-1, keepdims=True)
                p_scr[slot, u] = jnp.exp2(s - m2).astype(BF16)
                pm_scr[slot, u] = jnp.where(first_head, m2[:Q_BLOCK], m2[Q_BLOCK:])

        def values_stage(it, slot):
            for u, r, i, k0 in blocks(it):
                vw = v_ref[pl.ds(k0, K_WINDOW), :]
                ov = jnp.dot(p_scr[slot, u], jnp.concatenate([vw, ones], axis=1), preferred_element_type=F32)
                acc = jnp.where(first_head, ov[:Q_BLOCK, :LANES], ov[Q_BLOCK:, :LANES])
                den = jnp.where(first_head, ov[:Q_BLOCK, LANES:], ov[Q_BLOCK:, LANES:])
                m = pm_scr[slot, u]
                if d == 1:
                    pieces = [(0, Q_BLOCK, pl.ds(i * Q_BLOCK, Q_BLOCK))]
                else:
                    n = Q_BLOCK // d
                    base = r // STATE_INTERLEAVE + per * (r % STATE_INTERLEAVE)
                    pieces = [(k * n, n, pl.ds((i * d + k) * Q_BLOCK + base, n, stride=d // STATE_INTERLEAVE))
                              for k in range(d)]
                gather = lambda ref: jnp.concatenate([ref[idx, :] for _, _, idx in pieces], axis=0)
                if g > 0:
                    m_old = gather(m_scr)
                    m_new = jnp.maximum(m_old, m)
                    w_old, w_cur = jnp.exp2(m_old - m_new), jnp.exp2(m - m_new)
                    acc = w_old * gather(acc_scr) + w_cur * acc
                    den = w_old * gather(den_scr) + w_cur * den
                    m = m_new
                if g == ng - 1:
                    o = acc / den
                    for c in range(STATE_INTERLEAVE):
                        o_ref[pl.ds(i * Q_BLOCK + c, per, stride=STATE_INTERLEAVE), :] = o[c * per:(c + 1) * per]
                else:
                    for ref, val in ((acc_scr, acc), (m_scr, m), (den_scr, den)):
                        for r0, n, idx in pieces:
                            ref[idx, :] = val[r0:r0 + n]

        logits_stage(0, 0)

        def pair(j, carry):
            logits_stage(2 * j + 1, 1)
            values_stage(2 * j, 0)
            logits_stage(2 * j + 2, 0)
            values_stage(2 * j + 1, 1)
            return carry

        lax.fori_loop(0, ntrips // 2 - 1, pair, 0)
        logits_stage(ntrips - 1, 1)
        values_stage(ntrips - 2, 0)
        values_stage(ntrips - 1, 1)

    for g in range(ng):
        if g + 1 < ng:
            for cp in slab_copies(g + 1, hp, b):
                cp.start()
        else:
            @pl.when(step + 1 < nsteps)
            def _():
                nxt = step + 1
                for cp in slab_copies(0, nxt // nb, nxt % nb):
                    cp.start()
        for cp in slab_copies(g, hp, b):
            cp.wait()
        run_group(g)


def _window_attention(zs, rel_bias):
    bsz, s, _ = zs[0].shape
    order = sorted(range(len(B_GROUPS)), key=lambda gi: -B_GROUPS[gi][1])
    groups = tuple((B_GROUPS[gi][1], B_GROUPS[gi][0] // (2 * B_GROUPS[gi][1]), gi * B_HEADS) for gi in order)
    zs = [zs[gi] for gi in order]
    assert groups[-1][0] == 1
    for d, half_w, _ in groups:
        nblk = s // d // Q_BLOCK
        assert K_WINDOW == Q_BLOCK + 2 * half_w and s // d >= K_WINDOW and s % (d * Q_BLOCK) == 0
        assert (d * nblk) % (2 * ATTN_UNROLL) == 0 and (ATTN_UNROLL % nblk == 0 or nblk % ATTN_UNROLL == 0)
        assert d == 1 or (d % STATE_INTERLEAVE == 0 and Q_BLOCK % d == 0)
    bkt = jnp.asarray(np.stack([_bucket_row(d, half_w) for d, half_w, _ in groups]))
    return pl.pallas_call(
        functools.partial(_attn_kernel, seq=s, groups=groups, unroll=ATTN_UNROLL),
        grid=(HEAD_PAIRS, bsz),
        in_specs=[pl.BlockSpec(memory_space=pltpu.SMEM),
                  pl.BlockSpec(bkt.shape, lambda hp, b: (0, 0, 0))]
                 + [pl.BlockSpec(memory_space=pl.ANY)] * len(groups),
        out_specs=pl.BlockSpec((None, s, LANES), lambda hp, b: (b, 0, hp)),
        out_shape=jax.ShapeDtypeStruct((bsz, s, B_WIDTH), F32),
        scratch_shapes=[pltpu.VMEM((len(groups), 3, s, LANES), BF16),
                        pltpu.SemaphoreType.DMA((len(groups), 3)),
                        pltpu.VMEM((s, LANES), F32),
                        pltpu.VMEM((s, LANES), F32),
                        pltpu.VMEM((s, LANES), F32),
                        pltpu.VMEM((len(groups), 2, 3, Q_BLOCK, K_WINDOW), F32),
                        pltpu.VMEM((2, ATTN_UNROLL, 2 * Q_BLOCK, K_WINDOW), BF16),
                        pltpu.VMEM((2, ATTN_UNROLL, Q_BLOCK, LANES), F32)],
        compiler_params=pltpu.CompilerParams(dimension_semantics=("arbitrary", "arbitrary"),
                                             vmem_limit_bytes=VMEM_LIMIT),
        name="window_attention",
    )(rel_bias.astype(F32), bkt, *zs)


def _out_kernel(x_ref, o_ref, gpre_ref, gpost_ref, win_hbm, wout_hbm, out_ref, wg_ref, wout_ref, stage, sem):
    @pl.when(pl.program_id(0) == 0)
    def _():
        _cast_rows_once(win_hbm.at[0, :, pl.ds(B_QKV, B_WIDTH)], wg_ref, stage, sem, chunk=stage.shape[1])
        _cast_rows_once(wout_hbm.at[0], wout_ref, stage, sem, chunk=stage.shape[1])

    x = x_ref[...]
    h = _rms(x, gpre_ref[...]).astype(BF16)
    gate = jnp.dot(h, wg_ref[...], preferred_element_type=F32)
    y = (o_ref[...] * gate / _silu_denominator(gate)).astype(BF16)
    out = jnp.dot(y, wout_ref[...], preferred_element_type=F32)
    out_ref[...] = x + _rms(out, gpost_ref[...])


def _gate_output(x2, o2, gpre, gpost, w_in, w_out, *, tm=512):
    t = x2.shape[0]
    const = lambda *shape: pl.BlockSpec(shape, lambda i: (0,) * len(shape), pipeline_mode=pl.Buffered(1))
    row = pl.BlockSpec((tm, D_MODEL), lambda i: (i, 0))
    return pl.pallas_call(
        _out_kernel,
        grid=(t // tm,),
        in_specs=[row, row, const(1, D_MODEL), const(1, D_MODEL),
                  pl.BlockSpec(memory_space=pl.ANY), pl.BlockSpec(memory_space=pl.ANY)],
        out_specs=row,
        out_shape=jax.ShapeDtypeStruct((t, D_MODEL), F32),
        scratch_shapes=[pltpu.VMEM((D_MODEL, B_WIDTH), BF16), pltpu.VMEM((B_WIDTH, D_MODEL), BF16),
                        pltpu.VMEM((2, CAST_ROWS, D_MODEL), F32), pltpu.SemaphoreType.DMA((2,))],
        compiler_params=pltpu.CompilerParams(dimension_semantics=("arbitrary",),
                                             vmem_limit_bytes=VMEM_LIMIT),
        name="gate_output",
    )(x2, o2, gpre, gpost, w_in, w_out)


def kernel(x, norm_pre, norm_post, a_w_in, a_w_s, a_b_s, a_vnorm_g, a_vnorm_b, a_w_out, b_w_in, b_w_out, rel_bias):
    bsz, s, dm = x.shape
    assert dm == D_MODEL and norm_pre.shape[0] == 2 and a_w_in.shape[0] == 1 and b_w_in.shape[0] == 1

    b_s = jnp.broadcast_to(a_b_s[0][:, :, None], (A_HEADS, A_CHUNK, A_HEAD_DIM))
    x2 = _gmlp_layer(x.reshape(bsz * s, dm), norm_pre[0:1], norm_post[0:1],
                     a_w_in, a_w_s[0], b_s, a_vnorm_g, a_vnorm_b, a_w_out)

    zs = _qkv_projection(x2.reshape(bsz, s, dm), norm_pre[1:2], b_w_in)
    o = _window_attention([z.reshape(bsz, s, B_GROUP_COLS) for z in zs], rel_bias)
    out = _gate_output(x2, o.reshape(bsz * s, B_WIDTH), norm_pre[1:2], norm_post[1:2], b_w_in, b_w_out)
    return out.reshape(bsz, s, dm)
```

```python
import functools
import math

import jax
import jax.numpy as jnp
import numpy as np
from jax import lax
from jax.experimental import pallas as pl
from jax.experimental.pallas import tpu as pltpu

F32 = jnp.float32
BF16 = jnp.bfloat16

EPS = 1e-6
NEG_INF = -1e30

D_MODEL = 1024
A_WIDTH = 2048
A_CHUNK = 128
A_HEADS = 16
A_HEAD_DIM = 128

B_GROUPS = ((128, 1), (512, 4), (2048, 16))
B_HEADS = 16
B_HEAD_DIM = 64
B_WIDTH = B_HEADS * B_HEAD_DIM
B_GROUP_COLS = 3 * B_WIDTH
B_QKV = len(B_GROUPS) * B_GROUP_COLS
REL_BUCKETS = 32
REL_EXACT = 8
REL_MAX_DISTANCE = 1024

LOG2E = math.log2(math.e)
LOGIT_SCALE = B_HEAD_DIM ** -0.5 * LOG2E

LANES = 128
COL_CHUNK = 512
CAST_SLOTS = 4
CAST_ROWS = 256
CAST_ROWS_WIDE = 32
HEAD_PAIRS = B_WIDTH // LANES
Q_BLOCK = 128
K_WINDOW = 256
STATE_INTERLEAVE = 4
ATTN_UNROLL = 8
VMEM_LIMIT = 58 * 1024 * 1024


def _rms(x, g):
    return x * lax.rsqrt(jnp.mean(x * x, axis=-1, keepdims=True) + EPS) * g


def _cast_rows_once(src, dst, stage, sem, *, chunk):
    nchunk, nslot = dst.shape[0] // chunk, stage.shape[0]
    assert dst.shape[0] % chunk == 0 and nchunk >= nslot

    def copy(i):
        return pltpu.make_async_copy(src.at[pl.ds(i * chunk, chunk), :], stage.at[i % nslot], sem.at[i % nslot])

    for i in range(nslot - 1):
        copy(i).start()
    for i in range(nchunk):
        if i + nslot - 1 < nchunk:
            copy(i + nslot - 1).start()
        copy(i).wait()
        dst[pl.ds(i * chunk, chunk), :] = stage[i % nslot].astype(BF16)


_GELU_B = -2.0 * math.sqrt(2.0 / math.pi) * LOG2E
_GELU_A = 0.044715 * _GELU_B


def _gelu_denominator(x):
    return 1.0 + jnp.exp2(x * (_GELU_A * (x * x) + _GELU_B))


def _silu_denominator(x):
    return 1.0 + jnp.exp2(x * -LOG2E)


def _gmlp_kernel(x_ref, gpre_ref, gpost_ref, win_hbm, ws_ref, bs_ref, vng_ref, vnb_ref, wout_hbm,
                 o_ref, win_ref, wout_ref, stage_in, stage_out, sem_in, sem_out, vn_scr, ug_scr, y_scr, *, tm):
    nchunk = tm // A_CHUNK

    @pl.when(pl.program_id(0) == 0)
    def _():
        _cast_rows_once(win_hbm.at[0], win_ref, stage_in, sem_in, chunk=stage_in.shape[1])
        _cast_rows_once(wout_hbm.at[0], wout_ref, stage_out, sem_out, chunk=stage_out.shape[1])

    x = x_ref[...]
    h = _rms(x, gpre_ref[...]).astype(BF16)

    v = jnp.dot(h, win_ref[:, A_WIDTH:2 * A_WIDTH], preferred_element_type=F32)
    v = v / _gelu_denominator(v)
    mu = jnp.mean(v, axis=-1, keepdims=True)
    vc = v - mu
    vn = vc * lax.rsqrt(jnp.mean(vc * vc, axis=-1, keepdims=True) + EPS)
    vn_scr[...] = (vn * vng_ref[...] + vnb_ref[...]).astype(BF16)

    for cb in range(A_WIDTH // COL_CHUNK):
        lo, hi = cb * COL_CHUNK, (cb + 1) * COL_CHUNK
        u = jnp.dot(h, win_ref[:, lo:hi], preferred_element_type=F32)
        g = jnp.dot(h, win_ref[:, 2 * A_WIDTH + lo:2 * A_WIDTH + hi], preferred_element_type=F32)
        ug_scr[:, lo:hi] = (u * g) / (_gelu_denominator(u) * _silu_denominator(g))

    for hd in range(A_HEADS):
        c0 = hd * A_HEAD_DIM
        rhs = jnp.concatenate(
            [vn_scr[c * A_CHUNK:(c + 1) * A_CHUNK, c0:c0 + A_HEAD_DIM] for c in range(nchunk)], axis=1)
        sg = jnp.dot(ws_ref[hd].astype(BF16), rhs, preferred_element_type=F32)
        for c in range(nchunk):
            rows = slice(c * A_CHUNK, (c + 1) * A_CHUNK)
            sgc = sg[:, c * A_CHUNK:(c + 1) * A_CHUNK] + bs_ref[hd]
            y_scr[rows, c0:c0 + A_HEAD_DIM] = (ug_scr[rows, c0:c0 + A_HEAD_DIM] * sgc).astype(BF16)

    out = jnp.dot(y_scr[...], wout_ref[...], preferred_element_type=F32)
    o_ref[...] = x + _rms(out, gpost_ref[...])


def _gmlp_layer(x2, gpre, gpost, w_in, w_s, b_s, vn_g, vn_b, w_out, *, tm=512):
    t = x2.shape[0]
    const = lambda *shape: pl.BlockSpec(shape, lambda i: (0,) * len(shape), pipeline_mode=pl.Buffered(1))
    return pl.pallas_call(
        functools.partial(_gmlp_kernel, tm=tm),
        grid=(t // tm,),
        in_specs=[
            pl.BlockSpec((tm, D_MODEL), lambda i: (i, 0)),
            const(1, D_MODEL), const(1, D_MODEL),
            pl.BlockSpec(memory_space=pl.ANY),
            const(A_HEADS, A_CHUNK, A_CHUNK),
            const(A_HEADS, A_CHUNK, A_HEAD_DIM),
            const(1, A_WIDTH), const(1, A_WIDTH),
            pl.BlockSpec(memory_space=pl.ANY),
        ],
        out_specs=pl.BlockSpec((tm, D_MODEL), lambda i: (i, 0)),
        out_shape=jax.ShapeDtypeStruct((t, D_MODEL), F32),
        scratch_shapes=[pltpu.VMEM((D_MODEL, 3 * A_WIDTH), BF16), pltpu.VMEM((A_WIDTH, D_MODEL), BF16),
                        pltpu.VMEM((CAST_SLOTS, CAST_ROWS_WIDE, 3 * A_WIDTH), F32),
                        pltpu.VMEM((CAST_SLOTS, CAST_ROWS, D_MODEL), F32),
                        pltpu.SemaphoreType.DMA((CAST_SLOTS,)), pltpu.SemaphoreType.DMA((CAST_SLOTS,)),
                        pltpu.VMEM((tm, A_WIDTH), BF16), pltpu.VMEM((tm, A_WIDTH), F32),
                        pltpu.VMEM((tm, A_WIDTH), BF16)],
        compiler_params=pltpu.CompilerParams(dimension_semantics=("arbitrary",),
                                             vmem_limit_bytes=VMEM_LIMIT),
        name="gmlp_layer",
    )(x2, gpre, gpost, w_in, w_s, b_s, vn_g, vn_b, w_out)


def _qkv_kernel(x_ref, gpre_ref, w_hbm, *refs, tm, dils):
    out_refs = refs[:len(dils)]
    h_scr, w_ref, stage, sem = refs[len(dils):]
    nslab = D_MODEL // LANES

    @pl.when((pl.program_id(0) == 0) & (pl.program_id(1) == 0))
    def _():
        _cast_rows_once(w_hbm.at[0, :, pl.ds(0, B_QKV)], w_ref, stage, sem, chunk=stage.shape[1])

    h = _rms(x_ref[...], gpre_ref[...])
    for k in range(nslab):
        h_scr[k] = h[:, k * LANES:(k + 1) * LANES]
    nc = 512

    def gathered(starts, rows, stride):
        return jnp.concatenate(
            [jnp.concatenate([h_scr[k, pl.ds(st, rows, stride=stride), :] for st in starts], axis=0)
             for k in range(nslab)], axis=1).astype(BF16)

    for gi, d in enumerate(dils):
        rows = tm // d
        hp = gathered(range(d), rows, d)
        if d == 1:
            per = Q_BLOCK // STATE_INTERLEAVE
            hq = gathered([blk * Q_BLOCK + c for blk in range(tm // Q_BLOCK) for c in range(STATE_INTERLEAVE)],
                          per, STATE_INTERLEAVE)
        else:
            hq = hp
        for c in range(B_GROUP_COLS // nc):
            c0 = gi * B_GROUP_COLS + c * nc
            is_q = (c + 1) * nc <= B_WIDTH
            z = jnp.dot(hq if is_q else hp, w_ref[:, c0:c0 + nc], preferred_element_type=F32)
            if is_q:
                z = z * LOGIT_SCALE
            out_refs[gi][:, :, c * nc:(c + 1) * nc] = z.reshape(d, rows, nc).astype(BF16)


def _qkv_projection(x3, gpre, w_qkv, *, tm=512):
    bsz, s, _ = x3.shape
    dils = tuple(d for _, d in B_GROUPS)
    return pl.pallas_call(
        functools.partial(_qkv_kernel, tm=tm, dils=dils),
        grid=(bsz, s // tm),
        in_specs=[
            pl.BlockSpec((None, tm, D_MODEL), lambda b, i: (b, i, 0)),
            pl.BlockSpec((1, D_MODEL), lambda b, i: (0, 0)),
            pl.BlockSpec(memory_space=pl.ANY),
        ],
        out_specs=[pl.BlockSpec((None, d, tm // d, B_GROUP_COLS), lambda b, i: (b, 0, i, 0)) for d in dils],
        out_shape=[jax.ShapeDtypeStruct((bsz, d, s // d, B_GROUP_COLS), BF16) for d in dils],
        scratch_shapes=[pltpu.VMEM((D_MODEL // LANES, tm, LANES), F32),
                        pltpu.VMEM((D_MODEL, B_QKV), BF16),
                        pltpu.VMEM((CAST_SLOTS, CAST_ROWS_WIDE, B_QKV), F32),
                        pltpu.SemaphoreType.DMA((CAST_SLOTS,))],
        compiler_params=pltpu.CompilerParams(dimension_semantics=("arbitrary", "arbitrary"),
                                             vmem_limit_bytes=VMEM_LIMIT),
        name="qkv_projection",
    )(x3, gpre, w_qkv)


def _t5_bucket_np(rel):
    half = REL_BUCKETS // 2
    ret = np.where(rel > 0, half, 0)
    n = np.abs(rel)
    nf = np.maximum(n, 1).astype(np.float32)
    large = REL_EXACT + (np.log(nf / np.float32(REL_EXACT)) / np.float32(math.log(REL_MAX_DISTANCE / REL_EXACT))
                         * np.float32(half - REL_EXACT)).astype(np.int32)
    large = np.minimum(large, half - 1)
    return (ret + np.where(n < REL_EXACT, n, large)).astype(np.int32)


def _bucket_row(dilation, half_w):
    rel = np.arange(K_WINDOW, dtype=np.int32) - half_w
    row = np.where(np.abs(rel) <= half_w, _t5_bucket_np(rel * dilation), -1).astype(np.int32)
    return np.broadcast_to(row, (8, K_WINDOW))


def _attn_kernel(tbl_ref, bkt_ref, *refs, seq, groups, unroll):
    ng = len(groups)
    z_refs, o_ref = refs[:ng], refs[ng]
    buf, sem, acc_scr, m_scr, den_scr, bm_scr, p_scr, pm_scr = refs[ng + 1:]
    per = Q_BLOCK // STATE_INTERLEAVE
    hp, b = pl.program_id(0), pl.program_id(1)
    nb = pl.num_programs(1)
    step = hp * nb + b
    nsteps = pl.num_programs(0) * nb

    def slab_copies(g, hp_, b_):
        return [pltpu.make_async_copy(
            z_refs[g].at[b_, :, pl.ds(pl.multiple_of((j * HEAD_PAIRS + hp_) * LANES, LANES), LANES)],
            buf.at[g, j], sem.at[g, j]) for j in range(3)]

    @pl.when(step == 0)
    def _():
        for cp in slab_copies(0, hp, b):
            cp.start()

    @pl.when(b == 0)
    def _():
        key = lax.broadcasted_iota(jnp.int32, (Q_BLOCK, K_WINDOW), 1)
        for g, (d, half_w, col) in enumerate(groups):
            for hh in range(2):
                def fill(bk, row, g=g, hh=hh, col=col):
                    return jnp.where(bkt_ref[g] == bk, tbl_ref[bk, col + 2 * hp + hh], row)
                row = lax.fori_loop(0, REL_BUCKETS, fill, jnp.full((8, K_WINDOW), NEG_INF, F32)) * LOG2E
                if d == 1:
                    mid = jnp.concatenate(
                        [pltpu.roll(jnp.broadcast_to(row[0:1], (per, K_WINDOW)), c, 1,
                                    stride=STATE_INTERLEAVE, stride_axis=0) for c in range(STATE_INTERLEAVE)], axis=0)
                else:
                    mid = pltpu.roll(jnp.broadcast_to(row[0:1], (Q_BLOCK, K_WINDOW)), 0, 1, stride=1, stride_axis=0)
                bm_scr[g, hh, 1] = mid
                bm_scr[g, hh, 0] = jnp.where(key < K_WINDOW - half_w,
                                             pltpu.roll(mid, K_WINDOW - half_w, 1), NEG_INF)
                bm_scr[g, hh, 2] = jnp.where(key >= half_w, pltpu.roll(mid, half_w, 1), NEG_INF)

    first_head = lax.broadcasted_iota(jnp.int32, (Q_BLOCK, LANES), 1) < B_HEAD_DIM
    ones = jnp.ones((K_WINDOW, LANES), BF16)

    def run_group(g):
        d, half_w, _ = groups[g]
        length = seq // d
        nblk = length // Q_BLOCK
        ntrips = d * nblk // unroll
        q_ref, k_ref, v_ref = buf.at[g, 0], buf.at[g, 1], buf.at[g, 2]

        def blocks(it):
            for u in range(unroll):
                if unroll >= nblk:
                    r, i = it * (unroll // nblk) + u // nblk, u % nblk
                else:
                    chunks = nblk // unroll
                    r = it // chunks if d > 1 else 0
                    i = (it - r * chunks) * unroll + u
                k0 = pl.multiple_of(r * length + jnp.clip(i * Q_BLOCK - half_w, 0, length - K_WINDOW), half_w)
                yield u, r, i, k0

        def logits_stage(it, slot):
            for u, r, i, k0 in blocks(it):
                q0 = pl.multiple_of(r * length + i * Q_BLOCK, Q_BLOCK)
                var = jnp.where(i == 0, 0, jnp.where(i == nblk - 1, 2, 1))
                qb = q_ref[pl.ds(q0, Q_BLOCK), :]
                kw = k_ref[pl.ds(k0, K_WINDOW), :]
                zero = jnp.zeros_like(qb)
                q2 = jnp.concatenate([jnp.where(first_head, qb, zero), jnp.where(first_head, zero, qb)], axis=0)
                s = lax.dot_general(q2, kw, (((1,), (1,)), ((), ())), preferred_element_type=F32)
                s = s + jnp.concatenate([bm_scr[g, 0, var], bm_scr[g, 1, var]], axis=0)
                m2 = jnp.max(s, axis=-1, keepdims=True)
                p_scr[slot, u] = jnp.exp2(s - m2).astype(BF16)
                pm_scr[slot, u] = jnp.where(first_head, m2[:Q_BLOCK], m2[Q_BLOCK:])

        def values_stage(it, slot):
            for u, r, i, k0 in blocks(it):
                vw = v_ref[pl.ds(k0, K_WINDOW), :]
                ov = jnp.dot(p_scr[slot, u], jnp.concatenate([vw, ones], axis=1), preferred_element_type=F32)
                acc = jnp.where(first_head, ov[:Q_BLOCK, :LANES], ov[Q_BLOCK:, :LANES])
                den = jnp.where(first_head, ov[:Q_BLOCK, LANES:], ov[Q_BLOCK:, LANES:])
                m = pm_scr[slot, u]
                if d == 1:
                    pieces = [(0, Q_BLOCK, pl.ds(i * Q_BLOCK, Q_BLOCK))]
                else:
                    n = Q_BLOCK // d
                    base = r // STATE_INTERLEAVE + per * (r % STATE_INTERLEAVE)
                    pieces = [(k * n, n, pl.ds((i * d + k) * Q_BLOCK + base, n, stride=d // STATE_INTERLEAVE))
                              for k in range(d)]
                gather = lambda ref: jnp.concatenate([ref[idx, :] for _, _, idx in pieces], axis=0)
                if g > 0:
                    m_old = gather(m_scr)
                    m_new = jnp.maximum(m_old, m)
                    w_old, w_cur = jnp.exp2(m_old - m_new), jnp.exp2(m - m_new)
                    acc = w_old * gather(acc_scr) + w_cur * acc
                    den = w_old * gather(den_scr) + w_cur * den
                    m = m_new
                if g == ng - 1:
                    o = acc / den
                    for c in range(STATE_INTERLEAVE):
                        o_ref[pl.ds(i * Q_BLOCK + c, per, stride=STATE_INTERLEAVE), :] = o[c * per:(c + 1) * per]
                else:
                    for ref, val in ((acc_scr, acc), (m_scr, m), (den_scr, den)):
                        for r0, n, idx in pieces:
                            ref[idx, :] = val[r0:r0 + n]

        logits_stage(0, 0)

        def pair(j, carry):
            logits_stage(2 * j + 1, 1)
            values_stage(2 * j, 0)
            logits_stage(2 * j + 2, 0)
            values_stage(2 * j + 1, 1)
            return carry

        lax.fori_loop(0, ntrips // 2 - 1, pair, 0)
        logits_stage(ntrips - 1, 1)
        values_stage(ntrips - 2, 0)
        values_stage(ntrips - 1, 1)

    for g in range(ng):
        if g + 1 < ng:
            for cp in slab_copies(g + 1, hp, b):
                cp.start()
        else:
            @pl.when(step + 1 < nsteps)
            def _():
                nxt = step + 1
                for cp in slab_copies(0, nxt // nb, nxt % nb):
                    cp.start()
        for cp in slab_copies(g, hp, b):
            cp.wait()
        run_group(g)


def _window_attention(zs, rel_bias):
    bsz, s, _ = zs[0].shape
    order = sorted(range(len(B_GROUPS)), key=lambda gi: -B_GROUPS[gi][1])
    groups = tuple((B_GROUPS[gi][1], B_GROUPS[gi][0] // (2 * B_GROUPS[gi][1]), gi * B_HEADS) for gi in order)
    zs = [zs[gi] for gi in order]
    assert groups[-1][0] == 1
    for d, half_w, _ in groups:
        nblk = s // d // Q_BLOCK
        assert K_WINDOW == Q_BLOCK + 2 * half_w and s // d >= K_WINDOW and s % (d * Q_BLOCK) == 0
        assert (d * nblk) % (2 * ATTN_UNROLL) == 0 and (ATTN_UNROLL % nblk == 0 or nblk % ATTN_UNROLL == 0)
        assert d == 1 or (d % STATE_INTERLEAVE == 0 and Q_BLOCK % d == 0)
    bkt = jnp.asarray(np.stack([_bucket_row(d, half_w) for d, half_w, _ in groups]))
    return pl.pallas_call(
        functools.partial(_attn_kernel, seq=s, groups=groups, unroll=ATTN_UNROLL),
        grid=(HEAD_PAIRS, bsz),
        in_specs=[pl.BlockSpec(memory_space=pltpu.SMEM),
                  pl.BlockSpec(bkt.shape, lambda hp, b: (0, 0, 0))]
                 + [pl.BlockSpec(memory_space=pl.ANY)] * len(groups),
        out_specs=pl.BlockSpec((None, s, LANES), lambda hp, b: (b, 0, hp)),
        out_shape=jax.ShapeDtypeStruct((bsz, s, B_WIDTH), F32),
        scratch_shapes=[pltpu.VMEM((len(groups), 3, s, LANES), BF16),
                        pltpu.SemaphoreType.DMA((len(groups), 3)),
                        pltpu.VMEM((s, LANES), F32),
                        pltpu.VMEM((s, LANES), F32),
                        pltpu.VMEM((s, LANES), F32),
                        pltpu.VMEM((len(groups), 2, 3, Q_BLOCK, K_WINDOW), F32),
                        pltpu.VMEM((2, ATTN_UNROLL, 2 * Q_BLOCK, K_WINDOW), BF16),
                        pltpu.VMEM((2, ATTN_UNROLL, Q_BLOCK, LANES), F32)],
        compiler_params=pltpu.CompilerParams(dimension_semantics=("arbitrary", "arbitrary"),
                                             vmem_limit_bytes=VMEM_LIMIT),
        name="window_attention",
    )(rel_bias.astype(F32), bkt, *zs)


def _out_kernel(x_ref, o_ref, gpre_ref, gpost_ref, win_hbm, wout_hbm, out_ref, wg_ref, wout_ref, stage, sem):
    @pl.when(pl.program_id(0) == 0)
    def _():
        _cast_rows_once(win_hbm.at[0, :, pl.ds(B_QKV, B_WIDTH)], wg_ref, stage, sem, chunk=stage.shape[1])
        _cast_rows_once(wout_hbm.at[0], wout_ref, stage, sem, chunk=stage.shape[1])

    x = x_ref[...]
    h = _rms(x, gpre_ref[...]).astype(BF16)
    gate = jnp.dot(h, wg_ref[...], preferred_element_type=F32)
    y = (o_ref[...] * gate / _silu_denominator(gate)).astype(BF16)
    out = jnp.dot(y, wout_ref[...], preferred_element_type=F32)
    out_ref[...] = x + _rms(out, gpost_ref[...])


def _gate_output(x2, o2, gpre, gpost, w_in, w_out, *, tm=512):
    t = x2.shape[0]
    const = lambda *shape: pl.BlockSpec(shape, lambda i: (0,) * len(shape), pipeline_mode=pl.Buffered(1))
    row = pl.BlockSpec((tm, D_MODEL), lambda i: (i, 0))
    return pl.pallas_call(
        _out_kernel,
        grid=(t // tm,),
        in_specs=[row, row, const(1, D_MODEL), const(1, D_MODEL),
                  pl.BlockSpec(memory_space=pl.ANY), pl.BlockSpec(memory_space=pl.ANY)],
        out_specs=row,
        out_shape=jax.ShapeDtypeStruct((t, D_MODEL), F32),
        scratch_shapes=[pltpu.VMEM((D_MODEL, B_WIDTH), BF16), pltpu.VMEM((B_WIDTH, D_MODEL), BF16),
                        pltpu.VMEM((CAST_SLOTS, CAST_ROWS, D_MODEL), F32),
                        pltpu.SemaphoreType.DMA((CAST_SLOTS,))],
        compiler_params=pltpu.CompilerParams(dimension_semantics=("arbitrary",),
                                             vmem_limit_bytes=VMEM_LIMIT),
        name="gate_output",
    )(x2, o2, gpre, gpost, w_in, w_out)


def kernel(x, norm_pre, norm_post, a_w_in, a_w_s, a_b_s, a_vnorm_g, a_vnorm_b, a_w_out, b_w_in, b_w_out, rel_bias):
    bsz, s, dm = x.shape
    assert dm == D_MODEL and norm_pre.shape[0] == 2 and a_w_in.shape[0] == 1 and b_w_in.shape[0] == 1

    b_s = jnp.broadcast_to(a_b_s[0][:, :, None], (A_HEADS, A_CHUNK, A_HEAD_DIM))
    x2 = _gmlp_layer(x.reshape(bsz * s, dm), norm_pre[0:1], norm_post[0:1],
                     a_w_in, a_w_s[0], b_s, a_vnorm_g, a_vnorm_b, a_w_out)

    zs = _qkv_projection(x2.reshape(bsz, s, dm), norm_pre[1:2], b_w_in)
    o = _window_attention([z.reshape(bsz, s, B_GROUP_COLS) for z in zs], rel_bias)
    out = _gate_output(x2, o.reshape(bsz * s, B_WIDTH), norm_pre[1:2], norm_post[1:2], b_w_in, b_w_out)
    return out.reshape(bsz, s, dm)
```

```python
import functools
import math

import jax
import jax.numpy as jnp
import numpy as np
from jax import lax
from jax.experimental import pallas as pl
from jax.experimental.pallas import tpu as pltpu

F32 = jnp.float32
BF16 = jnp.bfloat16

EPS = 1e-6
NEG_INF = -1e30

D_MODEL = 1024
A_WIDTH = 2048
A_CHUNK = 128
A_HEADS = 16
A_HEAD_DIM = 128

B_GROUPS = ((128, 1), (512, 4), (2048, 16))
B_HEADS = 16
B_HEAD_DIM = 64
B_WIDTH = B_HEADS * B_HEAD_DIM
B_GROUP_COLS = 3 * B_WIDTH
B_QKV = len(B_GROUPS) * B_GROUP_COLS
REL_BUCKETS = 32
REL_EXACT = 8
REL_MAX_DISTANCE = 1024

LOG2E = math.log2(math.e)
LOGIT_SCALE = B_HEAD_DIM ** -0.5 * LOG2E

LANES = 128
COL_CHUNK = 512
CAST_SLOTS = 4
CAST_ROWS = 256
CAST_ROWS_WIDE = 32
HEAD_PAIRS = B_WIDTH // LANES
Q_BLOCK = 128
K_WINDOW = 256
STATE_INTERLEAVE = 4
ATTN_UNROLL = 16
VMEM_LIMIT = 58 * 1024 * 1024


def _rms(x, g):
    return x * lax.rsqrt(jnp.mean(x * x, axis=-1, keepdims=True) + EPS) * g


def _cast_rows_once(src, dst, stage, sem, *, chunk):
    nchunk, nslot = dst.shape[0] // chunk, stage.shape[0]
    assert dst.shape[0] % chunk == 0 and nchunk >= nslot

    def copy(i):
        return pltpu.make_async_copy(src.at[pl.ds(i * chunk, chunk), :], stage.at[i % nslot], sem.at[i % nslot])

    for i in range(nslot - 1):
        copy(i).start()
    for i in range(nchunk):
        if i + nslot - 1 < nchunk:
            copy(i + nslot - 1).start()
        copy(i).wait()
        dst[pl.ds(i * chunk, chunk), :] = stage[i % nslot].astype(BF16)


_GELU_B = -2.0 * math.sqrt(2.0 / math.pi) * LOG2E
_GELU_A = 0.044715 * _GELU_B


def _gelu_denominator(x):
    return 1.0 + jnp.exp2(x * (_GELU_A * (x * x) + _GELU_B))


def _silu_denominator(x):
    return 1.0 + jnp.exp2(x * -LOG2E)


def _gmlp_kernel(x_ref, gpre_ref, gpost_ref, win_hbm, ws_ref, bs_ref, vng_ref, vnb_ref, wout_hbm,
                 o_ref, win_ref, wout_ref, stage_in, stage_out, sem_in, sem_out, vn_scr, ug_scr, y_scr, *, tm):
    nchunk = tm // A_CHUNK

    @pl.when(pl.program_id(0) == 0)
    def _():
        _cast_rows_once(win_hbm.at[0], win_ref, stage_in, sem_in, chunk=stage_in.shape[1])
        _cast_rows_once(wout_hbm.at[0], wout_ref, stage_out, sem_out, chunk=stage_out.shape[1])

    x = x_ref[...]
    h = _rms(x, gpre_ref[...]).astype(BF16)

    v = jnp.dot(h, win_ref[:, A_WIDTH:2 * A_WIDTH], preferred_element_type=F32)
    v = v / _gelu_denominator(v)
    mu = jnp.mean(v, axis=-1, keepdims=True)
    vc = v - mu
    vn = vc * lax.rsqrt(jnp.mean(vc * vc, axis=-1, keepdims=True) + EPS)
    vn_scr[...] = (vn * vng_ref[...] + vnb_ref[...]).astype(BF16)

    for cb in range(A_WIDTH // COL_CHUNK):
        lo, hi = cb * COL_CHUNK, (cb + 1) * COL_CHUNK
        u = jnp.dot(h, win_ref[:, lo:hi], preferred_element_type=F32)
        g = jnp.dot(h, win_ref[:, 2 * A_WIDTH + lo:2 * A_WIDTH + hi], preferred_element_type=F32)
        ug_scr[:, lo:hi] = (u * g) / (_gelu_denominator(u) * _silu_denominator(g))

    for hd in range(A_HEADS):
        c0 = hd * A_HEAD_DIM
        rhs = jnp.concatenate(
            [vn_scr[c * A_CHUNK:(c + 1) * A_CHUNK, c0:c0 + A_HEAD_DIM] for c in range(nchunk)], axis=1)
        sg = jnp.dot(ws_ref[hd].astype(BF16), rhs, preferred_element_type=F32)
        for c in range(nchunk):
            rows = slice(c * A_CHUNK, (c + 1) * A_CHUNK)
            sgc = sg[:, c * A_CHUNK:(c + 1) * A_CHUNK] + bs_ref[hd]
            y_scr[rows, c0:c0 + A_HEAD_DIM] = (ug_scr[rows, c0:c0 + A_HEAD_DIM] * sgc).astype(BF16)

    out = jnp.dot(y_scr[...], wout_ref[...], preferred_element_type=F32)
    o_ref[...] = x + _rms(out, gpost_ref[...])


def _gmlp_layer(x2, gpre, gpost, w_in, w_s, b_s, vn_g, vn_b, w_out, *, tm=512):
    t = x2.shape[0]
    const = lambda *shape: pl.BlockSpec(shape, lambda i: (0,) * len(shape), pipeline_mode=pl.Buffered(1))
    return pl.pallas_call(
        functools.partial(_gmlp_kernel, tm=tm),
        grid=(t // tm,),
        in_specs=[
            pl.BlockSpec((tm, D_MODEL), lambda i: (i, 0)),
            const(1, D_MODEL), const(1, D_MODEL),
            pl.BlockSpec(memory_space=pl.ANY),
            const(A_HEADS, A_CHUNK, A_CHUNK),
            const(A_HEADS, A_CHUNK, A_HEAD_DIM),
            const(1, A_WIDTH), const(1, A_WIDTH),
            pl.BlockSpec(memory_space=pl.ANY),
        ],
        out_specs=pl.BlockSpec((tm, D_MODEL), lambda i: (i, 0)),
        out_shape=jax.ShapeDtypeStruct((t, D_MODEL), F32),
        scratch_shapes=[pltpu.VMEM((D_MODEL, 3 * A_WIDTH), BF16), pltpu.VMEM((A_WIDTH, D_MODEL), BF16),
                        pltpu.VMEM((CAST_SLOTS, CAST_ROWS_WIDE, 3 * A_WIDTH), F32),
                        pltpu.VMEM((CAST_SLOTS, CAST_ROWS, D_MODEL), F32),
                        pltpu.SemaphoreType.DMA((CAST_SLOTS,)), pltpu.SemaphoreType.DMA((CAST_SLOTS,)),
                        pltpu.VMEM((tm, A_WIDTH), BF16), pltpu.VMEM((tm, A_WIDTH), F32),
                        pltpu.VMEM((tm, A_WIDTH), BF16)],
        compiler_params=pltpu.CompilerParams(dimension_semantics=("arbitrary",),
                                             vmem_limit_bytes=VMEM_LIMIT),
        name="gmlp_layer",
    )(x2, gpre, gpost, w_in, w_s, b_s, vn_g, vn_b, w_out)


def _qkv_kernel(x_ref, gpre_ref, w_hbm, *refs, tm, dils):
    out_refs = refs[:len(dils)]
    h_scr, w_ref, stage, sem = refs[len(dils):]
    nslab = D_MODEL // LANES

    @pl.when((pl.program_id(0) == 0) & (pl.program_id(1) == 0))
    def _():
        _cast_rows_once(w_hbm.at[0, :, pl.ds(0, B_QKV)], w_ref, stage, sem, chunk=stage.shape[1])

    h = _rms(x_ref[...], gpre_ref[...])
    for k in range(nslab):
        h_scr[k] = h[:, k * LANES:(k + 1) * LANES]
    nc = 512

    def gathered(starts, rows, stride):
        return jnp.concatenate(
            [jnp.concatenate([h_scr[k, pl.ds(st, rows, stride=stride), :] for st in starts], axis=0)
             for k in range(nslab)], axis=1).astype(BF16)

    for gi, d in enumerate(dils):
        rows = tm // d
        hp = gathered(range(d), rows, d)
        if d == 1:
            per = Q_BLOCK // STATE_INTERLEAVE
            hq = gathered([blk * Q_BLOCK + c for blk in range(tm // Q_BLOCK) for c in range(STATE_INTERLEAVE)],
                          per, STATE_INTERLEAVE)
        else:
            hq = hp
        for c in range(B_GROUP_COLS // nc):
            c0 = gi * B_GROUP_COLS + c * nc
            is_q = (c + 1) * nc <= B_WIDTH
            z = jnp.dot(hq if is_q else hp, w_ref[:, c0:c0 + nc], preferred_element_type=F32)
            if is_q:
                z = z * LOGIT_SCALE
            out_refs[gi][:, :, c * nc:(c + 1) * nc] = z.reshape(d, rows, nc).astype(BF16)


def _qkv_projection(x3, gpre, w_qkv, *, tm=512):
    bsz, s, _ = x3.shape
    dils = tuple(d for _, d in B_GROUPS)
    return pl.pallas_call(
        functools.partial(_qkv_kernel, tm=tm, dils=dils),
        grid=(bsz, s // tm),
        in_specs=[
            pl.BlockSpec((None, tm, D_MODEL), lambda b, i: (b, i, 0)),
            pl.BlockSpec((1, D_MODEL), lambda b, i: (0, 0)),
            pl.BlockSpec(memory_space=pl.ANY),
        ],
        out_specs=[pl.BlockSpec((None, d, tm // d, B_GROUP_COLS), lambda b, i: (b, 0, i, 0)) for d in dils],
        out_shape=[jax.ShapeDtypeStruct((bsz, d, s // d, B_GROUP_COLS), BF16) for d in dils],
        scratch_shapes=[pltpu.VMEM((D_MODEL // LANES, tm, LANES), F32),
                        pltpu.VMEM((D_MODEL, B_QKV), BF16),
                        pltpu.VMEM((CAST_SLOTS, CAST_ROWS_WIDE, B_QKV), F32),
                        pltpu.SemaphoreType.DMA((CAST_SLOTS,))],
        compiler_params=pltpu.CompilerParams(dimension_semantics=("arbitrary", "arbitrary"),
                                             vmem_limit_bytes=VMEM_LIMIT),
        name="qkv_projection",
    )(x3, gpre, w_qkv)


def _t5_bucket_np(rel):
    half = REL_BUCKETS // 2
    ret = np.where(rel > 0, half, 0)
    n = np.abs(rel)
    nf = np.maximum(n, 1).astype(np.float32)
    large = REL_EXACT + (np.log(nf / np.float32(REL_EXACT)) / np.float32(math.log(REL_MAX_DISTANCE / REL_EXACT))
                         * np.float32(half - REL_EXACT)).astype(np.int32)
    large = np.minimum(large, half - 1)
    return (ret + np.where(n < REL_EXACT, n, large)).astype(np.int32)


def _bucket_row(dilation, half_w):
    rel = np.arange(K_WINDOW, dtype=np.int32) - half_w
    row = np.where(np.abs(rel) <= half_w, _t5_bucket_np(rel * dilation), -1).astype(np.int32)
    return np.broadcast_to(row, (8, K_WINDOW))


def _attn_kernel(tbl_ref, bkt_ref, *refs, seq, groups, unroll):
    ng = len(groups)
    z_refs, o_ref = refs[:ng], refs[ng]
    buf, sem, acc_scr, m_scr, den_scr, bm_scr, p_scr, pm_scr = refs[ng + 1:]
    per = Q_BLOCK // STATE_INTERLEAVE
    hp, b = pl.program_id(0), pl.program_id(1)
    nb = pl.num_programs(1)
    step = hp * nb + b
    nsteps = pl.num_programs(0) * nb

    def slab_copies(g, hp_, b_):
        return [pltpu.make_async_copy(
            z_refs[g].at[b_, :, pl.ds(pl.multiple_of((j * HEAD_PAIRS + hp_) * LANES, LANES), LANES)],
            buf.at[g, j], sem.at[g, j]) for j in range(3)]

    @pl.when(step == 0)
    def _():
        for cp in slab_copies(0, hp, b):
            cp.start()

    @pl.when(b == 0)
    def _():
        key = lax.broadcasted_iota(jnp.int32, (Q_BLOCK, K_WINDOW), 1)
        for g, (d, half_w, col) in enumerate(groups):
            for hh in range(2):
                def fill(bk, row, g=g, hh=hh, col=col):
                    return jnp.where(bkt_ref[g] == bk, tbl_ref[bk, col + 2 * hp + hh], row)
                row = lax.fori_loop(0, REL_BUCKETS, fill, jnp.full((8, K_WINDOW), NEG_INF, F32)) * LOG2E
                if d == 1:
                    mid = jnp.concatenate(
                        [pltpu.roll(jnp.broadcast_to(row[0:1], (per, K_WINDOW)), c, 1,
                                    stride=STATE_INTERLEAVE, stride_axis=0) for c in range(STATE_INTERLEAVE)], axis=0)
                else:
                    mid = pltpu.roll(jnp.broadcast_to(row[0:1], (Q_BLOCK, K_WINDOW)), 0, 1, stride=1, stride_axis=0)
                bm_scr[g, hh, 1] = mid
                bm_scr[g, hh, 0] = jnp.where(key < K_WINDOW - half_w,
                                             pltpu.roll(mid, K_WINDOW - half_w, 1), NEG_INF)
                bm_scr[g, hh, 2] = jnp.where(key >= half_w, pltpu.roll(mid, half_w, 1), NEG_INF)

    first_head = lax.broadcasted_iota(jnp.int32, (Q_BLOCK, LANES), 1) < B_HEAD_DIM
    ones = jnp.ones((K_WINDOW, LANES), BF16)

    def run_group(g):
        d, half_w, _ = groups[g]
        length = seq // d
        nblk = length // Q_BLOCK
        ntrips = d * nblk // unroll
        q_ref, k_ref, v_ref = buf.at[g, 0], buf.at[g, 1], buf.at[g, 2]

        def blocks(it):
            for u in range(unroll):
                if unroll >= nblk:
                    r, i = it * (unroll // nblk) + u // nblk, u % nblk
                else:
                    chunks = nblk // unroll
                    r = it // chunks if d > 1 else 0
                    i = (it - r * chunks) * unroll + u
                k0 = pl.multiple_of(r * length + jnp.clip(i * Q_BLOCK - half_w, 0, length - K_WINDOW), half_w)
                yield u, r, i, k0

        def logits_stage(it, slot):
            for u, r, i, k0 in blocks(it):
                q0 = pl.multiple_of(r * length + i * Q_BLOCK, Q_BLOCK)
                var = jnp.where(i == 0, 0, jnp.where(i == nblk - 1, 2, 1))
                qb = q_ref[pl.ds(q0, Q_BLOCK), :]
                kw = k_ref[pl.ds(k0, K_WINDOW), :]
                zero = jnp.zeros_like(qb)
                q2 = jnp.concatenate([jnp.where(first_head, qb, zero), jnp.where(first_head, zero, qb)], axis=0)
                s = lax.dot_general(q2, kw, (((1,), (1,)), ((), ())), preferred_element_type=F32)
                s = s + jnp.concatenate([bm_scr[g, 0, var], bm_scr[g, 1, var]], axis=0)
                m2 = jnp.max(s, axis=-1, keepdims=True)
                p_scr[slot, u] = jnp.exp2(s - m2).astype(BF16)
                pm_scr[slot, u] = jnp.where(first_head, m2[:Q_BLOCK], m2[Q_BLOCK:])

        def values_stage(it, slot):
            for u, r, i, k0 in blocks(it):
                vw = v_ref[pl.ds(k0, K_WINDOW), :]
                ov = jnp.dot(p_scr[slot, u], jnp.concatenate([vw, ones], axis=1), preferred_element_type=F32)
                acc = jnp.where(first_head, ov[:Q_BLOCK, :LANES], ov[Q_BLOCK:, :LANES])
                den = jnp.where(first_head, ov[:Q_BLOCK, LANES:], ov[Q_BLOCK:, LANES:])
                m = pm_scr[slot, u]
                if d == 1:
                    pieces = [(0, Q_BLOCK, pl.ds(i * Q_BLOCK, Q_BLOCK))]
                else:
                    n = Q_BLOCK // d
                    base = r // STATE_INTERLEAVE + per * (r % STATE_INTERLEAVE)
                    pieces = [(k * n, n, pl.ds((i * d + k) * Q_BLOCK + base, n, stride=d // STATE_INTERLEAVE))
                              for k in range(d)]
                gather = lambda ref: jnp.concatenate([ref[idx, :] for _, _, idx in pieces], axis=0)
                if g > 0:
                    m_old = gather(m_scr)
                    m_new = jnp.maximum(m_old, m)
                    w_old, w_cur = jnp.exp2(m_old - m_new), jnp.exp2(m - m_new)
                    acc = w_old * gather(acc_scr) + w_cur * acc
                    den = w_old * gather(den_scr) + w_cur * den
                    m = m_new
                if g == ng - 1:
                    o = acc / den
                    for c in range(STATE_INTERLEAVE):
                        o_ref[pl.ds(i * Q_BLOCK + c, per, stride=STATE_INTERLEAVE), :] = o[c * per:(c + 1) * per]
                else:
                    for ref, val in ((acc_scr, acc), (m_scr, m), (den_scr, den)):
                        for r0, n, idx in pieces:
                            ref[idx, :] = val[r0:r0 + n]

        logits_stage(0, 0)

        def pair(j, carry):
            logits_stage(2 * j + 1, 1)
            values_stage(2 * j, 0)
            logits_stage(2 * j + 2, 0)
            values_stage(2 * j + 1, 1)
            return carry

        lax.fori_loop(0, ntrips // 2 - 1, pair, 0)
        logits_stage(ntrips - 1, 1)
        values_stage(ntrips - 2, 0)
        values_stage(ntrips - 1, 1)

    for g in range(ng):
        if g + 1 < ng:
            for cp in slab_copies(g + 1, hp, b):
                cp.start()
        else:
            @pl.when(step + 1 < nsteps)
            def _():
                nxt = step + 1
                for cp in slab_copies(0, nxt // nb, nxt % nb):
                    cp.start()
        for cp in slab_copies(g, hp, b):
            cp.wait()
        run_group(g)


def _window_attention(zs, rel_bias):
    bsz, s, _ = zs[0].shape
    order = sorted(range(len(B_GROUPS)), key=lambda gi: -B_GROUPS[gi][1])
    groups = tuple((B_GROUPS[gi][1], B_GROUPS[gi][0] // (2 * B_GROUPS[gi][1]), gi * B_HEADS) for gi in order)
    zs = [zs[gi] for gi in order]
    assert groups[-1][0] == 1
    for d, half_w, _ in groups:
        nblk = s // d // Q_BLOCK
        assert K_WINDOW == Q_BLOCK + 2 * half_w and s // d >= K_WINDOW and s % (d * Q_BLOCK) == 0
        assert (d * nblk) % (2 * ATTN_UNROLL) == 0 and (ATTN_UNROLL % nblk == 0 or nblk % ATTN_UNROLL == 0)
        assert d == 1 or (d % STATE_INTERLEAVE == 0 and Q_BLOCK % d == 0)
    bkt = jnp.asarray(np.stack([_bucket_row(d, half_w) for d, half_w, _ in groups]))
    return pl.pallas_call(
        functools.partial(_attn_kernel, seq=s, groups=groups, unroll=ATTN_UNROLL),
        grid=(HEAD_PAIRS, bsz),
        in_specs=[pl.BlockSpec(memory_space=pltpu.SMEM),
                  pl.BlockSpec(bkt.shape, lambda hp, b: (0, 0, 0))]
                 + [pl.BlockSpec(memory_space=pl.ANY)] * len(groups),
        out_specs=pl.BlockSpec((None, s, LANES), lambda hp, b: (b, 0, hp)),
        out_shape=jax.ShapeDtypeStruct((bsz, s, B_WIDTH), F32),
        scratch_shapes=[pltpu.VMEM((len(groups), 3, s, LANES), BF16),
                        pltpu.SemaphoreType.DMA((len(groups), 3)),
                        pltpu.VMEM((s, LANES), F32),
                        pltpu.VMEM((s, LANES), F32),
                        pltpu.VMEM((s, LANES), F32),
                        pltpu.VMEM((len(groups), 2, 3, Q_BLOCK, K_WINDOW), F32),
                        pltpu.VMEM((2, ATTN_UNROLL, 2 * Q_BLOCK, K_WINDOW), BF16),
                        pltpu.VMEM((2, ATTN_UNROLL, Q_BLOCK, LANES), F32)],
        compiler_params=pltpu.CompilerParams(dimension_semantics=("arbitrary", "arbitrary"),
                                             vmem_limit_bytes=VMEM_LIMIT),
        name="window_attention",
    )(rel_bias.astype(F32), bkt, *zs)


def _out_kernel(x_ref, o_ref, gpre_ref, gpost_ref, win_hbm, wout_hbm, out_ref, wg_ref, wout_ref, stage, sem):
    @pl.when(pl.program_id(0) == 0)
    def _():
        _cast_rows_once(win_hbm.at[0, :, pl.ds(B_QKV, B_WIDTH)], wg_ref, stage, sem, chunk=stage.shape[1])
        _cast_rows_once(wout_hbm.at[0], wout_ref, stage, sem, chunk=stage.shape[1])

    x = x_ref[...]
    h = _rms(x, gpre_ref[...]).astype(BF16)
    gate = jnp.dot(h, wg_ref[...], preferred_element_type=F32)
    y = (o_ref[...] * gate / _silu_denominator(gate)).astype(BF16)
    out = jnp.dot(y, wout_ref[...], preferred_element_type=F32)
    out_ref[...] = x + _rms(out, gpost_ref[...])


def _gate_output(x2, o2, gpre, gpost, w_in, w_out, *, tm=512):
    t = x2.shape[0]
    const = lambda *shape: pl.BlockSpec(shape, lambda i: (0,) * len(shape), pipeline_mode=pl.Buffered(1))
    row = pl.BlockSpec((tm, D_MODEL), lambda i: (i, 0))
    return pl.pallas_call(
        _out_kernel,
        grid=(t // tm,),
        in_specs=[row, row, const(1, D_MODEL), const(1, D_MODEL),
                  pl.BlockSpec(memory_space=pl.ANY), pl.BlockSpec(memory_space=pl.ANY)],
        out_specs=row,
        out_shape=jax.ShapeDtypeStruct((t, D_MODEL), F32),
        scratch_shapes=[pltpu.VMEM((D_MODEL, B_WIDTH), BF16), pltpu.VMEM((B_WIDTH, D_MODEL), BF16),
                        pltpu.VMEM((CAST_SLOTS, CAST_ROWS, D_MODEL), F32),
                        pltpu.SemaphoreType.DMA((CAST_SLOTS,))],
        compiler_params=pltpu.CompilerParams(dimension_semantics=("arbitrary",),
                                             vmem_limit_bytes=VMEM_LIMIT),
        name="gate_output",
    )(x2, o2, gpre, gpost, w_in, w_out)


def kernel(x, norm_pre, norm_post, a_w_in, a_w_s, a_b_s, a_vnorm_g, a_vnorm_b, a_w_out, b_w_in, b_w_out, rel_bias):
    bsz, s, dm = x.shape
    assert dm == D_MODEL and norm_pre.shape[0] == 2 and a_w_in.shape[0] == 1 and b_w_in.shape[0] == 1

    b_s = jnp.broadcast_to(a_b_s[0][:, :, None], (A_HEADS, A_CHUNK, A_HEAD_DIM))
    x2 = _gmlp_layer(x.reshape(bsz * s, dm), norm_pre[0:1], norm_post[0:1],
                     a_w_in, a_w_s[0], b_s, a_vnorm_g, a_vnorm_b, a_w_out)

    zs = _qkv_projection(x2.reshape(bsz, s, dm), norm_pre[1:2], b_w_in)
    o = _window_attention([z.reshape(bsz, s, B_GROUP_COLS) for z in zs], rel_bias)
    out = _gate_output(x2, o.reshape(bsz * s, B_WIDTH), norm_pre[1:2], norm_post[1:2], b_w_in, b_w_out)
    return out.reshape(bsz, s, dm)
```

```python
import functools
import math

import jax
import jax.numpy as jnp
import numpy as np
from jax import lax
from jax.experimental import pallas as pl
from jax.experimental.pallas import tpu as pltpu

F32 = jnp.float32
BF16 = jnp.bfloat16

EPS = 1e-6
NEG_INF = -1e30

D_MODEL = 1024
A_WIDTH = 2048
A_CHUNK = 128
A_HEADS = 16
A_HEAD_DIM = 128

B_GROUPS = ((128, 1), (512, 4), (2048, 16))
B_HEADS = 16
B_HEAD_DIM = 64
B_WIDTH = B_HEADS * B_HEAD_DIM
B_GROUP_COLS = 3 * B_WIDTH
B_QKV = len(B_GROUPS) * B_GROUP_COLS
REL_BUCKETS = 32
REL_EXACT = 8
REL_MAX_DISTANCE = 1024

LOG2E = math.log2(math.e)
LOGIT_SCALE = B_HEAD_DIM ** -0.5 * LOG2E

LANES = 128
COL_CHUNK = 512
GMLP_ROW_PARTS = 2
CAST_SLOTS = 4
CAST_ROWS = 256
CAST_ROWS_WIDE = 32
HEAD_PAIRS = B_WIDTH // LANES
Q_BLOCK = 128
K_WINDOW = 256
STATE_INTERLEAVE = 4
ATTN_UNROLL = 16
VMEM_LIMIT = 58 * 1024 * 1024


def _rms(x, g):
    return x * lax.rsqrt(jnp.mean(x * x, axis=-1, keepdims=True) + EPS) * g


def _cast_rows_once(src, dst, stage, sem, *, chunk, col_blocks=None):
    col_blocks = col_blocks or [(0, 0, dst.shape[1])]
    nchunk, nslot = dst.shape[0] // chunk, stage.shape[0]
    assert dst.shape[0] % chunk == 0 and nchunk >= nslot

    def copy(i):
        return pltpu.make_async_copy(src.at[pl.ds(i * chunk, chunk), :], stage.at[i % nslot], sem.at[i % nslot])

    for i in range(nslot - 1):
        copy(i).start()
    for i in range(nchunk):
        if i + nslot - 1 < nchunk:
            copy(i + nslot - 1).start()
        copy(i).wait()
        for s0, d0, width in col_blocks:
            dst[pl.ds(i * chunk, chunk), d0:d0 + width] = stage[i % nslot, :, s0:s0 + width].astype(BF16)


_GELU_B = -2.0 * math.sqrt(2.0 / math.pi) * LOG2E
_GELU_A = 0.044715 * _GELU_B


def _gelu_denominator(x):
    return 1.0 + jnp.exp2(x * (_GELU_A * (x * x) + _GELU_B))


def _silu_denominator(x):
    return 1.0 + jnp.exp2(x * -LOG2E)


def _gmlp_kernel(x_ref, gpre_ref, gpost_ref, win_hbm, ws_ref, bs_ref, vng_ref, vnb_ref, wout_hbm,
                 o_ref, win_ref, wout_ref, stage_in, stage_out, sem_in, sem_out, vn_scr, ug_scr, y_scr, *, tm):
    @pl.when(pl.program_id(0) == 0)
    def _():
        half = COL_CHUNK // 2
        pairs = [blk for c in range(A_WIDTH // half)
                 for blk in ((c * half, A_WIDTH + c * COL_CHUNK, half),
                             (2 * A_WIDTH + c * half, A_WIDTH + c * COL_CHUNK + half, half))]
        _cast_rows_once(win_hbm.at[0], win_ref, stage_in, sem_in, chunk=stage_in.shape[1],
                        col_blocks=[(A_WIDTH, 0, A_WIDTH)] + pairs)
        _cast_rows_once(wout_hbm.at[0], wout_ref, stage_out, sem_out, chunk=stage_out.shape[1])

    parts = [slice(k * (tm // GMLP_ROW_PARTS), (k + 1) * (tm // GMLP_ROW_PARTS)) for k in range(GMLP_ROW_PARTS)]
    hs = [_rms(x_ref[rows, :], gpre_ref[...]).astype(BF16) for rows in parts]

    def v_path(k):
        v = jnp.dot(hs[k], win_ref[:, :A_WIDTH], preferred_element_type=F32)
        v = v / _gelu_denominator(v)
        mu = jnp.mean(v, axis=-1, keepdims=True)
        vc = v - mu
        vn = vc * lax.rsqrt(jnp.mean(vc * vc, axis=-1, keepdims=True) + EPS)
        vn_scr[parts[k], :] = (vn * vng_ref[...] + vnb_ref[...]).astype(BF16)

    def gate_path(k):
        half = COL_CHUNK // 2
        for cb in range(A_WIDTH // half):
            z = jnp.dot(hs[k], win_ref[:, A_WIDTH + cb * COL_CHUNK:A_WIDTH + (cb + 1) * COL_CHUNK],
                        preferred_element_type=F32)
            u, g = z[:, :half], z[:, half:]
            ug_scr[parts[k], cb * half:(cb + 1) * half] = (u * g) / (_gelu_denominator(u) * _silu_denominator(g))

    def mix(k):
        chunks = range(parts[k].start // A_CHUNK, parts[k].stop // A_CHUNK)
        for hd in range(A_HEADS):
            c0 = hd * A_HEAD_DIM
            rhs = jnp.concatenate(
                [vn_scr[c * A_CHUNK:(c + 1) * A_CHUNK, c0:c0 + A_HEAD_DIM] for c in chunks], axis=1)
            sg = jnp.dot(ws_ref[hd].astype(BF16), rhs, preferred_element_type=F32)
            for n, c in enumerate(chunks):
                rows = slice(c * A_CHUNK, (c + 1) * A_CHUNK)
                sgc = sg[:, n * A_CHUNK:(n + 1) * A_CHUNK] + bs_ref[hd]
                y_scr[rows, c0:c0 + A_HEAD_DIM] = (ug_scr[rows, c0:c0 + A_HEAD_DIM] * sgc).astype(BF16)

    def finish(k):
        out = jnp.dot(y_scr[parts[k], :], wout_ref[...], preferred_element_type=F32)
        o_ref[parts[k], :] = x_ref[parts[k], :] + _rms(out, gpost_ref[...])

    for k in range(GMLP_ROW_PARTS):
        v_path(k)
    for k in range(GMLP_ROW_PARTS):
        gate_path(k)
        if k > 0:
            finish(k - 1)
        mix(k)
    finish(GMLP_ROW_PARTS - 1)


def _gmlp_layer(x2, gpre, gpost, w_in, w_s, b_s, vn_g, vn_b, w_out, *, tm=512):
    t = x2.shape[0]
    const = lambda *shape: pl.BlockSpec(shape, lambda i: (0,) * len(shape), pipeline_mode=pl.Buffered(1))
    return pl.pallas_call(
        functools.partial(_gmlp_kernel, tm=tm),
        grid=(t // tm,),
        in_specs=[
            pl.BlockSpec((tm, D_MODEL), lambda i: (i, 0)),
            const(1, D_MODEL), const(1, D_MODEL),
            pl.BlockSpec(memory_space=pl.ANY),
            const(A_HEADS, A_CHUNK, A_CHUNK),
            const(A_HEADS, A_CHUNK, A_HEAD_DIM),
            const(1, A_WIDTH), const(1, A_WIDTH),
            pl.BlockSpec(memory_space=pl.ANY),
        ],
        out_specs=pl.BlockSpec((tm, D_MODEL), lambda i: (i, 0)),
        out_shape=jax.ShapeDtypeStruct((t, D_MODEL), F32),
        scratch_shapes=[pltpu.VMEM((D_MODEL, 3 * A_WIDTH), BF16), pltpu.VMEM((A_WIDTH, D_MODEL), BF16),
                        pltpu.VMEM((CAST_SLOTS, CAST_ROWS_WIDE, 3 * A_WIDTH), F32),
                        pltpu.VMEM((CAST_SLOTS, CAST_ROWS, D_MODEL), F32),
                        pltpu.SemaphoreType.DMA((CAST_SLOTS,)), pltpu.SemaphoreType.DMA((CAST_SLOTS,)),
                        pltpu.VMEM((tm, A_WIDTH), BF16), pltpu.VMEM((tm, A_WIDTH), F32),
                        pltpu.VMEM((tm, A_WIDTH), BF16)],
        compiler_params=pltpu.CompilerParams(dimension_semantics=("arbitrary",),
                                             vmem_limit_bytes=VMEM_LIMIT),
        name="gmlp_layer",
    )(x2, gpre, gpost, w_in, w_s, b_s, vn_g, vn_b, w_out)


def _qkv_kernel(x_ref, gpre_ref, w_hbm, *refs, tm, dils):
    out_refs = refs[:len(dils)]
    h_scr, w_ref, stage, sem = refs[len(dils):]
    nslab = D_MODEL // LANES

    @pl.when((pl.program_id(0) == 0) & (pl.program_id(1) == 0))
    def _():
        _cast_rows_once(w_hbm.at[0, :, pl.ds(0, B_QKV)], w_ref, stage, sem, chunk=stage.shape[1])

    h = _rms(x_ref[...], gpre_ref[...])
    for k in range(nslab):
        h_scr[k] = h[:, k * LANES:(k + 1) * LANES]
    nc = 512

    def gathered(starts, rows, stride):
        return jnp.concatenate(
            [jnp.concatenate([h_scr[k, pl.ds(st, rows, stride=stride), :] for st in starts], axis=0)
             for k in range(nslab)], axis=1).astype(BF16)

    for gi, d in enumerate(dils):
        rows = tm // d
        hp = gathered(range(d), rows, d)
        if d == 1:
            per = Q_BLOCK // STATE_INTERLEAVE
            hq = gathered([blk * Q_BLOCK + c for blk in range(tm // Q_BLOCK) for c in range(STATE_INTERLEAVE)],
                          per, STATE_INTERLEAVE)
        else:
            hq = hp
        for c in range(B_GROUP_COLS // nc):
            c0 = gi * B_GROUP_COLS + c * nc
            is_q = (c + 1) * nc <= B_WIDTH
            z = jnp.dot(hq if is_q else hp, w_ref[:, c0:c0 + nc], preferred_element_type=F32)
            if is_q:
                z = z * LOGIT_SCALE
            out_refs[gi][:, :, c * nc:(c + 1) * nc] = z.reshape(d, rows, nc).astype(BF16)


def _qkv_projection(x3, gpre, w_qkv, *, tm=512):
    bsz, s, _ = x3.shape
    dils = tuple(d for _, d in B_GROUPS)
    return pl.pallas_call(
        functools.partial(_qkv_kernel, tm=tm, dils=dils),
        grid=(bsz, s // tm),
        in_specs=[
            pl.BlockSpec((None, tm, D_MODEL), lambda b, i: (b, i, 0)),
            pl.BlockSpec((1, D_MODEL), lambda b, i: (0, 0)),
            pl.BlockSpec(memory_space=pl.ANY),
        ],
        out_specs=[pl.BlockSpec((None, d, tm // d, B_GROUP_COLS), lambda b, i: (b, 0, i, 0)) for d in dils],
        out_shape=[jax.ShapeDtypeStruct((bsz, d, s // d, B_GROUP_COLS), BF16) for d in dils],
        scratch_shapes=[pltpu.VMEM((D_MODEL // LANES, tm, LANES), F32),
                        pltpu.VMEM((D_MODEL, B_QKV), BF16),
                        pltpu.VMEM((CAST_SLOTS, CAST_ROWS_WIDE, B_QKV), F32),
                        pltpu.SemaphoreType.DMA((CAST_SLOTS,))],
        compiler_params=pltpu.CompilerParams(dimension_semantics=("arbitrary", "arbitrary"),
                                             vmem_limit_bytes=VMEM_LIMIT),
        name="qkv_projection",
    )(x3, gpre, w_qkv)


def _t5_bucket_np(rel):
    half = REL_BUCKETS // 2
    ret = np.where(rel > 0, half, 0)
    n = np.abs(rel)
    nf = np.maximum(n, 1).astype(np.float32)
    large = REL_EXACT + (np.log(nf / np.float32(REL_EXACT)) / np.float32(math.log(REL_MAX_DISTANCE / REL_EXACT))
                         * np.float32(half - REL_EXACT)).astype(np.int32)
    large = np.minimum(large, half - 1)
    return (ret + np.where(n < REL_EXACT, n, large)).astype(np.int32)


def _bucket_row(dilation, half_w):
    rel = np.arange(K_WINDOW, dtype=np.int32) - half_w
    row = np.where(np.abs(rel) <= half_w, _t5_bucket_np(rel * dilation), -1).astype(np.int32)
    return np.broadcast_to(row, (8, K_WINDOW))


def _attn_kernel(tbl_ref, bkt_ref, *refs, seq, groups, unroll):
    ng = len(groups)
    z_refs, o_ref = refs[:ng], refs[ng]
    buf, sem, acc_scr, m_scr, den_scr, bm_scr, p_scr, pm_scr = refs[ng + 1:]
    per = Q_BLOCK // STATE_INTERLEAVE
    hp, b = pl.program_id(0), pl.program_id(1)
    nb = pl.num_programs(1)
    step = hp * nb + b
    nsteps = pl.num_programs(0) * nb

    def slab_copies(g, hp_, b_):
        return [pltpu.make_async_copy(
            z_refs[g].at[b_, :, pl.ds(pl.multiple_of((j * HEAD_PAIRS + hp_) * LANES, LANES), LANES)],
            buf.at[g, j], sem.at[g, j]) for j in range(3)]

    @pl.when(step == 0)
    def _():
        for cp in slab_copies(0, hp, b):
            cp.start()

    @pl.when(b == 0)
    def _():
        key = lax.broadcasted_iota(jnp.int32, (Q_BLOCK, K_WINDOW), 1)
        for g, (d, half_w, col) in enumerate(groups):
            for hh in range(2):
                def fill(bk, row, g=g, hh=hh, col=col):
                    return jnp.where(bkt_ref[g] == bk, tbl_ref[bk, col + 2 * hp + hh], row)
                row = lax.fori_loop(0, REL_BUCKETS, fill, jnp.full((8, K_WINDOW), NEG_INF, F32)) * LOG2E
                if d == 1:
                    mid = jnp.concatenate(
                        [pltpu.roll(jnp.broadcast_to(row[0:1], (per, K_WINDOW)), c, 1,
                                    stride=STATE_INTERLEAVE, stride_axis=0) for c in range(STATE_INTERLEAVE)], axis=0)
                else:
                    mid = pltpu.roll(jnp.broadcast_to(row[0:1], (Q_BLOCK, K_WINDOW)), 0, 1, stride=1, stride_axis=0)
                bm_scr[g, hh, 1] = mid
                bm_scr[g, hh, 0] = jnp.where(key < K_WINDOW - half_w,
                                             pltpu.roll(mid, K_WINDOW - half_w, 1), NEG_INF)
                bm_scr[g, hh, 2] = jnp.where(key >= half_w, pltpu.roll(mid, half_w, 1), NEG_INF)

    first_head = lax.broadcasted_iota(jnp.int32, (Q_BLOCK, LANES), 1) < B_HEAD_DIM
    ones = jnp.ones((K_WINDOW, LANES), BF16)

    def run_group(g):
        d, half_w, _ = groups[g]
        length = seq // d
        nblk = length // Q_BLOCK
        ntrips = d * nblk // unroll
        q_ref, k_ref, v_ref = buf.at[g, 0], buf.at[g, 1], buf.at[g, 2]

        def blocks(it):
            for u in range(unroll):
                if unroll >= nblk:
                    r, i = it * (unroll // nblk) + u // nblk, u % nblk
                else:
                    chunks = nblk // unroll
                    r = it // chunks if d > 1 else 0
                    i = (it - r * chunks) * unroll + u
                k0 = pl.multiple_of(r * length + jnp.clip(i * Q_BLOCK - half_w, 0, length - K_WINDOW), half_w)
                yield u, r, i, k0

        def logits_stage(it, slot):
            for u, r, i, k0 in blocks(it):
                q0 = pl.multiple_of(r * length + i * Q_BLOCK, Q_BLOCK)
                var = jnp.where(i == 0, 0, jnp.where(i == nblk - 1, 2, 1))
                qb = q_ref[pl.ds(q0, Q_BLOCK), :]
                kw = k_ref[pl.ds(k0, K_WINDOW), :]
                zero = jnp.zeros_like(qb)
                q2 = jnp.concatenate([jnp.where(first_head, qb, zero), jnp.where(first_head, zero, qb)], axis=0)
                s = lax.dot_general(q2, kw, (((1,), (1,)), ((), ())), preferred_element_type=F32)
                s = s + jnp.concatenate([bm_scr[g, 0, var], bm_scr[g, 1, var]], axis=0)
                m2 = jnp.max(s, axis=-1, keepdims=True)
                p_scr[slot, u] = jnp.exp2(s - m2).astype(BF16)
                pm_scr[slot, u] = jnp.where(first_head, m2[:Q_BLOCK], m2[Q_BLOCK:])

        def values_stage(it, slot):
            for u, r, i, k0 in blocks(it):
                vw = v_ref[pl.ds(k0, K_WINDOW), :]
                ov = jnp.dot(p_scr[slot, u], jnp.concatenate([vw, ones], axis=1), preferred_element_type=F32)
                acc = jnp.where(first_head, ov[:Q_BLOCK, :LANES], ov[Q_BLOCK:, :LANES])
                den = jnp.where(first_head, ov[:Q_BLOCK, LANES:], ov[Q_BLOCK:, LANES:])
                m = pm_scr[slot, u]
                if d == 1:
                    pieces = [(0, Q_BLOCK, pl.ds(i * Q_BLOCK, Q_BLOCK))]
                else:
                    n = Q_BLOCK // d
                    base = r // STATE_INTERLEAVE + per * (r % STATE_INTERLEAVE)
                    pieces = [(k * n, n, pl.ds((i * d + k) * Q_BLOCK + base, n, stride=d // STATE_INTERLEAVE))
                              for k in range(d)]
                gather = lambda ref: jnp.concatenate([ref[idx, :] for _, _, idx in pieces], axis=0)
                if g > 0:
                    m_old = gather(m_scr)
                    m_new = jnp.maximum(m_old, m)
                    w_old, w_cur = jnp.exp2(m_old - m_new), jnp.exp2(m - m_new)
                    acc = w_old * gather(acc_scr) + w_cur * acc
                    den = w_old * gather(den_scr) + w_cur * den
                    m = m_new
                if g == ng - 1:
                    o = acc / den
                    for c in range(STATE_INTERLEAVE):
                        o_ref[pl.ds(i * Q_BLOCK + c, per, stride=STATE_INTERLEAVE), :] = o[c * per:(c + 1) * per]
                else:
                    for ref, val in ((acc_scr, acc), (m_scr, m), (den_scr, den)):
                        for r0, n, idx in pieces:
                            ref[idx, :] = val[r0:r0 + n]

        logits_stage(0, 0)

        def pair(j, carry):
            logits_stage(2 * j + 1, 1)
            values_stage(2 * j, 0)
            logits_stage(2 * j + 2, 0)
            values_stage(2 * j + 1, 1)
            return carry

        lax.fori_loop(0, ntrips // 2 - 1, pair, 0)
        logits_stage(ntrips - 1, 1)
        values_stage(ntrips - 2, 0)
        values_stage(ntrips - 1, 1)

    for g in range(ng):
        if g + 1 < ng:
            for cp in slab_copies(g + 1, hp, b):
                cp.start()
        else:
            @pl.when(step + 1 < nsteps)
            def _():
                nxt = step + 1
                for cp in slab_copies(0, nxt // nb, nxt % nb):
                    cp.start()
        for cp in slab_copies(g, hp, b):
            cp.wait()
        run_group(g)


def _window_attention(zs, rel_bias):
    bsz, s, _ = zs[0].shape
    order = sorted(range(len(B_GROUPS)), key=lambda gi: -B_GROUPS[gi][1])
    groups = tuple((B_GROUPS[gi][1], B_GROUPS[gi][0] // (2 * B_GROUPS[gi][1]), gi * B_HEADS) for gi in order)
    zs = [zs[gi] for gi in order]
    assert groups[-1][0] == 1
    for d, half_w, _ in groups:
        nblk = s // d // Q_BLOCK
        assert K_WINDOW == Q_BLOCK + 2 * half_w and s // d >= K_WINDOW and s % (d * Q_BLOCK) == 0
        assert (d * nblk) % (2 * ATTN_UNROLL) == 0 and (ATTN_UNROLL % nblk == 0 or nblk % ATTN_UNROLL == 0)
        assert d == 1 or (d % STATE_INTERLEAVE == 0 and Q_BLOCK % d == 0)
    bkt = jnp.asarray(np.stack([_bucket_row(d, half_w) for d, half_w, _ in groups]))
    return pl.pallas_call(
        functools.partial(_attn_kernel, seq=s, groups=groups, unroll=ATTN_UNROLL),
        grid=(HEAD_PAIRS, bsz),
        in_specs=[pl.BlockSpec(memory_space=pltpu.SMEM),
                  pl.BlockSpec(bkt.shape, lambda hp, b: (0, 0, 0))]
                 + [pl.BlockSpec(memory_space=pl.ANY)] * len(groups),
        out_specs=pl.BlockSpec((None, s, LANES), lambda hp, b: (b, 0, hp)),
        out_shape=jax.ShapeDtypeStruct((bsz, s, B_WIDTH), F32),
        scratch_shapes=[pltpu.VMEM((len(groups), 3, s, LANES), BF16),
                        pltpu.SemaphoreType.DMA((len(groups), 3)),
                        pltpu.VMEM((s, LANES), F32),
                        pltpu.VMEM((s, LANES), F32),
                        pltpu.VMEM((s, LANES), F32),
                        pltpu.VMEM((len(groups), 2, 3, Q_BLOCK, K_WINDOW), F32),
                        pltpu.VMEM((2, ATTN_UNROLL, 2 * Q_BLOCK, K_WINDOW), BF16),
                        pltpu.VMEM((2, ATTN_UNROLL, Q_BLOCK, LANES), F32)],
        compiler_params=pltpu.CompilerParams(dimension_semantics=("arbitrary", "arbitrary"),
                                             vmem_limit_bytes=VMEM_LIMIT),
        name="window_attention",
    )(rel_bias.astype(F32), bkt, *zs)


def _out_kernel(x_ref, o_ref, gpre_ref, gpost_ref, win_hbm, wout_hbm, out_ref, wg_ref, wout_ref, stage, sem):
    @pl.when(pl.program_id(0) == 0)
    def _():
        _cast_rows_once(win_hbm.at[0, :, pl.ds(B_QKV, B_WIDTH)], wg_ref, stage, sem, chunk=stage.shape[1])
        _cast_rows_once(wout_hbm.at[0], wout_ref, stage, sem, chunk=stage.shape[1])

    tm = x_ref.shape[0]
    parts = [slice(k * (tm // GMLP_ROW_PARTS), (k + 1) * (tm // GMLP_ROW_PARTS)) for k in range(GMLP_ROW_PARTS)]
    ys = []
    for rows in parts:
        h = _rms(x_ref[rows, :], gpre_ref[...]).astype(BF16)
        gate = jnp.dot(h, wg_ref[...], preferred_element_type=F32)
        ys.append((o_ref[rows, :] * gate / _silu_denominator(gate)).astype(BF16))
    for rows, y in zip(parts, ys):
        out = jnp.dot(y, wout_ref[...], preferred_element_type=F32)
        out_ref[rows, :] = x_ref[rows, :] + _rms(out, gpost_ref[...])


def _gate_output(x2, o2, gpre, gpost, w_in, w_out, *, tm=512):
    t = x2.shape[0]
    const = lambda *shape: pl.BlockSpec(shape, lambda i: (0,) * len(shape), pipeline_mode=pl.Buffered(1))
    row = pl.BlockSpec((tm, D_MODEL), lambda i: (i, 0))
    return pl.pallas_call(
        _out_kernel,
        grid=(t // tm,),
        in_specs=[row, row, const(1, D_MODEL), const(1, D_MODEL),
                  pl.BlockSpec(memory_space=pl.ANY), pl.BlockSpec(memory_space=pl.ANY)],
        out_specs=row,
        out_shape=jax.ShapeDtypeStruct((t, D_MODEL), F32),
        scratch_shapes=[pltpu.VMEM((D_MODEL, B_WIDTH), BF16), pltpu.VMEM((B_WIDTH, D_MODEL), BF16),
                        pltpu.VMEM((CAST_SLOTS, CAST_ROWS, D_MODEL), F32),
                        pltpu.SemaphoreType.DMA((CAST_SLOTS,))],
        compiler_params=pltpu.CompilerParams(dimension_semantics=("arbitrary",),
                                             vmem_limit_bytes=VMEM_LIMIT),
        name="gate_output",
    )(x2, o2, gpre, gpost, w_in, w_out)


def kernel(x, norm_pre, norm_post, a_w_in, a_w_s, a_b_s, a_vnorm_g, a_vnorm_b, a_w_out, b_w_in, b_w_out, rel_bias):
    bsz, s, dm = x.shape
    assert dm == D_MODEL and norm_pre.shape[0] == 2 and a_w_in.shape[0] == 1 and b_w_in.shape[0] == 1

    b_s = jnp.broadcast_to(a_b_s[0][:, :, None], (A_HEADS, A_CHUNK, A_HEAD_DIM))
    x2 = _gmlp_layer(x.reshape(bsz * s, dm), norm_pre[0:1], norm_post[0:1],
                     a_w_in, a_w_s[0], b_s, a_vnorm_g, a_vnorm_b, a_w_out)

    zs = _qkv_projection(x2.reshape(bsz, s, dm), norm_pre[1:2], b_w_in)
    o = _window_attention([z.reshape(bsz, s, B_GROUP_COLS) for z in zs], rel_bias)
    out = _gate_output(x2, o.reshape(bsz * s, B_WIDTH), norm_pre[1:2], norm_post[1:2], b_w_in, b_w_out)
    return out.reshape(bsz, s, dm)
```

```python
import functools
import math

import jax
import jax.numpy as jnp
import numpy as np
from jax import lax
from jax.experimental import pallas as pl
from jax.experimental.pallas import tpu as pltpu

F32 = jnp.float32
BF16 = jnp.bfloat16

EPS = 1e-6
NEG_INF = -1e30

D_MODEL = 1024
A_WIDTH = 2048
A_CHUNK = 128
A_HEADS = 16
A_HEAD_DIM = 128

B_GROUPS = ((128, 1), (512, 4), (2048, 16))
B_HEADS = 16
B_HEAD_DIM = 64
B_WIDTH = B_HEADS * B_HEAD_DIM
B_GROUP_COLS = 3 * B_WIDTH
B_QKV = len(B_GROUPS) * B_GROUP_COLS
REL_BUCKETS = 32
REL_EXACT = 8
REL_MAX_DISTANCE = 1024

LOG2E = math.log2(math.e)
LOGIT_SCALE = B_HEAD_DIM ** -0.5 * LOG2E

LANES = 128
COL_CHUNK = 512
GMLP_ROW_PARTS = 1
GATE_ROW_PARTS = 2
CAST_SLOTS = 4
CAST_ROWS = 256
CAST_ROWS_WIDE = 32
HEAD_PAIRS = B_WIDTH // LANES
Q_BLOCK = 128
K_WINDOW = 256
STATE_INTERLEAVE = 4
ATTN_UNROLL = 16
VMEM_LIMIT = 58 * 1024 * 1024


def _rms(x, g):
    return x * lax.rsqrt(jnp.mean(x * x, axis=-1, keepdims=True) + EPS) * g


def _cast_rows_once(src, dst, stage, sem, *, chunk, col_blocks=None):
    col_blocks = col_blocks or [(0, 0, dst.shape[1])]
    nchunk, nslot = dst.shape[0] // chunk, stage.shape[0]
    assert dst.shape[0] % chunk == 0 and nchunk >= nslot

    def copy(i):
        return pltpu.make_async_copy(src.at[pl.ds(i * chunk, chunk), :], stage.at[i % nslot], sem.at[i % nslot])

    for i in range(nslot - 1):
        copy(i).start()
    for i in range(nchunk):
        if i + nslot - 1 < nchunk:
            copy(i + nslot - 1).start()
        copy(i).wait()
        for s0, d0, width in col_blocks:
            dst[pl.ds(i * chunk, chunk), d0:d0 + width] = stage[i % nslot, :, s0:s0 + width].astype(BF16)


_GELU_B = -2.0 * math.sqrt(2.0 / math.pi) * LOG2E
_GELU_A = 0.044715 * _GELU_B


def _gelu_denominator(x):
    return 1.0 + jnp.exp2(x * (_GELU_A * (x * x) + _GELU_B))


def _silu_denominator(x):
    return 1.0 + jnp.exp2(x * -LOG2E)


def _gmlp_kernel(x_ref, gpre_ref, gpost_ref, win_hbm, ws_ref, bs_ref, vng_ref, vnb_ref, wout_hbm,
                 o_ref, win_ref, wout_ref, stage_in, stage_out, sem_in, sem_out, vn_scr, ug_scr, y_scr, *, tm):
    @pl.when(pl.program_id(0) == 0)
    def _():
        half = COL_CHUNK // 2
        pairs = [blk for c in range(A_WIDTH // half)
                 for blk in ((c * half, A_WIDTH + c * COL_CHUNK, half),
                             (2 * A_WIDTH + c * half, A_WIDTH + c * COL_CHUNK + half, half))]
        _cast_rows_once(win_hbm.at[0], win_ref, stage_in, sem_in, chunk=stage_in.shape[1],
                        col_blocks=[(A_WIDTH, 0, A_WIDTH)] + pairs)
        _cast_rows_once(wout_hbm.at[0], wout_ref, stage_out, sem_out, chunk=stage_out.shape[1])

    parts = [slice(k * (tm // GMLP_ROW_PARTS), (k + 1) * (tm // GMLP_ROW_PARTS)) for k in range(GMLP_ROW_PARTS)]
    hs = [_rms(x_ref[rows, :], gpre_ref[...]).astype(BF16) for rows in parts]

    def v_path(k):
        v = jnp.dot(hs[k], win_ref[:, :A_WIDTH], preferred_element_type=F32)
        v = v / _gelu_denominator(v)
        mu = jnp.mean(v, axis=-1, keepdims=True)
        vc = v - mu
        vn = vc * lax.rsqrt(jnp.mean(vc * vc, axis=-1, keepdims=True) + EPS)
        vn_scr[parts[k], :] = (vn * vng_ref[...] + vnb_ref[...]).astype(BF16)

    def gate_path(k):
        half = COL_CHUNK // 2
        for cb in range(A_WIDTH // half):
            z = jnp.dot(hs[k], win_ref[:, A_WIDTH + cb * COL_CHUNK:A_WIDTH + (cb + 1) * COL_CHUNK],
                        preferred_element_type=F32)
            u, g = z[:, :half], z[:, half:]
            ug_scr[parts[k], cb * half:(cb + 1) * half] = (u * g) / (_gelu_denominator(u) * _silu_denominator(g))

    def mix(k):
        chunks = range(parts[k].start // A_CHUNK, parts[k].stop // A_CHUNK)
        for hd in range(A_HEADS):
            c0 = hd * A_HEAD_DIM
            rhs = jnp.concatenate(
                [vn_scr[c * A_CHUNK:(c + 1) * A_CHUNK, c0:c0 + A_HEAD_DIM] for c in chunks], axis=1)
            sg = jnp.dot(ws_ref[hd].astype(BF16), rhs, preferred_element_type=F32)
            for n, c in enumerate(chunks):
                rows = slice(c * A_CHUNK, (c + 1) * A_CHUNK)
                sgc = sg[:, n * A_CHUNK:(n + 1) * A_CHUNK] + bs_ref[hd]
                y_scr[rows, c0:c0 + A_HEAD_DIM] = (ug_scr[rows, c0:c0 + A_HEAD_DIM] * sgc).astype(BF16)

    def finish(k):
        out = jnp.dot(y_scr[parts[k], :], wout_ref[...], preferred_element_type=F32)
        o_ref[parts[k], :] = x_ref[parts[k], :] + _rms(out, gpost_ref[...])

    for k in range(GMLP_ROW_PARTS):
        v_path(k)
    for k in range(GMLP_ROW_PARTS):
        gate_path(k)
        if k > 0:
            finish(k - 1)
        mix(k)
    finish(GMLP_ROW_PARTS - 1)


def _gmlp_layer(x2, gpre, gpost, w_in, w_s, b_s, vn_g, vn_b, w_out, *, tm=512):
    t = x2.shape[0]
    const = lambda *shape: pl.BlockSpec(shape, lambda i: (0,) * len(shape), pipeline_mode=pl.Buffered(1))
    return pl.pallas_call(
        functools.partial(_gmlp_kernel, tm=tm),
        grid=(t // tm,),
        in_specs=[
            pl.BlockSpec((tm, D_MODEL), lambda i: (i, 0)),
            const(1, D_MODEL), const(1, D_MODEL),
            pl.BlockSpec(memory_space=pl.ANY),
            const(A_HEADS, A_CHUNK, A_CHUNK),
            const(A_HEADS, A_CHUNK, A_HEAD_DIM),
            const(1, A_WIDTH), const(1, A_WIDTH),
            pl.BlockSpec(memory_space=pl.ANY),
        ],
        out_specs=pl.BlockSpec((tm, D_MODEL), lambda i: (i, 0)),
        out_shape=jax.ShapeDtypeStruct((t, D_MODEL), F32),
        scratch_shapes=[pltpu.VMEM((D_MODEL, 3 * A_WIDTH), BF16), pltpu.VMEM((A_WIDTH, D_MODEL), BF16),
                        pltpu.VMEM((CAST_SLOTS, CAST_ROWS_WIDE, 3 * A_WIDTH), F32),
                        pltpu.VMEM((CAST_SLOTS, CAST_ROWS, D_MODEL), F32),
                        pltpu.SemaphoreType.DMA((CAST_SLOTS,)), pltpu.SemaphoreType.DMA((CAST_SLOTS,)),
                        pltpu.VMEM((tm, A_WIDTH), BF16), pltpu.VMEM((tm, A_WIDTH), F32),
                        pltpu.VMEM((tm, A_WIDTH), BF16)],
        compiler_params=pltpu.CompilerParams(dimension_semantics=("arbitrary",),
                                             vmem_limit_bytes=VMEM_LIMIT),
        name="gmlp_layer",
    )(x2, gpre, gpost, w_in, w_s, b_s, vn_g, vn_b, w_out)


def _qkv_kernel(x_ref, gpre_ref, w_hbm, *refs, tm, dils):
    out_refs = refs[:len(dils)]
    h_scr, w_ref, stage, sem = refs[len(dils):]
    nslab = D_MODEL // LANES

    @pl.when((pl.program_id(0) == 0) & (pl.program_id(1) == 0))
    def _():
        _cast_rows_once(w_hbm.at[0, :, pl.ds(0, B_QKV)], w_ref, stage, sem, chunk=stage.shape[1])

    h = _rms(x_ref[...], gpre_ref[...])
    for k in range(nslab):
        h_scr[k] = h[:, k * LANES:(k + 1) * LANES]
    nc = 512

    def gathered(starts, rows, stride):
        return jnp.concatenate(
            [jnp.concatenate([h_scr[k, pl.ds(st, rows, stride=stride), :] for st in starts], axis=0)
             for k in range(nslab)], axis=1).astype(BF16)

    for gi, d in enumerate(dils):
        rows = tm // d
        hp = gathered(range(d), rows, d)
        if d == 1:
            per = Q_BLOCK // STATE_INTERLEAVE
            hq = gathered([blk * Q_BLOCK + c for blk in range(tm // Q_BLOCK) for c in range(STATE_INTERLEAVE)],
                          per, STATE_INTERLEAVE)
        else:
            hq = hp
        for c in range(B_GROUP_COLS // nc):
            c0 = gi * B_GROUP_COLS + c * nc
            is_q = (c + 1) * nc <= B_WIDTH
            z = jnp.dot(hq if is_q else hp, w_ref[:, c0:c0 + nc], preferred_element_type=F32)
            if is_q:
                z = z * LOGIT_SCALE
            out_refs[gi][:, :, c * nc:(c + 1) * nc] = z.reshape(d, rows, nc).astype(BF16)


def _qkv_projection(x3, gpre, w_qkv, *, tm=512):
    bsz, s, _ = x3.shape
    dils = tuple(d for _, d in B_GROUPS)
    return pl.pallas_call(
        functools.partial(_qkv_kernel, tm=tm, dils=dils),
        grid=(bsz, s // tm),
        in_specs=[
            pl.BlockSpec((None, tm, D_MODEL), lambda b, i: (b, i, 0)),
            pl.BlockSpec((1, D_MODEL), lambda b, i: (0, 0)),
            pl.BlockSpec(memory_space=pl.ANY),
        ],
        out_specs=[pl.BlockSpec((None, d, tm // d, B_GROUP_COLS), lambda b, i: (b, 0, i, 0)) for d in dils],
        out_shape=[jax.ShapeDtypeStruct((bsz, d, s // d, B_GROUP_COLS), BF16) for d in dils],
        scratch_shapes=[pltpu.VMEM((D_MODEL // LANES, tm, LANES), F32),
                        pltpu.VMEM((D_MODEL, B_QKV), BF16),
                        pltpu.VMEM((CAST_SLOTS, CAST_ROWS_WIDE, B_QKV), F32),
                        pltpu.SemaphoreType.DMA((CAST_SLOTS,))],
        compiler_params=pltpu.CompilerParams(dimension_semantics=("arbitrary", "arbitrary"),
                                             vmem_limit_bytes=VMEM_LIMIT),
        name="qkv_projection",
    )(x3, gpre, w_qkv)


def _t5_bucket_np(rel):
    half = REL_BUCKETS // 2
    ret = np.where(rel > 0, half, 0)
    n = np.abs(rel)
    nf = np.maximum(n, 1).astype(np.float32)
    large = REL_EXACT + (np.log(nf / np.float32(REL_EXACT)) / np.float32(math.log(REL_MAX_DISTANCE / REL_EXACT))
                         * np.float32(half - REL_EXACT)).astype(np.int32)
    large = np.minimum(large, half - 1)
    return (ret + np.where(n < REL_EXACT, n, large)).astype(np.int32)


def _bucket_row(dilation, half_w):
    rel = np.arange(K_WINDOW, dtype=np.int32) - half_w
    row = np.where(np.abs(rel) <= half_w, _t5_bucket_np(rel * dilation), -1).astype(np.int32)
    return np.broadcast_to(row, (8, K_WINDOW))


def _attn_kernel(tbl_ref, bkt_ref, *refs, seq, groups, unroll):
    ng = len(groups)
    z_refs, o_ref = refs[:ng], refs[ng]
    buf, sem, acc_scr, m_scr, den_scr, bm_scr, p_scr, pm_scr = refs[ng + 1:]
    per = Q_BLOCK // STATE_INTERLEAVE
    hp, b = pl.program_id(0), pl.program_id(1)
    nb = pl.num_programs(1)
    step = hp * nb + b
    nsteps = pl.num_programs(0) * nb

    def slab_copies(g, hp_, b_):
        return [pltpu.make_async_copy(
            z_refs[g].at[b_, :, pl.ds(pl.multiple_of((j * HEAD_PAIRS + hp_) * LANES, LANES), LANES)],
            buf.at[g, j], sem.at[g, j]) for j in range(3)]

    @pl.when(step == 0)
    def _():
        for cp in slab_copies(0, hp, b):
            cp.start()

    @pl.when(b == 0)
    def _():
        key = lax.broadcasted_iota(jnp.int32, (Q_BLOCK, K_WINDOW), 1)
        for g, (d, half_w, col) in enumerate(groups):
            for hh in range(2):
                def fill(bk, row, g=g, hh=hh, col=col):
                    return jnp.where(bkt_ref[g] == bk, tbl_ref[bk, col + 2 * hp + hh], row)
                row = lax.fori_loop(0, REL_BUCKETS, fill, jnp.full((8, K_WINDOW), NEG_INF, F32)) * LOG2E
                if d == 1:
                    mid = jnp.concatenate(
                        [pltpu.roll(jnp.broadcast_to(row[0:1], (per, K_WINDOW)), c, 1,
                                    stride=STATE_INTERLEAVE, stride_axis=0) for c in range(STATE_INTERLEAVE)], axis=0)
                else:
                    mid = pltpu.roll(jnp.broadcast_to(row[0:1], (Q_BLOCK, K_WINDOW)), 0, 1, stride=1, stride_axis=0)
                bm_scr[g, hh, 1] = mid
                bm_scr[g, hh, 0] = jnp.where(key < K_WINDOW - half_w,
                                             pltpu.roll(mid, K_WINDOW - half_w, 1), NEG_INF)
                bm_scr[g, hh, 2] = jnp.where(key >= half_w, pltpu.roll(mid, half_w, 1), NEG_INF)

    first_head = lax.broadcasted_iota(jnp.int32, (Q_BLOCK, LANES), 1) < B_HEAD_DIM
    ones = jnp.ones((K_WINDOW, LANES), BF16)

    def run_group(g):
        d, half_w, _ = groups[g]
        length = seq // d
        nblk = length // Q_BLOCK
        ntrips = d * nblk // unroll
        q_ref, k_ref, v_ref = buf.at[g, 0], buf.at[g, 1], buf.at[g, 2]

        def blocks(it):
            for u in range(unroll):
                if unroll >= nblk:
                    r, i = it * (unroll // nblk) + u // nblk, u % nblk
                else:
                    chunks = nblk // unroll
                    r = it // chunks if d > 1 else 0
                    i = (it - r * chunks) * unroll + u
                k0 = pl.multiple_of(r * length + jnp.clip(i * Q_BLOCK - half_w, 0, length - K_WINDOW), half_w)
                yield u, r, i, k0

        def logits_stage(it, slot):
            for u, r, i, k0 in blocks(it):
                q0 = pl.multiple_of(r * length + i * Q_BLOCK, Q_BLOCK)
                var = jnp.where(i == 0, 0, jnp.where(i == nblk - 1, 2, 1))
                qb = q_ref[pl.ds(q0, Q_BLOCK), :]
                kw = k_ref[pl.ds(k0, K_WINDOW), :]
                zero = jnp.zeros_like(qb)
                q2 = jnp.concatenate([jnp.where(first_head, qb, zero), jnp.where(first_head, zero, qb)], axis=0)
                s = lax.dot_general(q2, kw, (((1,), (1,)), ((), ())), preferred_element_type=F32)
                s = s + jnp.concatenate([bm_scr[g, 0, var], bm_scr[g, 1, var]], axis=0)
                m2 = jnp.max(s, axis=-1, keepdims=True)
                p_scr[slot, u] = jnp.exp2(s - m2).astype(BF16)
                pm_scr[slot, u] = jnp.where(first_head, m2[:Q_BLOCK], m2[Q_BLOCK:])

        def values_stage(it, slot):
            for u, r, i, k0 in blocks(it):
                vw = v_ref[pl.ds(k0, K_WINDOW), :]
                ov = jnp.dot(p_scr[slot, u], jnp.concatenate([vw, ones], axis=1), preferred_element_type=F32)
                acc = jnp.where(first_head, ov[:Q_BLOCK, :LANES], ov[Q_BLOCK:, :LANES])
                den = jnp.where(first_head, ov[:Q_BLOCK, LANES:], ov[Q_BLOCK:, LANES:])
                m = pm_scr[slot, u]
                if d == 1:
                    pieces = [(0, Q_BLOCK, pl.ds(i * Q_BLOCK, Q_BLOCK))]
                else:
                    n = Q_BLOCK // d
                    base = r // STATE_INTERLEAVE + per * (r % STATE_INTERLEAVE)
                    pieces = [(k * n, n, pl.ds((i * d + k) * Q_BLOCK + base, n, stride=d // STATE_INTERLEAVE))
                              for k in range(d)]
                gather = lambda ref: jnp.concatenate([ref[idx, :] for _, _, idx in pieces], axis=0)
                if g > 0:
                    m_old = gather(m_scr)
                    m_new = jnp.maximum(m_old, m)
                    w_old, w_cur = jnp.exp2(m_old - m_new), jnp.exp2(m - m_new)
                    acc = w_old * gather(acc_scr) + w_cur * acc
                    den = w_old * gather(den_scr) + w_cur * den
                    m = m_new
                if g == ng - 1:
                    o = acc / den
                    for c in range(STATE_INTERLEAVE):
                        o_ref[pl.ds(i * Q_BLOCK + c, per, stride=STATE_INTERLEAVE), :] = o[c * per:(c + 1) * per]
                else:
                    for ref, val in ((acc_scr, acc), (m_scr, m), (den_scr, den)):
                        for r0, n, idx in pieces:
                            ref[idx, :] = val[r0:r0 + n]

        logits_stage(0, 0)

        def pair(j, carry):
            logits_stage(2 * j + 1, 1)
            values_stage(2 * j, 0)
            logits_stage(2 * j + 2, 0)
            values_stage(2 * j + 1, 1)
            return carry

        lax.fori_loop(0, ntrips // 2 - 1, pair, 0)
        logits_stage(ntrips - 1, 1)
        values_stage(ntrips - 2, 0)
        values_stage(ntrips - 1, 1)

    for g in range(ng):
        if g + 1 < ng:
            for cp in slab_copies(g + 1, hp, b):
                cp.start()
        else:
            @pl.when(step + 1 < nsteps)
            def _():
                nxt = step + 1
                for cp in slab_copies(0, nxt // nb, nxt % nb):
                    cp.start()
        for cp in slab_copies(g, hp, b):
            cp.wait()
        run_group(g)


def _window_attention(zs, rel_bias):
    bsz, s, _ = zs[0].shape
    order = sorted(range(len(B_GROUPS)), key=lambda gi: -B_GROUPS[gi][1])
    groups = tuple((B_GROUPS[gi][1], B_GROUPS[gi][0] // (2 * B_GROUPS[gi][1]), gi * B_HEADS) for gi in order)
    zs = [zs[gi] for gi in order]
    assert groups[-1][0] == 1
    for d, half_w, _ in groups:
        nblk = s // d // Q_BLOCK
        assert K_WINDOW == Q_BLOCK + 2 * half_w and s // d >= K_WINDOW and s % (d * Q_BLOCK) == 0
        assert (d * nblk) % (2 * ATTN_UNROLL) == 0 and (ATTN_UNROLL % nblk == 0 or nblk % ATTN_UNROLL == 0)
        assert d == 1 or (d % STATE_INTERLEAVE == 0 and Q_BLOCK % d == 0)
    bkt = jnp.asarray(np.stack([_bucket_row(d, half_w) for d, half_w, _ in groups]))
    return pl.pallas_call(
        functools.partial(_attn_kernel, seq=s, groups=groups, unroll=ATTN_UNROLL),
        grid=(HEAD_PAIRS, bsz),
        in_specs=[pl.BlockSpec(memory_space=pltpu.SMEM),
                  pl.BlockSpec(bkt.shape, lambda hp, b: (0, 0, 0))]
                 + [pl.BlockSpec(memory_space=pl.ANY)] * len(groups),
        out_specs=pl.BlockSpec((None, s, LANES), lambda hp, b: (b, 0, hp)),
        out_shape=jax.ShapeDtypeStruct((bsz, s, B_WIDTH), F32),
        scratch_shapes=[pltpu.VMEM((len(groups), 3, s, LANES), BF16),
                        pltpu.SemaphoreType.DMA((len(groups), 3)),
                        pltpu.VMEM((s, LANES), F32),
                        pltpu.VMEM((s, LANES), F32),
                        pltpu.VMEM((s, LANES), F32),
                        pltpu.VMEM((len(groups), 2, 3, Q_BLOCK, K_WINDOW), F32),
                        pltpu.VMEM((2, ATTN_UNROLL, 2 * Q_BLOCK, K_WINDOW), BF16),
                        pltpu.VMEM((2, ATTN_UNROLL, Q_BLOCK, LANES), F32)],
        compiler_params=pltpu.CompilerParams(dimension_semantics=("arbitrary", "arbitrary"),
                                             vmem_limit_bytes=VMEM_LIMIT),
        name="window_attention",
    )(rel_bias.astype(F32), bkt, *zs)


def _out_kernel(x_ref, o_ref, gpre_ref, gpost_ref, win_hbm, wout_hbm, out_ref, wg_ref, wout_ref, stage, sem):
    @pl.when(pl.program_id(0) == 0)
    def _():
        _cast_rows_once(win_hbm.at[0, :, pl.ds(B_QKV, B_WIDTH)], wg_ref, stage, sem, chunk=stage.shape[1])
        _cast_rows_once(wout_hbm.at[0], wout_ref, stage, sem, chunk=stage.shape[1])

    tm = x_ref.shape[0]
    parts = [slice(k * (tm // GATE_ROW_PARTS), (k + 1) * (tm // GATE_ROW_PARTS)) for k in range(GATE_ROW_PARTS)]
    ys = []
    for rows in parts:
        h = _rms(x_ref[rows, :], gpre_ref[...]).astype(BF16)
        gate = jnp.dot(h, wg_ref[...], preferred_element_type=F32)
        ys.append((o_ref[rows, :] * gate / _silu_denominator(gate)).astype(BF16))
    for rows, y in zip(parts, ys):
        out = jnp.dot(y, wout_ref[...], preferred_element_type=F32)
        out_ref[rows, :] = x_ref[rows, :] + _rms(out, gpost_ref[...])


def _gate_output(x2, o2, gpre, gpost, w_in, w_out, *, tm=512):
    t = x2.shape[0]
    const = lambda *shape: pl.BlockSpec(shape, lambda i: (0,) * len(shape), pipeline_mode=pl.Buffered(1))
    row = pl.BlockSpec((tm, D_MODEL), lambda i: (i, 0))
    return pl.pallas_call(
        _out_kernel,
        grid=(t // tm,),
        in_specs=[row, row, const(1, D_MODEL), const(1, D_MODEL),
                  pl.BlockSpec(memory_space=pl.ANY), pl.BlockSpec(memory_space=pl.ANY)],
        out_specs=row,
        out_shape=jax.ShapeDtypeStruct((t, D_MODEL), F32),
        scratch_shapes=[pltpu.VMEM((D_MODEL, B_WIDTH), BF16), pltpu.VMEM((B_WIDTH, D_MODEL), BF16),
                        pltpu.VMEM((CAST_SLOTS, CAST_ROWS, D_MODEL), F32),
                        pltpu.SemaphoreType.DMA((CAST_SLOTS,))],
        compiler_params=pltpu.CompilerParams(dimension_semantics=("arbitrary",),
                                             vmem_limit_bytes=VMEM_LIMIT),
        name="gate_output",
    )(x2, o2, gpre, gpost, w_in, w_out)


def kernel(x, norm_pre, norm_post, a_w_in, a_w_s, a_b_s, a_vnorm_g, a_vnorm_b, a_w_out, b_w_in, b_w_out, rel_bias):
    bsz, s, dm = x.shape
    assert dm == D_MODEL and norm_pre.shape[0] == 2 and a_w_in.shape[0] == 1 and b_w_in.shape[0] == 1

    b_s = jnp.broadcast_to(a_b_s[0][:, :, None], (A_HEADS, A_CHUNK, A_HEAD_DIM))
    x2 = _gmlp_layer(x.reshape(bsz * s, dm), norm_pre[0:1], norm_post[0:1],
                     a_w_in, a_w_s[0], b_s, a_vnorm_g, a_vnorm_b, a_w_out)

    zs = _qkv_projection(x2.reshape(bsz, s, dm), norm_pre[1:2], b_w_in)
    o = _window_attention([z.reshape(bsz, s, B_GROUP_COLS) for z in zs], rel_bias)
    out = _gate_output(x2, o.reshape(bsz * s, B_WIDTH), norm_pre[1:2], norm_post[1:2], b_w_in, b_w_out)
    return out.reshape(bsz, s, dm)
```

```python
import functools
import math

import jax
import jax.numpy as jnp
import numpy as np
from jax import lax
from jax.experimental import pallas as pl
from jax.experimental.pallas import tpu as pltpu

F32 = jnp.float32
BF16 = jnp.bfloat16

EPS = 1e-6
NEG_INF = -1e30

D_MODEL = 1024
A_WIDTH = 2048
A_CHUNK = 128
A_HEADS = 16
A_HEAD_DIM = 128

B_GROUPS = ((128, 1), (512, 4), (2048, 16))
B_HEADS = 16
B_HEAD_DIM = 64
B_WIDTH = B_HEADS * B_HEAD_DIM
B_GROUP_COLS = 3 * B_WIDTH
B_QKV = len(B_GROUPS) * B_GROUP_COLS
REL_BUCKETS = 32
REL_EXACT = 8
REL_MAX_DISTANCE = 1024

LOG2E = math.log2(math.e)
LOGIT_SCALE = B_HEAD_DIM ** -0.5 * LOG2E

LANES = 128
COL_CHUNK = 512
GMLP_ROW_PARTS = 1
GATE_ROW_PARTS = 2
CAST_SLOTS = 4
CAST_ROWS = 256
CAST_ROWS_WIDE = 32
HEAD_PAIRS = B_WIDTH // LANES
Q_BLOCK = 128
K_WINDOW = 256
STATE_INTERLEAVE = 4
ATTN_UNROLL = 16
VMEM_LIMIT = 58 * 1024 * 1024


def _rms(x, g):
    return x * lax.rsqrt(jnp.mean(x * x, axis=-1, keepdims=True) + EPS) * g


def _cast_rows_once(src, dst, stage, sem, *, chunk, col_blocks=None):
    col_blocks = col_blocks or [(0, 0, dst.shape[1])]
    nchunk, nslot = dst.shape[0] // chunk, stage.shape[0]
    assert dst.shape[0] % chunk == 0 and nchunk >= nslot

    def copy(i):
        return pltpu.make_async_copy(src.at[pl.ds(i * chunk, chunk), :], stage.at[i % nslot], sem.at[i % nslot])

    for i in range(nslot - 1):
        copy(i).start()
    for i in range(nchunk):
        if i + nslot - 1 < nchunk:
            copy(i + nslot - 1).start()
        copy(i).wait()
        for s0, d0, width in col_blocks:
            dst[pl.ds(i * chunk, chunk), d0:d0 + width] = stage[i % nslot, :, s0:s0 + width].astype(BF16)


_GELU_B = -2.0 * math.sqrt(2.0 / math.pi) * LOG2E
_GELU_A = 0.044715 * _GELU_B


def _gelu_denominator(x):
    return 1.0 + jnp.exp2(x * (_GELU_A * (x * x) + _GELU_B))


def _silu_denominator(x):
    return 1.0 + jnp.exp2(x * -LOG2E)


def _gmlp_kernel(x_ref, gpre_ref, gpost_ref, win_hbm, ws_ref, bs_ref, vng_ref, vnb_ref, wout_hbm,
                 o_ref, win_ref, wout_ref, stage_in, stage_out, sem_in, sem_out, vn_scr, ug_scr, y_scr, *, tm):
    @pl.when(pl.program_id(0) == 0)
    def _():
        half = COL_CHUNK // 2
        pairs = [blk for c in range(A_WIDTH // half)
                 for blk in ((c * half, A_WIDTH + c * COL_CHUNK, half),
                             (2 * A_WIDTH + c * half, A_WIDTH + c * COL_CHUNK + half, half))]
        _cast_rows_once(win_hbm.at[0], win_ref, stage_in, sem_in, chunk=stage_in.shape[1],
                        col_blocks=[(A_WIDTH, 0, A_WIDTH)] + pairs)
        _cast_rows_once(wout_hbm.at[0], wout_ref, stage_out, sem_out, chunk=stage_out.shape[1])

    parts = [slice(k * (tm // GMLP_ROW_PARTS), (k + 1) * (tm // GMLP_ROW_PARTS)) for k in range(GMLP_ROW_PARTS)]
    hs = [_rms(x_ref[rows, :], gpre_ref[...]).astype(BF16) for rows in parts]

    def v_path(k):
        v = jnp.dot(hs[k], win_ref[:, :A_WIDTH], preferred_element_type=F32)
        v = v / _gelu_denominator(v)
        mu = jnp.mean(v, axis=-1, keepdims=True)
        vc = v - mu
        vn = vc * lax.rsqrt(jnp.mean(vc * vc, axis=-1, keepdims=True) + EPS)
        vn_scr[parts[k], :] = (vn * vng_ref[...] + vnb_ref[...]).astype(BF16)

    def gate_path(k):
        half = COL_CHUNK // 2
        for cb in range(A_WIDTH // half):
            z = jnp.dot(hs[k], win_ref[:, A_WIDTH + cb * COL_CHUNK:A_WIDTH + (cb + 1) * COL_CHUNK],
                        preferred_element_type=F32)
            u, g = z[:, :half], z[:, half:]
            ug_scr[parts[k], cb * half:(cb + 1) * half] = (u * g) / (_gelu_denominator(u) * _silu_denominator(g))

    def mix(k):
        chunks = range(parts[k].start // A_CHUNK, parts[k].stop // A_CHUNK)
        for hd in range(A_HEADS):
            c0 = hd * A_HEAD_DIM
            rhs = jnp.concatenate(
                [vn_scr[c * A_CHUNK:(c + 1) * A_CHUNK, c0:c0 + A_HEAD_DIM] for c in chunks], axis=1)
            sg = jnp.dot(ws_ref[hd].astype(BF16), rhs, preferred_element_type=F32)
            for n, c in enumerate(chunks):
                rows = slice(c * A_CHUNK, (c + 1) * A_CHUNK)
                sgc = sg[:, n * A_CHUNK:(n + 1) * A_CHUNK] + bs_ref[hd]
                y_scr[rows, c0:c0 + A_HEAD_DIM] = (ug_scr[rows, c0:c0 + A_HEAD_DIM] * sgc).astype(BF16)

    def finish(k):
        out = jnp.dot(y_scr[parts[k], :], wout_ref[...], preferred_element_type=F32)
        o_ref[parts[k], :] = x_ref[parts[k], :] + _rms(out, gpost_ref[...])

    for k in range(GMLP_ROW_PARTS):
        v_path(k)
    for k in range(GMLP_ROW_PARTS):
        gate_path(k)
        if k > 0:
            finish(k - 1)
        mix(k)
    finish(GMLP_ROW_PARTS - 1)


def _gmlp_layer(x2, gpre, gpost, w_in, w_s, b_s, vn_g, vn_b, w_out, *, tm=512):
    t = x2.shape[0]
    const = lambda *shape: pl.BlockSpec(shape, lambda i: (0,) * len(shape), pipeline_mode=pl.Buffered(1))
    return pl.pallas_call(
        functools.partial(_gmlp_kernel, tm=tm),
        grid=(t // tm,),
        in_specs=[
            pl.BlockSpec((tm, D_MODEL), lambda i: (i, 0)),
            const(1, D_MODEL), const(1, D_MODEL),
            pl.BlockSpec(memory_space=pl.ANY),
            const(A_HEADS, A_CHUNK, A_CHUNK),
            const(A_HEADS, A_CHUNK, A_HEAD_DIM),
            const(1, A_WIDTH), const(1, A_WIDTH),
            pl.BlockSpec(memory_space=pl.ANY),
        ],
        out_specs=pl.BlockSpec((tm, D_MODEL), lambda i: (i, 0)),
        out_shape=jax.ShapeDtypeStruct((t, D_MODEL), F32),
        scratch_shapes=[pltpu.VMEM((D_MODEL, 3 * A_WIDTH), BF16), pltpu.VMEM((A_WIDTH, D_MODEL), BF16),
                        pltpu.VMEM((CAST_SLOTS, CAST_ROWS_WIDE, 3 * A_WIDTH), F32),
                        pltpu.VMEM((CAST_SLOTS, CAST_ROWS, D_MODEL), F32),
                        pltpu.SemaphoreType.DMA((CAST_SLOTS,)), pltpu.SemaphoreType.DMA((CAST_SLOTS,)),
                        pltpu.VMEM((tm, A_WIDTH), BF16), pltpu.VMEM((tm, A_WIDTH), F32),
                        pltpu.VMEM((tm, A_WIDTH), BF16)],
        compiler_params=pltpu.CompilerParams(dimension_semantics=("arbitrary",),
                                             vmem_limit_bytes=VMEM_LIMIT),
        name="gmlp_layer",
    )(x2, gpre, gpost, w_in, w_s, b_s, vn_g, vn_b, w_out)


def _qkv_kernel(x_ref, gpre_ref, w_hbm, *refs, tm, dils):
    out_refs = refs[:len(dils)]
    h_scr, w_ref, stage, sem = refs[len(dils):]
    nslab = D_MODEL // LANES

    @pl.when((pl.program_id(0) == 0) & (pl.program_id(1) == 0))
    def _():
        _cast_rows_once(w_hbm.at[0, :, pl.ds(0, B_QKV)], w_ref, stage, sem, chunk=stage.shape[1])

    h = _rms(x_ref[...], gpre_ref[...])
    for k in range(nslab):
        h_scr[k] = h[:, k * LANES:(k + 1) * LANES]
    nc = 512

    def gathered(starts, rows, stride):
        return jnp.concatenate(
            [jnp.concatenate([h_scr[k, pl.ds(st, rows, stride=stride), :] for st in starts], axis=0)
             for k in range(nslab)], axis=1).astype(BF16)

    for gi, d in enumerate(dils):
        rows = tm // d
        hp = gathered(range(d), rows, d)
        if d == 1:
            per = Q_BLOCK // STATE_INTERLEAVE
            hq = gathered([blk * Q_BLOCK + c for blk in range(tm // Q_BLOCK) for c in range(STATE_INTERLEAVE)],
                          per, STATE_INTERLEAVE)
        else:
            hq = hp
        for c in range(B_GROUP_COLS // nc):
            c0 = gi * B_GROUP_COLS + c * nc
            is_q = (c + 1) * nc <= B_WIDTH
            z = jnp.dot(hq if is_q else hp, w_ref[:, c0:c0 + nc], preferred_element_type=F32)
            if is_q:
                z = z * LOGIT_SCALE
            out_refs[gi][:, :, c * nc:(c + 1) * nc] = z.reshape(d, rows, nc).astype(BF16)


def _qkv_projection(x3, gpre, w_qkv, *, tm=512):
    bsz, s, _ = x3.shape
    dils = tuple(d for _, d in B_GROUPS)
    return pl.pallas_call(
        functools.partial(_qkv_kernel, tm=tm, dils=dils),
        grid=(bsz, s // tm),
        in_specs=[
            pl.BlockSpec((None, tm, D_MODEL), lambda b, i: (b, i, 0)),
            pl.BlockSpec((1, D_MODEL), lambda b, i: (0, 0)),
            pl.BlockSpec(memory_space=pl.ANY),
        ],
        out_specs=[pl.BlockSpec((None, d, tm // d, B_GROUP_COLS), lambda b, i: (b, 0, i, 0)) for d in dils],
        out_shape=[jax.ShapeDtypeStruct((bsz, d, s // d, B_GROUP_COLS), BF16) for d in dils],
        scratch_shapes=[pltpu.VMEM((D_MODEL // LANES, tm, LANES), F32),
                        pltpu.VMEM((D_MODEL, B_QKV), BF16),
                        pltpu.VMEM((CAST_SLOTS, CAST_ROWS_WIDE, B_QKV), F32),
                        pltpu.SemaphoreType.DMA((CAST_SLOTS,))],
        compiler_params=pltpu.CompilerParams(dimension_semantics=("arbitrary", "arbitrary"),
                                             vmem_limit_bytes=VMEM_LIMIT),
        name="qkv_projection",
    )(x3, gpre, w_qkv)


def _t5_bucket_np(rel):
    half = REL_BUCKETS // 2
    ret = np.where(rel > 0, half, 0)
    n = np.abs(rel)
    nf = np.maximum(n, 1).astype(np.float32)
    large = REL_EXACT + (np.log(nf / np.float32(REL_EXACT)) / np.float32(math.log(REL_MAX_DISTANCE / REL_EXACT))
                         * np.float32(half - REL_EXACT)).astype(np.int32)
    large = np.minimum(large, half - 1)
    return (ret + np.where(n < REL_EXACT, n, large)).astype(np.int32)


def _bucket_row(dilation, half_w):
    rel = np.arange(K_WINDOW, dtype=np.int32) - half_w
    row = np.where(np.abs(rel) <= half_w, _t5_bucket_np(rel * dilation), -1).astype(np.int32)
    return np.broadcast_to(row, (8, K_WINDOW))


def _attn_kernel(tbl_ref, bkt_ref, *refs, seq, groups, unroll):
    ng = len(groups)
    z_refs, o_ref = refs[:ng], refs[ng]
    buf, sem, acc_scr, m_scr, den_scr, bm_scr, p_scr, pm_scr = refs[ng + 1:]
    per = Q_BLOCK // STATE_INTERLEAVE
    hp, b = pl.program_id(0), pl.program_id(1)
    nb = pl.num_programs(1)
    step = hp * nb + b
    nsteps = pl.num_programs(0) * nb

    def slab_copies(g, hp_, b_):
        return [pltpu.make_async_copy(
            z_refs[g].at[b_, :, pl.ds(pl.multiple_of((j * HEAD_PAIRS + hp_) * LANES, LANES), LANES)],
            buf.at[g, j], sem.at[g, j]) for j in range(3)]

    @pl.when(step == 0)
    def _():
        for cp in slab_copies(0, hp, b):
            cp.start()

    @pl.when(b == 0)
    def _():
        key = lax.broadcasted_iota(jnp.int32, (Q_BLOCK, K_WINDOW), 1)
        for g, (d, half_w, col) in enumerate(groups):
            for hh in range(2):
                def fill(bk, row, g=g, hh=hh, col=col):
                    return jnp.where(bkt_ref[g] == bk, tbl_ref[bk, col + 2 * hp + hh], row)
                row = lax.fori_loop(0, REL_BUCKETS, fill, jnp.full((8, K_WINDOW), NEG_INF, F32)) * LOG2E
                if d == 1:
                    mid = jnp.concatenate(
                        [pltpu.roll(jnp.broadcast_to(row[0:1], (per, K_WINDOW)), c, 1,
                                    stride=STATE_INTERLEAVE, stride_axis=0) for c in range(STATE_INTERLEAVE)], axis=0)
                else:
                    mid = pltpu.roll(jnp.broadcast_to(row[0:1], (Q_BLOCK, K_WINDOW)), 0, 1, stride=1, stride_axis=0)
                bm_scr[g, hh, 1] = mid
                bm_scr[g, hh, 0] = jnp.where(key < K_WINDOW - half_w,
                                             pltpu.roll(mid, K_WINDOW - half_w, 1), NEG_INF)
                bm_scr[g, hh, 2] = jnp.where(key >= half_w, pltpu.roll(mid, half_w, 1), NEG_INF)

    first_head = lax.broadcasted_iota(jnp.int32, (Q_BLOCK, LANES), 1) < B_HEAD_DIM
    key_first = lax.broadcasted_iota(jnp.int32, (K_WINDOW, LANES), 1) < B_HEAD_DIM
    ones_first = jnp.where(key_first, 1.0, 0.0).astype(BF16)
    ones_second = jnp.where(key_first, 0.0, 1.0).astype(BF16)

    def run_group(g):
        d, half_w, _ = groups[g]
        length = seq // d
        nblk = length // Q_BLOCK
        ntrips = d * nblk // unroll
        q_ref, k_ref, v_ref = buf.at[g, 0], buf.at[g, 1], buf.at[g, 2]

        def blocks(it):
            for u in range(unroll):
                if unroll >= nblk:
                    r, i = it * (unroll // nblk) + u // nblk, u % nblk
                else:
                    chunks = nblk // unroll
                    r = it // chunks if d > 1 else 0
                    i = (it - r * chunks) * unroll + u
                k0 = pl.multiple_of(r * length + jnp.clip(i * Q_BLOCK - half_w, 0, length - K_WINDOW), half_w)
                yield u, r, i, k0

        def logits_stage(it, slot):
            for u, r, i, k0 in blocks(it):
                q0 = pl.multiple_of(r * length + i * Q_BLOCK, Q_BLOCK)
                var = jnp.where(i == 0, 0, jnp.where(i == nblk - 1, 2, 1))
                qb = q_ref[pl.ds(q0, Q_BLOCK), :]
                kw = k_ref[pl.ds(k0, K_WINDOW), :]
                zero = jnp.zeros_like(qb)
                q2 = jnp.concatenate([jnp.where(first_head, qb, zero), jnp.where(first_head, zero, qb)], axis=0)
                s = lax.dot_general(q2, kw, (((1,), (1,)), ((), ())), preferred_element_type=F32)
                s = s + jnp.concatenate([bm_scr[g, 0, var], bm_scr[g, 1, var]], axis=0)
                m2 = jnp.max(s, axis=-1, keepdims=True)
                p_scr[slot, u] = jnp.exp2(s - m2).astype(BF16)
                pm_scr[slot, u] = jnp.where(first_head, m2[:Q_BLOCK], m2[Q_BLOCK:])

        def values_stage(it, slot):
            for u, r, i, k0 in blocks(it):
                vw = v_ref[pl.ds(k0, K_WINDOW), :]
                p = p_scr[slot, u]
                zero = jnp.zeros_like(vw)
                rhs = jnp.concatenate(
                    [jnp.concatenate([jnp.where(key_first, vw, zero), ones_first], axis=1),
                     jnp.concatenate([jnp.where(key_first, zero, vw), ones_second], axis=1)], axis=0)
                ov = jnp.dot(jnp.concatenate([p[:Q_BLOCK], p[Q_BLOCK:]], axis=1), rhs, preferred_element_type=F32)
                acc, den = ov[:, :LANES], ov[:, LANES:]
                m = pm_scr[slot, u]
                if d == 1:
                    pieces = [(0, Q_BLOCK, pl.ds(i * Q_BLOCK, Q_BLOCK))]
                else:
                    n = Q_BLOCK // d
                    base = r // STATE_INTERLEAVE + per * (r % STATE_INTERLEAVE)
                    pieces = [(k * n, n, pl.ds((i * d + k) * Q_BLOCK + base, n, stride=d // STATE_INTERLEAVE))
                              for k in range(d)]
                gather = lambda ref: jnp.concatenate([ref[idx, :] for _, _, idx in pieces], axis=0)
                if g > 0:
                    m_old = gather(m_scr)
                    m_new = jnp.maximum(m_old, m)
                    w_old, w_cur = jnp.exp2(m_old - m_new), jnp.exp2(m - m_new)
                    acc = w_old * gather(acc_scr) + w_cur * acc
                    den = w_old * gather(den_scr) + w_cur * den
                    m = m_new
                if g == ng - 1:
                    o = acc / den
                    for c in range(STATE_INTERLEAVE):
                        o_ref[pl.ds(i * Q_BLOCK + c, per, stride=STATE_INTERLEAVE), :] = o[c * per:(c + 1) * per]
                else:
                    for ref, val in ((acc_scr, acc), (m_scr, m), (den_scr, den)):
                        for r0, n, idx in pieces:
                            ref[idx, :] = val[r0:r0 + n]

        logits_stage(0, 0)

        def pair(j, carry):
            logits_stage(2 * j + 1, 1)
            values_stage(2 * j, 0)
            logits_stage(2 * j + 2, 0)
            values_stage(2 * j + 1, 1)
            return carry

        lax.fori_loop(0, ntrips // 2 - 1, pair, 0)
        logits_stage(ntrips - 1, 1)
        values_stage(ntrips - 2, 0)
        values_stage(ntrips - 1, 1)

    for g in range(ng):
        if g + 1 < ng:
            for cp in slab_copies(g + 1, hp, b):
                cp.start()
        else:
            @pl.when(step + 1 < nsteps)
            def _():
                nxt = step + 1
                for cp in slab_copies(0, nxt // nb, nxt % nb):
                    cp.start()
        for cp in slab_copies(g, hp, b):
            cp.wait()
        run_group(g)


def _window_attention(zs, rel_bias):
    bsz, s, _ = zs[0].shape
    order = sorted(range(len(B_GROUPS)), key=lambda gi: -B_GROUPS[gi][1])
    groups = tuple((B_GROUPS[gi][1], B_GROUPS[gi][0] // (2 * B_GROUPS[gi][1]), gi * B_HEADS) for gi in order)
    zs = [zs[gi] for gi in order]
    assert groups[-1][0] == 1
    for d, half_w, _ in groups:
        nblk = s // d // Q_BLOCK
        assert K_WINDOW == Q_BLOCK + 2 * half_w and s // d >= K_WINDOW and s % (d * Q_BLOCK) == 0
        assert (d * nblk) % (2 * ATTN_UNROLL) == 0 and (ATTN_UNROLL % nblk == 0 or nblk % ATTN_UNROLL == 0)
        assert d == 1 or (d % STATE_INTERLEAVE == 0 and Q_BLOCK % d == 0)
    bkt = jnp.asarray(np.stack([_bucket_row(d, half_w) for d, half_w, _ in groups]))
    return pl.pallas_call(
        functools.partial(_attn_kernel, seq=s, groups=groups, unroll=ATTN_UNROLL),
        grid=(HEAD_PAIRS, bsz),
        in_specs=[pl.BlockSpec(memory_space=pltpu.SMEM),
                  pl.BlockSpec(bkt.shape, lambda hp, b: (0, 0, 0))]
                 + [pl.BlockSpec(memory_space=pl.ANY)] * len(groups),
        out_specs=pl.BlockSpec((None, s, LANES), lambda hp, b: (b, 0, hp)),
        out_shape=jax.ShapeDtypeStruct((bsz, s, B_WIDTH), F32),
        scratch_shapes=[pltpu.VMEM((len(groups), 3, s, LANES), BF16),
                        pltpu.SemaphoreType.DMA((len(groups), 3)),
                        pltpu.VMEM((s, LANES), F32),
                        pltpu.VMEM((s, LANES), F32),
                        pltpu.VMEM((s, LANES), F32),
                        pltpu.VMEM((len(groups), 2, 3, Q_BLOCK, K_WINDOW), F32),
                        pltpu.VMEM((2, ATTN_UNROLL, 2 * Q_BLOCK, K_WINDOW), BF16),
                        pltpu.VMEM((2, ATTN_UNROLL, Q_BLOCK, LANES), F32)],
        compiler_params=pltpu.CompilerParams(dimension_semantics=("arbitrary", "arbitrary"),
                                             vmem_limit_bytes=VMEM_LIMIT),
        name="window_attention",
    )(rel_bias.astype(F32), bkt, *zs)


def _out_kernel(x_ref, o_ref, gpre_ref, gpost_ref, win_hbm, wout_hbm, out_ref, wg_ref, wout_ref, stage, sem):
    @pl.when(pl.program_id(0) == 0)
    def _():
        _cast_rows_once(win_hbm.at[0, :, pl.ds(B_QKV, B_WIDTH)], wg_ref, stage, sem, chunk=stage.shape[1])
        _cast_rows_once(wout_hbm.at[0], wout_ref, stage, sem, chunk=stage.shape[1])

    tm = x_ref.shape[0]
    parts = [slice(k * (tm // GATE_ROW_PARTS), (k + 1) * (tm // GATE_ROW_PARTS)) for k in range(GATE_ROW_PARTS)]
    ys = []
    for rows in parts:
        h = _rms(x_ref[rows, :], gpre_ref[...]).astype(BF16)
        gate = jnp.dot(h, wg_ref[...], preferred_element_type=F32)
        ys.append((o_ref[rows, :] * gate / _silu_denominator(gate)).astype(BF16))
    for rows, y in zip(parts, ys):
        out = jnp.dot(y, wout_ref[...], preferred_element_type=F32)
        out_ref[rows, :] = x_ref[rows, :] + _rms(out, gpost_ref[...])


def _gate_output(x2, o2, gpre, gpost, w_in, w_out, *, tm=512):
    t = x2.shape[0]
    const = lambda *shape: pl.BlockSpec(shape, lambda i: (0,) * len(shape), pipeline_mode=pl.Buffered(1))
    row = pl.BlockSpec((tm, D_MODEL), lambda i: (i, 0))
    return pl.pallas_call(
        _out_kernel,
        grid=(t // tm,),
        in_specs=[row, row, const(1, D_MODEL), const(1, D_MODEL),
                  pl.BlockSpec(memory_space=pl.ANY), pl.BlockSpec(memory_space=pl.ANY)],
        out_specs=row,
        out_shape=jax.ShapeDtypeStruct((t, D_MODEL), F32),
        scratch_shapes=[pltpu.VMEM((D_MODEL, B_WIDTH), BF16), pltpu.VMEM((B_WIDTH, D_MODEL), BF16),
                        pltpu.VMEM((CAST_SLOTS, CAST_ROWS, D_MODEL), F32),
                        pltpu.SemaphoreType.DMA((CAST_SLOTS,))],
        compiler_params=pltpu.CompilerParams(dimension_semantics=("arbitrary",),
                                             vmem_limit_bytes=VMEM_LIMIT),
        name="gate_output",
    )(x2, o2, gpre, gpost, w_in, w_out)


def kernel(x, norm_pre, norm_post, a_w_in, a_w_s, a_b_s, a_vnorm_g, a_vnorm_b, a_w_out, b_w_in, b_w_out, rel_bias):
    bsz, s, dm = x.shape
    assert dm == D_MODEL and norm_pre.shape[0] == 2 and a_w_in.shape[0] == 1 and b_w_in.shape[0] == 1

    b_s = jnp.broadcast_to(a_b_s[0][:, :, None], (A_HEADS, A_CHUNK, A_HEAD_DIM))
    x2 = _gmlp_layer(x.reshape(bsz * s, dm), norm_pre[0:1], norm_post[0:1],
                     a_w_in, a_w_s[0], b_s, a_vnorm_g, a_vnorm_b, a_w_out)

    zs = _qkv_projection(x2.reshape(bsz, s, dm), norm_pre[1:2], b_w_in)
    o = _window_attention([z.reshape(bsz, s, B_GROUP_COLS) for z in zs], rel_bias)
    out = _gate_output(x2, o.reshape(bsz * s, B_WIDTH), norm_pre[1:2], norm_post[1:2], b_w_in, b_w_out)
    return out.reshape(bsz, s, dm)
```

```python
import functools
import math

import jax
import jax.numpy as jnp
import numpy as np
from jax import lax
from jax.experimental import pallas as pl
from jax.experimental.pallas import tpu as pltpu

F32 = jnp.float32
BF16 = jnp.bfloat16

EPS = 1e-6
NEG_INF = -1e30

D_MODEL = 1024
A_WIDTH = 2048
A_CHUNK = 128
A_HEADS = 16
A_HEAD_DIM = 128

B_GROUPS = ((128, 1), (512, 4), (2048, 16))
B_HEADS = 16
B_HEAD_DIM = 64
B_WIDTH = B_HEADS * B_HEAD_DIM
B_GROUP_COLS = 3 * B_WIDTH
B_QKV = len(B_GROUPS) * B_GROUP_COLS
REL_BUCKETS = 32
REL_EXACT = 8
REL_MAX_DISTANCE = 1024

LOG2E = math.log2(math.e)
LOGIT_SCALE = B_HEAD_DIM ** -0.5 * LOG2E

LANES = 128
COL_CHUNK = 512
GMLP_ROW_PARTS = 1
GATE_ROW_PARTS = 2
CAST_SLOTS = 4
CAST_ROWS = 256
CAST_ROWS_WIDE = 32
HEAD_PAIRS = B_WIDTH // LANES
Q_BLOCK = 128
K_WINDOW = 256
STATE_INTERLEAVE = 4
ATTN_UNROLL = 16
VMEM_LIMIT = 58 * 1024 * 1024


def _rms(x, g):
    return x * lax.rsqrt(jnp.mean(x * x, axis=-1, keepdims=True) + EPS) * g


def _cast_rows_once(src, dst, stage, sem, *, chunk, col_blocks=None):
    col_blocks = col_blocks or [(0, 0, dst.shape[1])]
    nchunk, nslot = dst.shape[0] // chunk, stage.shape[0]
    assert dst.shape[0] % chunk == 0 and nchunk >= nslot

    def copy(i):
        return pltpu.make_async_copy(src.at[pl.ds(i * chunk, chunk), :], stage.at[i % nslot], sem.at[i % nslot])

    for i in range(nslot - 1):
        copy(i).start()
    for i in range(nchunk):
        if i + nslot - 1 < nchunk:
            copy(i + nslot - 1).start()
        copy(i).wait()
        for s0, d0, width in col_blocks:
            dst[pl.ds(i * chunk, chunk), d0:d0 + width] = stage[i % nslot, :, s0:s0 + width].astype(BF16)


_GELU_B = -2.0 * math.sqrt(2.0 / math.pi) * LOG2E
_GELU_A = 0.044715 * _GELU_B


def _gelu_denominator(x):
    return 1.0 + jnp.exp2(x * (_GELU_A * (x * x) + _GELU_B))


def _silu_denominator(x):
    return 1.0 + jnp.exp2(x * -LOG2E)


def _gmlp_kernel(x_ref, gpre_ref, gpost_ref, win_hbm, ws_ref, bs_ref, vng_ref, vnb_ref, wout_hbm,
                 o_ref, win_ref, wout_ref, stage_in, stage_out, sem_in, sem_out, vn_scr, ug_scr, y_scr, *, tm):
    @pl.when(pl.program_id(0) == 0)
    def _():
        half = COL_CHUNK // 2
        pairs = [blk for c in range(A_WIDTH // half)
                 for blk in ((c * half, A_WIDTH + c * COL_CHUNK, half),
                             (2 * A_WIDTH + c * half, A_WIDTH + c * COL_CHUNK + half, half))]
        _cast_rows_once(win_hbm.at[0], win_ref, stage_in, sem_in, chunk=stage_in.shape[1],
                        col_blocks=[(A_WIDTH, 0, A_WIDTH)] + pairs)
        _cast_rows_once(wout_hbm.at[0], wout_ref, stage_out, sem_out, chunk=stage_out.shape[1])

    parts = [slice(k * (tm // GMLP_ROW_PARTS), (k + 1) * (tm // GMLP_ROW_PARTS)) for k in range(GMLP_ROW_PARTS)]
    hs = [_rms(x_ref[rows, :], gpre_ref[...]).astype(BF16) for rows in parts]

    def v_path(k):
        v = jnp.dot(hs[k], win_ref[:, :A_WIDTH], preferred_element_type=F32)
        v = v / _gelu_denominator(v)
        mu = jnp.mean(v, axis=-1, keepdims=True)
        vc = v - mu
        vn = vc * lax.rsqrt(jnp.mean(vc * vc, axis=-1, keepdims=True) + EPS)
        vn_scr[parts[k], :] = (vn * vng_ref[...] + vnb_ref[...]).astype(BF16)

    def gate_path(k):
        half = COL_CHUNK // 2
        for cb in range(A_WIDTH // half):
            z = jnp.dot(hs[k], win_ref[:, A_WIDTH + cb * COL_CHUNK:A_WIDTH + (cb + 1) * COL_CHUNK],
                        preferred_element_type=F32)
            u, g = z[:, :half], z[:, half:]
            ug_scr[parts[k], cb * half:(cb + 1) * half] = (u * g) / (_gelu_denominator(u) * _silu_denominator(g))

    def mix(k):
        chunks = range(parts[k].start // A_CHUNK, parts[k].stop // A_CHUNK)
        for pair in range(A_HEADS // 2):
            heads = (2 * pair, 2 * pair + 1)
            rhs = jnp.concatenate(
                [jnp.concatenate([vn_scr[c * A_CHUNK:(c + 1) * A_CHUNK, hd * A_HEAD_DIM:(hd + 1) * A_HEAD_DIM]
                                  for hd in heads], axis=0) for c in chunks], axis=1)
            sg = jnp.dot(ws_ref[pair].astype(BF16), rhs, preferred_element_type=F32)
            for hh, hd in enumerate(heads):
                c0 = hd * A_HEAD_DIM
                for n, c in enumerate(chunks):
                    rows = slice(c * A_CHUNK, (c + 1) * A_CHUNK)
                    sgc = sg[hh * A_CHUNK:(hh + 1) * A_CHUNK, n * A_CHUNK:(n + 1) * A_CHUNK] + bs_ref[hd]
                    y_scr[rows, c0:c0 + A_HEAD_DIM] = (ug_scr[rows, c0:c0 + A_HEAD_DIM] * sgc).astype(BF16)

    def finish(k):
        out = jnp.dot(y_scr[parts[k], :], wout_ref[...], preferred_element_type=F32)
        o_ref[parts[k], :] = x_ref[parts[k], :] + _rms(out, gpost_ref[...])

    for k in range(GMLP_ROW_PARTS):
        v_path(k)
    for k in range(GMLP_ROW_PARTS):
        gate_path(k)
        if k > 0:
            finish(k - 1)
        mix(k)
    finish(GMLP_ROW_PARTS - 1)


def _gmlp_layer(x2, gpre, gpost, w_in, w_s, b_s, vn_g, vn_b, w_out, *, tm=512):
    t = x2.shape[0]
    const = lambda *shape: pl.BlockSpec(shape, lambda i: (0,) * len(shape), pipeline_mode=pl.Buffered(1))
    return pl.pallas_call(
        functools.partial(_gmlp_kernel, tm=tm),
        grid=(t // tm,),
        in_specs=[
            pl.BlockSpec((tm, D_MODEL), lambda i: (i, 0)),
            const(1, D_MODEL), const(1, D_MODEL),
            pl.BlockSpec(memory_space=pl.ANY),
            const(A_HEADS // 2, 2 * A_CHUNK, 2 * A_CHUNK),
            const(A_HEADS, A_CHUNK, A_HEAD_DIM),
            const(1, A_WIDTH), const(1, A_WIDTH),
            pl.BlockSpec(memory_space=pl.ANY),
        ],
        out_specs=pl.BlockSpec((tm, D_MODEL), lambda i: (i, 0)),
        out_shape=jax.ShapeDtypeStruct((t, D_MODEL), F32),
        scratch_shapes=[pltpu.VMEM((D_MODEL, 3 * A_WIDTH), BF16), pltpu.VMEM((A_WIDTH, D_MODEL), BF16),
                        pltpu.VMEM((CAST_SLOTS, CAST_ROWS_WIDE, 3 * A_WIDTH), F32),
                        pltpu.VMEM((CAST_SLOTS, CAST_ROWS, D_MODEL), F32),
                        pltpu.SemaphoreType.DMA((CAST_SLOTS,)), pltpu.SemaphoreType.DMA((CAST_SLOTS,)),
                        pltpu.VMEM((tm, A_WIDTH), BF16), pltpu.VMEM((tm, A_WIDTH), F32),
                        pltpu.VMEM((tm, A_WIDTH), BF16)],
        compiler_params=pltpu.CompilerParams(dimension_semantics=("arbitrary",),
                                             vmem_limit_bytes=VMEM_LIMIT),
        name="gmlp_layer",
    )(x2, gpre, gpost, w_in, w_s, b_s, vn_g, vn_b, w_out)


def _qkv_kernel(x_ref, gpre_ref, w_hbm, *refs, tm, dils):
    out_refs = refs[:len(dils)]
    h_scr, w_ref, stage, sem = refs[len(dils):]
    nslab = D_MODEL // LANES

    @pl.when((pl.program_id(0) == 0) & (pl.program_id(1) == 0))
    def _():
        _cast_rows_once(w_hbm.at[0, :, pl.ds(0, B_QKV)], w_ref, stage, sem, chunk=stage.shape[1])

    h = _rms(x_ref[...], gpre_ref[...])
    for k in range(nslab):
        h_scr[k] = h[:, k * LANES:(k + 1) * LANES]
    nc = 512

    def gathered(starts, rows, stride):
        return jnp.concatenate(
            [jnp.concatenate([h_scr[k, pl.ds(st, rows, stride=stride), :] for st in starts], axis=0)
             for k in range(nslab)], axis=1).astype(BF16)

    for gi, d in enumerate(dils):
        rows = tm // d
        hp = gathered(range(d), rows, d)
        if d == 1:
            per = Q_BLOCK // STATE_INTERLEAVE
            hq = gathered([blk * Q_BLOCK + c for blk in range(tm // Q_BLOCK) for c in range(STATE_INTERLEAVE)],
                          per, STATE_INTERLEAVE)
        else:
            hq = hp
        for c in range(B_GROUP_COLS // nc):
            c0 = gi * B_GROUP_COLS + c * nc
            is_q = (c + 1) * nc <= B_WIDTH
            z = jnp.dot(hq if is_q else hp, w_ref[:, c0:c0 + nc], preferred_element_type=F32)
            if is_q:
                z = z * LOGIT_SCALE
            out_refs[gi][:, :, c * nc:(c + 1) * nc] = z.reshape(d, rows, nc).astype(BF16)


def _qkv_projection(x3, gpre, w_qkv, *, tm=512):
    bsz, s, _ = x3.shape
    dils = tuple(d for _, d in B_GROUPS)
    return pl.pallas_call(
        functools.partial(_qkv_kernel, tm=tm, dils=dils),
        grid=(bsz, s // tm),
        in_specs=[
            pl.BlockSpec((None, tm, D_MODEL), lambda b, i: (b, i, 0)),
            pl.BlockSpec((1, D_MODEL), lambda b, i: (0, 0)),
            pl.BlockSpec(memory_space=pl.ANY),
        ],
        out_specs=[pl.BlockSpec((None, d, tm // d, B_GROUP_COLS), lambda b, i: (b, 0, i, 0)) for d in dils],
        out_shape=[jax.ShapeDtypeStruct((bsz, d, s // d, B_GROUP_COLS), BF16) for d in dils],
        scratch_shapes=[pltpu.VMEM((D_MODEL // LANES, tm, LANES), F32),
                        pltpu.VMEM((D_MODEL, B_QKV), BF16),
                        pltpu.VMEM((CAST_SLOTS, CAST_ROWS_WIDE, B_QKV), F32),
                        pltpu.SemaphoreType.DMA((CAST_SLOTS,))],
        compiler_params=pltpu.CompilerParams(dimension_semantics=("arbitrary", "arbitrary"),
                                             vmem_limit_bytes=VMEM_LIMIT),
        name="qkv_projection",
    )(x3, gpre, w_qkv)


def _t5_bucket_np(rel):
    half = REL_BUCKETS // 2
    ret = np.where(rel > 0, half, 0)
    n = np.abs(rel)
    nf = np.maximum(n, 1).astype(np.float32)
    large = REL_EXACT + (np.log(nf / np.float32(REL_EXACT)) / np.float32(math.log(REL_MAX_DISTANCE / REL_EXACT))
                         * np.float32(half - REL_EXACT)).astype(np.int32)
    large = np.minimum(large, half - 1)
    return (ret + np.where(n < REL_EXACT, n, large)).astype(np.int32)


def _bucket_row(dilation, half_w):
    rel = np.arange(K_WINDOW, dtype=np.int32) - half_w
    row = np.where(np.abs(rel) <= half_w, _t5_bucket_np(rel * dilation), -1).astype(np.int32)
    return np.broadcast_to(row, (8, K_WINDOW))


def _attn_kernel(tbl_ref, bkt_ref, *refs, seq, groups, unroll):
    ng = len(groups)
    z_refs, o_ref = refs[:ng], refs[ng]
    buf, sem, acc_scr, m_scr, den_scr, bm_scr, p_scr, pm_scr = refs[ng + 1:]
    per = Q_BLOCK // STATE_INTERLEAVE
    hp, b = pl.program_id(0), pl.program_id(1)
    nb = pl.num_programs(1)
    step = hp * nb + b
    nsteps = pl.num_programs(0) * nb

    def slab_copies(g, hp_, b_):
        return [pltpu.make_async_copy(
            z_refs[g].at[b_, :, pl.ds(pl.multiple_of((j * HEAD_PAIRS + hp_) * LANES, LANES), LANES)],
            buf.at[g, j], sem.at[g, j]) for j in range(3)]

    @pl.when(step == 0)
    def _():
        for cp in slab_copies(0, hp, b):
            cp.start()

    @pl.when(b == 0)
    def _():
        key = lax.broadcasted_iota(jnp.int32, (Q_BLOCK, K_WINDOW), 1)
        for g, (d, half_w, col) in enumerate(groups):
            for hh in range(2):
                def fill(bk, row, g=g, hh=hh, col=col):
                    return jnp.where(bkt_ref[g] == bk, tbl_ref[bk, col + 2 * hp + hh], row)
                row = lax.fori_loop(0, REL_BUCKETS, fill, jnp.full((8, K_WINDOW), NEG_INF, F32)) * LOG2E
                if d == 1:
                    mid = jnp.concatenate(
                        [pltpu.roll(jnp.broadcast_to(row[0:1], (per, K_WINDOW)), c, 1,
                                    stride=STATE_INTERLEAVE, stride_axis=0) for c in range(STATE_INTERLEAVE)], axis=0)
                else:
                    mid = pltpu.roll(jnp.broadcast_to(row[0:1], (Q_BLOCK, K_WINDOW)), 0, 1, stride=1, stride_axis=0)
                bm_scr[g, hh, 1] = mid
                bm_scr[g, hh, 0] = jnp.where(key < K_WINDOW - half_w,
                                             pltpu.roll(mid, K_WINDOW - half_w, 1), NEG_INF)
                bm_scr[g, hh, 2] = jnp.where(key >= half_w, pltpu.roll(mid, half_w, 1), NEG_INF)

    first_head = lax.broadcasted_iota(jnp.int32, (Q_BLOCK, LANES), 1) < B_HEAD_DIM
    ones = jnp.ones((K_WINDOW, LANES), BF16)

    def run_group(g):
        d, half_w, _ = groups[g]
        length = seq // d
        nblk = length // Q_BLOCK
        ntrips = d * nblk // unroll
        q_ref, k_ref, v_ref = buf.at[g, 0], buf.at[g, 1], buf.at[g, 2]

        def blocks(it):
            for u in range(unroll):
                if unroll >= nblk:
                    r, i = it * (unroll // nblk) + u // nblk, u % nblk
                else:
                    chunks = nblk // unroll
                    r = it // chunks if d > 1 else 0
                    i = (it - r * chunks) * unroll + u
                k0 = pl.multiple_of(r * length + jnp.clip(i * Q_BLOCK - half_w, 0, length - K_WINDOW), half_w)
                yield u, r, i, k0

        def logits_stage(it, slot):
            for u, r, i, k0 in blocks(it):
                q0 = pl.multiple_of(r * length + i * Q_BLOCK, Q_BLOCK)
                var = jnp.where(i == 0, 0, jnp.where(i == nblk - 1, 2, 1))
                qb = q_ref[pl.ds(q0, Q_BLOCK), :]
                kw = k_ref[pl.ds(k0, K_WINDOW), :]
                zero = jnp.zeros_like(qb)
                q2 = jnp.concatenate([jnp.where(first_head, qb, zero), jnp.where(first_head, zero, qb)], axis=0)
                s = lax.dot_general(q2, kw, (((1,), (1,)), ((), ())), preferred_element_type=F32)
                s = s + jnp.concatenate([bm_scr[g, 0, var], bm_scr[g, 1, var]], axis=0)
                m2 = jnp.max(s, axis=-1, keepdims=True)
                p_scr[slot, u] = jnp.exp2(s - m2).astype(BF16)
                pm_scr[slot, u] = jnp.where(first_head, m2[:Q_BLOCK], m2[Q_BLOCK:])

        def values_stage(it, slot):
            for u, r, i, k0 in blocks(it):
                vw = v_ref[pl.ds(k0, K_WINDOW), :]
                ov = jnp.dot(p_scr[slot, u], jnp.concatenate([vw, ones], axis=1), preferred_element_type=F32)
                acc = jnp.where(first_head, ov[:Q_BLOCK, :LANES], ov[Q_BLOCK:, :LANES])
                den = jnp.where(first_head, ov[:Q_BLOCK, LANES:], ov[Q_BLOCK:, LANES:])
                m = pm_scr[slot, u]
                if d == 1:
                    pieces = [(0, Q_BLOCK, pl.ds(i * Q_BLOCK, Q_BLOCK))]
                else:
                    n = Q_BLOCK // d
                    base = r // STATE_INTERLEAVE + per * (r % STATE_INTERLEAVE)
                    pieces = [(k * n, n, pl.ds((i * d + k) * Q_BLOCK + base, n, stride=d // STATE_INTERLEAVE))
                              for k in range(d)]
                gather = lambda ref: jnp.concatenate([ref[idx, :] for _, _, idx in pieces], axis=0)
                if g > 0:
                    m_old = gather(m_scr)
                    m_new = jnp.maximum(m_old, m)
                    w_old, w_cur = jnp.exp2(m_old - m_new), jnp.exp2(m - m_new)
                    acc = w_old * gather(acc_scr) + w_cur * acc
                    den = w_old * gather(den_scr) + w_cur * den
                    m = m_new
                if g == ng - 1:
                    o = acc / den
                    for c in range(STATE_INTERLEAVE):
                        o_ref[pl.ds(i * Q_BLOCK + c, per, stride=STATE_INTERLEAVE), :] = o[c * per:(c + 1) * per]
                else:
                    for ref, val in ((acc_scr, acc), (m_scr, m), (den_scr, den)):
                        for r0, n, idx in pieces:
                            ref[idx, :] = val[r0:r0 + n]

        logits_stage(0, 0)

        def pair(j, carry):
            logits_stage(2 * j + 1, 1)
            values_stage(2 * j, 0)
            logits_stage(2 * j + 2, 0)
            values_stage(2 * j + 1, 1)
            return carry

        lax.fori_loop(0, ntrips // 2 - 1, pair, 0)
        logits_stage(ntrips - 1, 1)
        values_stage(ntrips - 2, 0)
        values_stage(ntrips - 1, 1)

    for g in range(ng):
        if g + 1 < ng:
            for cp in slab_copies(g + 1, hp, b):
                cp.start()
        else:
            @pl.when(step + 1 < nsteps)
            def _():
                nxt = step + 1
                for cp in slab_copies(0, nxt // nb, nxt % nb):
                    cp.start()
        for cp in slab_copies(g, hp, b):
            cp.wait()
        run_group(g)


def _window_attention(zs, rel_bias):
    bsz, s, _ = zs[0].shape
    order = sorted(range(len(B_GROUPS)), key=lambda gi: -B_GROUPS[gi][1])
    groups = tuple((B_GROUPS[gi][1], B_GROUPS[gi][0] // (2 * B_GROUPS[gi][1]), gi * B_HEADS) for gi in order)
    zs = [zs[gi] for gi in order]
    assert groups[-1][0] == 1
    for d, half_w, _ in groups:
        nblk = s // d // Q_BLOCK
        assert K_WINDOW == Q_BLOCK + 2 * half_w and s // d >= K_WINDOW and s % (d * Q_BLOCK) == 0
        assert (d * nblk) % (2 * ATTN_UNROLL) == 0 and (ATTN_UNROLL % nblk == 0 or nblk % ATTN_UNROLL == 0)
        assert d == 1 or (d % STATE_INTERLEAVE == 0 and Q_BLOCK % d == 0)
    bkt = jnp.asarray(np.stack([_bucket_row(d, half_w) for d, half_w, _ in groups]))
    return pl.pallas_call(
        functools.partial(_attn_kernel, seq=s, groups=groups, unroll=ATTN_UNROLL),
        grid=(HEAD_PAIRS, bsz),
        in_specs=[pl.BlockSpec(memory_space=pltpu.SMEM),
                  pl.BlockSpec(bkt.shape, lambda hp, b: (0, 0, 0))]
                 + [pl.BlockSpec(memory_space=pl.ANY)] * len(groups),
        out_specs=pl.BlockSpec((None, s, LANES), lambda hp, b: (b, 0, hp)),
        out_shape=jax.ShapeDtypeStruct((bsz, s, B_WIDTH), F32),
        scratch_shapes=[pltpu.VMEM((len(groups), 3, s, LANES), BF16),
                        pltpu.SemaphoreType.DMA((len(groups), 3)),
                        pltpu.VMEM((s, LANES), F32),
                        pltpu.VMEM((s, LANES), F32),
                        pltpu.VMEM((s, LANES), F32),
                        pltpu.VMEM((len(groups), 2, 3, Q_BLOCK, K_WINDOW), F32),
                        pltpu.VMEM((2, ATTN_UNROLL, 2 * Q_BLOCK, K_WINDOW), BF16),
                        pltpu.VMEM((2, ATTN_UNROLL, Q_BLOCK, LANES), F32)],
        compiler_params=pltpu.CompilerParams(dimension_semantics=("arbitrary", "arbitrary"),
                                             vmem_limit_bytes=VMEM_LIMIT),
        name="window_attention",
    )(rel_bias.astype(F32), bkt, *zs)


def _out_kernel(x_ref, o_ref, gpre_ref, gpost_ref, win_hbm, wout_hbm, out_ref, wg_ref, wout_ref, stage, sem):
    @pl.when(pl.program_id(0) == 0)
    def _():
        _cast_rows_once(win_hbm.at[0, :, pl.ds(B_QKV, B_WIDTH)], wg_ref, stage, sem, chunk=stage.shape[1])
        _cast_rows_once(wout_hbm.at[0], wout_ref, stage, sem, chunk=stage.shape[1])

    tm = x_ref.shape[0]
    parts = [slice(k * (tm // GATE_ROW_PARTS), (k + 1) * (tm // GATE_ROW_PARTS)) for k in range(GATE_ROW_PARTS)]
    ys = []
    for rows in parts:
        h = _rms(x_ref[rows, :], gpre_ref[...]).astype(BF16)
        gate = jnp.dot(h, wg_ref[...], preferred_element_type=F32)
        ys.append((o_ref[rows, :] * gate / _silu_denominator(gate)).astype(BF16))
    for rows, y in zip(parts, ys):
        out = jnp.dot(y, wout_ref[...], preferred_element_type=F32)
        out_ref[rows, :] = x_ref[rows, :] + _rms(out, gpost_ref[...])


def _gate_output(x2, o2, gpre, gpost, w_in, w_out, *, tm=512):
    t = x2.shape[0]
    const = lambda *shape: pl.BlockSpec(shape, lambda i: (0,) * len(shape), pipeline_mode=pl.Buffered(1))
    row = pl.BlockSpec((tm, D_MODEL), lambda i: (i, 0))
    return pl.pallas_call(
        _out_kernel,
        grid=(t // tm,),
        in_specs=[row, row, const(1, D_MODEL), const(1, D_MODEL),
                  pl.BlockSpec(memory_space=pl.ANY), pl.BlockSpec(memory_space=pl.ANY)],
        out_specs=row,
        out_shape=jax.ShapeDtypeStruct((t, D_MODEL), F32),
        scratch_shapes=[pltpu.VMEM((D_MODEL, B_WIDTH), BF16), pltpu.VMEM((B_WIDTH, D_MODEL), BF16),
                        pltpu.VMEM((CAST_SLOTS, CAST_ROWS, D_MODEL), F32),
                        pltpu.SemaphoreType.DMA((CAST_SLOTS,))],
        compiler_params=pltpu.CompilerParams(dimension_semantics=("arbitrary",),
                                             vmem_limit_bytes=VMEM_LIMIT),
        name="gate_output",
    )(x2, o2, gpre, gpost, w_in, w_out)


def kernel(x, norm_pre, norm_post, a_w_in, a_w_s, a_b_s, a_vnorm_g, a_vnorm_b, a_w_out, b_w_in, b_w_out, rel_bias):
    bsz, s, dm = x.shape
    assert dm == D_MODEL and norm_pre.shape[0] == 2 and a_w_in.shape[0] == 1 and b_w_in.shape[0] == 1

    b_s = jnp.broadcast_to(a_b_s[0][:, :, None], (A_HEADS, A_CHUNK, A_HEAD_DIM))
    ws = a_w_s[0].reshape(A_HEADS // 2, 2, A_CHUNK, A_CHUNK)
    nil = jnp.zeros_like(ws[:, 0])
    ws_pairs = jnp.concatenate([jnp.concatenate([ws[:, 0], nil], axis=2),
                                jnp.concatenate([nil, ws[:, 1]], axis=2)], axis=1)
    x2 = _gmlp_layer(x.reshape(bsz * s, dm), norm_pre[0:1], norm_post[0:1],
                     a_w_in, ws_pairs, b_s, a_vnorm_g, a_vnorm_b, a_w_out)

    zs = _qkv_projection(x2.reshape(bsz, s, dm), norm_pre[1:2], b_w_in)
    o = _window_attention([z.reshape(bsz, s, B_GROUP_COLS) for z in zs], rel_bias)
    out = _gate_output(x2, o.reshape(bsz * s, B_WIDTH), norm_pre[1:2], norm_post[1:2], b_w_in, b_w_out)
    return out.reshape(bsz, s, dm)
```

```python
import functools
import math

import jax
import jax.numpy as jnp
import numpy as np
from jax import lax
from jax.experimental import pallas as pl
from jax.experimental.pallas import tpu as pltpu

F32 = jnp.float32
BF16 = jnp.bfloat16

EPS = 1e-6
NEG_INF = -1e30

D_MODEL = 1024
A_WIDTH = 2048
A_CHUNK = 128
A_HEADS = 16
A_HEAD_DIM = 128

B_GROUPS = ((128, 1), (512, 4), (2048, 16))
B_HEADS = 16
B_HEAD_DIM = 64
B_WIDTH = B_HEADS * B_HEAD_DIM
B_GROUP_COLS = 3 * B_WIDTH
B_QKV = len(B_GROUPS) * B_GROUP_COLS
REL_BUCKETS = 32
REL_EXACT = 8
REL_MAX_DISTANCE = 1024

LOG2E = math.log2(math.e)
LOGIT_SCALE = B_HEAD_DIM ** -0.5 * LOG2E

LANES = 128
GMLP_TILE = 512
QKV_TILE = 512
GATE_TILE = 1024
COL_CHUNK = 512
GMLP_ROW_PARTS = 1
GATE_ROW_PARTS = 2
CAST_SLOTS = 4
CAST_ROWS = 256
CAST_ROWS_WIDE = 32
HEAD_PAIRS = B_WIDTH // LANES
Q_BLOCK = 128
K_WINDOW = 256
STATE_INTERLEAVE = 4
ATTN_UNROLL = 16
VMEM_LIMIT = 58 * 1024 * 1024


def _rms(x, g):
    return x * lax.rsqrt(jnp.mean(x * x, axis=-1, keepdims=True) + EPS) * g


def _cast_rows_once(src, dst, stage, sem, *, chunk, col_blocks=None):
    col_blocks = col_blocks or [(0, 0, dst.shape[1])]
    nchunk, nslot = dst.shape[0] // chunk, stage.shape[0]
    assert dst.shape[0] % chunk == 0 and nchunk >= nslot

    def copy(i):
        return pltpu.make_async_copy(src.at[pl.ds(i * chunk, chunk), :], stage.at[i % nslot], sem.at[i % nslot])

    for i in range(nslot - 1):
        copy(i).start()
    for i in range(nchunk):
        if i + nslot - 1 < nchunk:
            copy(i + nslot - 1).start()
        copy(i).wait()
        for s0, d0, width in col_blocks:
            dst[pl.ds(i * chunk, chunk), d0:d0 + width] = stage[i % nslot, :, s0:s0 + width].astype(BF16)


_GELU_B = -2.0 * math.sqrt(2.0 / math.pi) * LOG2E
_GELU_A = 0.044715 * _GELU_B


def _gelu_denominator(x):
    return 1.0 + jnp.exp2(x * (_GELU_A * (x * x) + _GELU_B))


def _silu_denominator(x):
    return 1.0 + jnp.exp2(x * -LOG2E)


def _gmlp_kernel(x_ref, gpre_ref, gpost_ref, win_hbm, ws_ref, bs_ref, vng_ref, vnb_ref, wout_hbm,
                 o_ref, win_ref, wout_ref, stage_in, stage_out, sem_in, sem_out, vn_scr, ug_scr, y_scr, *, tm):
    @pl.when(pl.program_id(0) == 0)
    def _():
        half = COL_CHUNK // 2
        pairs = [blk for c in range(A_WIDTH // half)
                 for blk in ((c * half, A_WIDTH + c * COL_CHUNK, half),
                             (2 * A_WIDTH + c * half, A_WIDTH + c * COL_CHUNK + half, half))]
        _cast_rows_once(win_hbm.at[0], win_ref, stage_in, sem_in, chunk=stage_in.shape[1],
                        col_blocks=[(A_WIDTH, 0, A_WIDTH)] + pairs)
        _cast_rows_once(wout_hbm.at[0], wout_ref, stage_out, sem_out, chunk=stage_out.shape[1])

    parts = [slice(k * (tm // GMLP_ROW_PARTS), (k + 1) * (tm // GMLP_ROW_PARTS)) for k in range(GMLP_ROW_PARTS)]
    hs = [_rms(x_ref[rows, :], gpre_ref[...]).astype(BF16) for rows in parts]

    def v_path(k):
        v = jnp.dot(hs[k], win_ref[:, :A_WIDTH], preferred_element_type=F32)
        v = v / _gelu_denominator(v)
        mu = jnp.mean(v, axis=-1, keepdims=True)
        vc = v - mu
        vn = vc * lax.rsqrt(jnp.mean(vc * vc, axis=-1, keepdims=True) + EPS)
        vn_scr[parts[k], :] = (vn * vng_ref[...] + vnb_ref[...]).astype(BF16)

    def gate_path(k):
        half = COL_CHUNK // 2
        for cb in range(A_WIDTH // half):
            z = jnp.dot(hs[k], win_ref[:, A_WIDTH + cb * COL_CHUNK:A_WIDTH + (cb + 1) * COL_CHUNK],
                        preferred_element_type=F32)
            u, g = z[:, :half], z[:, half:]
            ug_scr[parts[k], cb * half:(cb + 1) * half] = (u * g) / (_gelu_denominator(u) * _silu_denominator(g))

    def mix(k):
        chunks = range(parts[k].start // A_CHUNK, parts[k].stop // A_CHUNK)
        for hd in range(A_HEADS):
            c0 = hd * A_HEAD_DIM
            rhs = jnp.concatenate(
                [vn_scr[c * A_CHUNK:(c + 1) * A_CHUNK, c0:c0 + A_HEAD_DIM] for c in chunks], axis=1)
            sg = jnp.dot(ws_ref[hd].astype(BF16), rhs, preferred_element_type=F32)
            for n, c in enumerate(chunks):
                rows = slice(c * A_CHUNK, (c + 1) * A_CHUNK)
                sgc = sg[:, n * A_CHUNK:(n + 1) * A_CHUNK] + bs_ref[hd]
                y_scr[rows, c0:c0 + A_HEAD_DIM] = (ug_scr[rows, c0:c0 + A_HEAD_DIM] * sgc).astype(BF16)

    def finish(k):
        out = jnp.dot(y_scr[parts[k], :], wout_ref[...], preferred_element_type=F32)
        o_ref[parts[k], :] = x_ref[parts[k], :] + _rms(out, gpost_ref[...])

    for k in range(GMLP_ROW_PARTS):
        v_path(k)
    for k in range(GMLP_ROW_PARTS):
        gate_path(k)
        if k > 0:
            finish(k - 1)
        mix(k)
    finish(GMLP_ROW_PARTS - 1)


def _gmlp_layer(x2, gpre, gpost, w_in, w_s, b_s, vn_g, vn_b, w_out, *, tm=GMLP_TILE):
    t = x2.shape[0]
    const = lambda *shape: pl.BlockSpec(shape, lambda i: (0,) * len(shape), pipeline_mode=pl.Buffered(1))
    return pl.pallas_call(
        functools.partial(_gmlp_kernel, tm=tm),
        grid=(t // tm,),
        in_specs=[
            pl.BlockSpec((tm, D_MODEL), lambda i: (i, 0)),
            const(1, D_MODEL), const(1, D_MODEL),
            pl.BlockSpec(memory_space=pl.ANY),
            const(A_HEADS, A_CHUNK, A_CHUNK),
            const(A_HEADS, A_CHUNK, A_HEAD_DIM),
            const(1, A_WIDTH), const(1, A_WIDTH),
            pl.BlockSpec(memory_space=pl.ANY),
        ],
        out_specs=pl.BlockSpec((tm, D_MODEL), lambda i: (i, 0)),
        out_shape=jax.ShapeDtypeStruct((t, D_MODEL), F32),
        scratch_shapes=[pltpu.VMEM((D_MODEL, 3 * A_WIDTH), BF16), pltpu.VMEM((A_WIDTH, D_MODEL), BF16),
                        pltpu.VMEM((CAST_SLOTS, CAST_ROWS_WIDE, 3 * A_WIDTH), F32),
                        pltpu.VMEM((CAST_SLOTS, CAST_ROWS, D_MODEL), F32),
                        pltpu.SemaphoreType.DMA((CAST_SLOTS,)), pltpu.SemaphoreType.DMA((CAST_SLOTS,)),
                        pltpu.VMEM((tm, A_WIDTH), BF16), pltpu.VMEM((tm, A_WIDTH), F32),
                        pltpu.VMEM((tm, A_WIDTH), BF16)],
        compiler_params=pltpu.CompilerParams(dimension_semantics=("arbitrary",),
                                             vmem_limit_bytes=VMEM_LIMIT),
        name="gmlp_layer",
    )(x2, gpre, gpost, w_in, w_s, b_s, vn_g, vn_b, w_out)


def _qkv_kernel(x_ref, gpre_ref, w_hbm, *refs, tm, dils):
    out_refs = refs[:len(dils)]
    h_scr, w_ref, stage, sem = refs[len(dils):]
    nslab = D_MODEL // LANES

    @pl.when((pl.program_id(0) == 0) & (pl.program_id(1) == 0))
    def _():
        _cast_rows_once(w_hbm.at[0, :, pl.ds(0, B_QKV)], w_ref, stage, sem, chunk=stage.shape[1])

    h = _rms(x_ref[...], gpre_ref[...])
    for k in range(nslab):
        h_scr[k] = h[:, k * LANES:(k + 1) * LANES]
    nc = COL_CHUNK

    def gathered(starts, rows, stride):
        return jnp.concatenate(
            [jnp.concatenate([h_scr[k, pl.ds(st, rows, stride=stride), :] for st in starts], axis=0)
             for k in range(nslab)], axis=1).astype(BF16)

    for gi, d in enumerate(dils):
        rows = tm // d
        hp = gathered(range(d), rows, d)
        if d == 1:
            per = Q_BLOCK // STATE_INTERLEAVE
            hq = gathered([blk * Q_BLOCK + c for blk in range(tm // Q_BLOCK) for c in range(STATE_INTERLEAVE)],
                          per, STATE_INTERLEAVE)
        else:
            hq = hp
        for c in range(B_GROUP_COLS // nc):
            c0 = gi * B_GROUP_COLS + c * nc
            is_q = (c + 1) * nc <= B_WIDTH
            z = jnp.dot(hq if is_q else hp, w_ref[:, c0:c0 + nc], preferred_element_type=F32)
            if is_q:
                z = z * LOGIT_SCALE
            out_refs[gi][:, :, c * nc:(c + 1) * nc] = z.reshape(d, rows, nc).astype(BF16)


def _qkv_projection(x3, gpre, w_qkv, *, tm=QKV_TILE):
    bsz, s, _ = x3.shape
    dils = tuple(d for _, d in B_GROUPS)
    return pl.pallas_call(
        functools.partial(_qkv_kernel, tm=tm, dils=dils),
        grid=(bsz, s // tm),
        in_specs=[
            pl.BlockSpec((None, tm, D_MODEL), lambda b, i: (b, i, 0)),
            pl.BlockSpec((1, D_MODEL), lambda b, i: (0, 0)),
            pl.BlockSpec(memory_space=pl.ANY),
        ],
        out_specs=[pl.BlockSpec((None, d, tm // d, B_GROUP_COLS), lambda b, i: (b, 0, i, 0)) for d in dils],
        out_shape=[jax.ShapeDtypeStruct((bsz, d, s // d, B_GROUP_COLS), BF16) for d in dils],
        scratch_shapes=[pltpu.VMEM((D_MODEL // LANES, tm, LANES), F32),
                        pltpu.VMEM((D_MODEL, B_QKV), BF16),
                        pltpu.VMEM((CAST_SLOTS, CAST_ROWS_WIDE, B_QKV), F32),
                        pltpu.SemaphoreType.DMA((CAST_SLOTS,))],
        compiler_params=pltpu.CompilerParams(dimension_semantics=("arbitrary", "arbitrary"),
                                             vmem_limit_bytes=VMEM_LIMIT),
        name="qkv_projection",
    )(x3, gpre, w_qkv)


def _t5_bucket_np(rel):
    half = REL_BUCKETS // 2
    ret = np.where(rel > 0, half, 0)
    n = np.abs(rel)
    nf = np.maximum(n, 1).astype(np.float32)
    large = REL_EXACT + (np.log(nf / np.float32(REL_EXACT)) / np.float32(math.log(REL_MAX_DISTANCE / REL_EXACT))
                         * np.float32(half - REL_EXACT)).astype(np.int32)
    large = np.minimum(large, half - 1)
    return (ret + np.where(n < REL_EXACT, n, large)).astype(np.int32)


def _bucket_row(dilation, half_w):
    rel = np.arange(K_WINDOW, dtype=np.int32) - half_w
    row = np.where(np.abs(rel) <= half_w, _t5_bucket_np(rel * dilation), -1).astype(np.int32)
    return np.broadcast_to(row, (8, K_WINDOW))


def _attn_kernel(tbl_ref, bkt_ref, *refs, seq, groups, unroll):
    ng = len(groups)
    z_refs, o_ref = refs[:ng], refs[ng]
    buf, sem, acc_scr, m_scr, den_scr, bm_scr, p_scr, pm_scr = refs[ng + 1:]
    per = Q_BLOCK // STATE_INTERLEAVE
    hp, b = pl.program_id(0), pl.program_id(1)
    nb = pl.num_programs(1)
    step = hp * nb + b
    nsteps = pl.num_programs(0) * nb

    def slab_copies(g, hp_, b_):
        return [pltpu.make_async_copy(
            z_refs[g].at[b_, :, pl.ds(pl.multiple_of((j * HEAD_PAIRS + hp_) * LANES, LANES), LANES)],
            buf.at[g, j], sem.at[g, j]) for j in range(3)]

    @pl.when(step == 0)
    def _():
        for cp in slab_copies(0, hp, b):
            cp.start()

    @pl.when(b == 0)
    def _():
        key = lax.broadcasted_iota(jnp.int32, (Q_BLOCK, K_WINDOW), 1)
        for g, (d, half_w, col) in enumerate(groups):
            for hh in range(2):
                def fill(bk, row, g=g, hh=hh, col=col):
                    return jnp.where(bkt_ref[g] == bk, tbl_ref[bk, col + 2 * hp + hh], row)
                row = lax.fori_loop(0, REL_BUCKETS, fill, jnp.full((8, K_WINDOW), NEG_INF, F32)) * LOG2E
                if d == 1:
                    mid = jnp.concatenate(
                        [pltpu.roll(jnp.broadcast_to(row[0:1], (per, K_WINDOW)), c, 1,
                                    stride=STATE_INTERLEAVE, stride_axis=0) for c in range(STATE_INTERLEAVE)], axis=0)
                else:
                    mid = pltpu.roll(jnp.broadcast_to(row[0:1], (Q_BLOCK, K_WINDOW)), 0, 1, stride=1, stride_axis=0)
                bm_scr[g, hh, 1] = mid
                bm_scr[g, hh, 0] = jnp.where(key < K_WINDOW - half_w,
                                             pltpu.roll(mid, K_WINDOW - half_w, 1), NEG_INF)
                bm_scr[g, hh, 2] = jnp.where(key >= half_w, pltpu.roll(mid, half_w, 1), NEG_INF)

    first_head = lax.broadcasted_iota(jnp.int32, (Q_BLOCK, LANES), 1) < B_HEAD_DIM
    ones = jnp.ones((K_WINDOW, LANES), BF16)

    def run_group(g):
        d, half_w, _ = groups[g]
        length = seq // d
        nblk = length // Q_BLOCK
        ntrips = d * nblk // unroll
        q_ref, k_ref, v_ref = buf.at[g, 0], buf.at[g, 1], buf.at[g, 2]

        def blocks(it):
            for u in range(unroll):
                if unroll >= nblk:
                    r, i = it * (unroll // nblk) + u // nblk, u % nblk
                else:
                    chunks = nblk // unroll
                    r = it // chunks if d > 1 else 0
                    i = (it - r * chunks) * unroll + u
                k0 = pl.multiple_of(r * length + jnp.clip(i * Q_BLOCK - half_w, 0, length - K_WINDOW), half_w)
                yield u, r, i, k0

        def logits_stage(it, slot):
            for u, r, i, k0 in blocks(it):
                q0 = pl.multiple_of(r * length + i * Q_BLOCK, Q_BLOCK)
                var = jnp.where(i == 0, 0, jnp.where(i == nblk - 1, 2, 1))
                qb = q_ref[pl.ds(q0, Q_BLOCK), :]
                kw = k_ref[pl.ds(k0, K_WINDOW), :]
                zero = jnp.zeros_like(qb)
                q2 = jnp.concatenate([jnp.where(first_head, qb, zero), jnp.where(first_head, zero, qb)], axis=0)
                s = lax.dot_general(q2, kw, (((1,), (1,)), ((), ())), preferred_element_type=F32)
                s = s + jnp.concatenate([bm_scr[g, 0, var], bm_scr[g, 1, var]], axis=0)
                m2 = jnp.max(s, axis=-1, keepdims=True)
                p_scr[slot, u] = jnp.exp2(s - m2).astype(BF16)
                pm_scr[slot, u] = jnp.where(first_head, m2[:Q_BLOCK], m2[Q_BLOCK:])

        def values_stage(it, slot):
            for u, r, i, k0 in blocks(it):
                vw = v_ref[pl.ds(k0, K_WINDOW), :]
                ov = jnp.dot(p_scr[slot, u], jnp.concatenate([vw, ones], axis=1), preferred_element_type=F32)
                acc = jnp.where(first_head, ov[:Q_BLOCK, :LANES], ov[Q_BLOCK:, :LANES])
                den = jnp.where(first_head, ov[:Q_BLOCK, LANES:], ov[Q_BLOCK:, LANES:])
                m = pm_scr[slot, u]
                if d == 1:
                    pieces = [(0, Q_BLOCK, pl.ds(i * Q_BLOCK, Q_BLOCK))]
                else:
                    n = Q_BLOCK // d
                    base = r // STATE_INTERLEAVE + per * (r % STATE_INTERLEAVE)
                    pieces = [(k * n, n, pl.ds((i * d + k) * Q_BLOCK + base, n, stride=d // STATE_INTERLEAVE))
                              for k in range(d)]
                gather = lambda ref: jnp.concatenate([ref[idx, :] for _, _, idx in pieces], axis=0)
                if g > 0:
                    m_old = gather(m_scr)
                    m_new = jnp.maximum(m_old, m)
                    w_old, w_cur = jnp.exp2(m_old - m_new), jnp.exp2(m - m_new)
                    acc = w_old * gather(acc_scr) + w_cur * acc
                    den = w_old * gather(den_scr) + w_cur * den
                    m = m_new
                if g == ng - 1:
                    o = acc / den
                    for c in range(STATE_INTERLEAVE):
                        o_ref[pl.ds(i * Q_BLOCK + c, per, stride=STATE_INTERLEAVE), :] = o[c * per:(c + 1) * per]
                else:
                    for ref, val in ((acc_scr, acc), (m_scr, m), (den_scr, den)):
                        for r0, n, idx in pieces:
                            ref[idx, :] = val[r0:r0 + n]

        logits_stage(0, 0)

        def pair(j, carry):
            logits_stage(2 * j + 1, 1)
            values_stage(2 * j, 0)
            logits_stage(2 * j + 2, 0)
            values_stage(2 * j + 1, 1)
            return carry

        lax.fori_loop(0, ntrips // 2 - 1, pair, 0)
        logits_stage(ntrips - 1, 1)
        values_stage(ntrips - 2, 0)
        values_stage(ntrips - 1, 1)

    for g in range(ng):
        if g + 1 < ng:
            for cp in slab_copies(g + 1, hp, b):
                cp.start()
        else:
            @pl.when(step + 1 < nsteps)
            def _():
                nxt = step + 1
                for cp in slab_copies(0, nxt // nb, nxt % nb):
                    cp.start()
        for cp in slab_copies(g, hp, b):
            cp.wait()
        run_group(g)


def _window_attention(zs, rel_bias):
    bsz, s, _ = zs[0].shape
    order = sorted(range(len(B_GROUPS)), key=lambda gi: -B_GROUPS[gi][1])
    groups = tuple((B_GROUPS[gi][1], B_GROUPS[gi][0] // (2 * B_GROUPS[gi][1]), gi * B_HEADS) for gi in order)
    zs = [zs[gi] for gi in order]
    assert groups[-1][0] == 1
    for d, half_w, _ in groups:
        nblk = s // d // Q_BLOCK
        assert K_WINDOW == Q_BLOCK + 2 * half_w and s // d >= K_WINDOW and s % (d * Q_BLOCK) == 0
        assert (d * nblk) % (2 * ATTN_UNROLL) == 0 and (ATTN_UNROLL % nblk == 0 or nblk % ATTN_UNROLL == 0)
        assert d == 1 or (d % STATE_INTERLEAVE == 0 and Q_BLOCK % d == 0)
    bkt = jnp.asarray(np.stack([_bucket_row(d, half_w) for d, half_w, _ in groups]))
    return pl.pallas_call(
        functools.partial(_attn_kernel, seq=s, groups=groups, unroll=ATTN_UNROLL),
        grid=(HEAD_PAIRS, bsz),
        in_specs=[pl.BlockSpec(memory_space=pltpu.SMEM),
                  pl.BlockSpec(bkt.shape, lambda hp, b: (0, 0, 0))]
                 + [pl.BlockSpec(memory_space=pl.ANY)] * len(groups),
        out_specs=pl.BlockSpec((None, s, LANES), lambda hp, b: (b, 0, hp)),
        out_shape=jax.ShapeDtypeStruct((bsz, s, B_WIDTH), F32),
        scratch_shapes=[pltpu.VMEM((len(groups), 3, s, LANES), BF16),
                        pltpu.SemaphoreType.DMA((len(groups), 3)),
                        pltpu.VMEM((s, LANES), F32),
                        pltpu.VMEM((s, LANES), F32),
                        pltpu.VMEM((s, LANES), F32),
                        pltpu.VMEM((len(groups), 2, 3, Q_BLOCK, K_WINDOW), F32),
                        pltpu.VMEM((2, ATTN_UNROLL, 2 * Q_BLOCK, K_WINDOW), BF16),
                        pltpu.VMEM((2, ATTN_UNROLL, Q_BLOCK, LANES), F32)],
        compiler_params=pltpu.CompilerParams(dimension_semantics=("arbitrary", "arbitrary"),
                                             vmem_limit_bytes=VMEM_LIMIT),
        name="window_attention",
    )(rel_bias.astype(F32), bkt, *zs)


def _out_kernel(x_ref, o_ref, gpre_ref, gpost_ref, win_hbm, wout_hbm, out_ref, wg_ref, wout_ref, stage, sem):
    @pl.when(pl.program_id(0) == 0)
    def _():
        _cast_rows_once(win_hbm.at[0, :, pl.ds(B_QKV, B_WIDTH)], wg_ref, stage, sem, chunk=stage.shape[1])
        _cast_rows_once(wout_hbm.at[0], wout_ref, stage, sem, chunk=stage.shape[1])

    tm = x_ref.shape[0]
    parts = [slice(k * (tm // GATE_ROW_PARTS), (k + 1) * (tm // GATE_ROW_PARTS)) for k in range(GATE_ROW_PARTS)]
    ys = []
    for rows in parts:
        h = _rms(x_ref[rows, :], gpre_ref[...]).astype(BF16)
        gate = jnp.dot(h, wg_ref[...], preferred_element_type=F32)
        ys.append((o_ref[rows, :] * gate / _silu_denominator(gate)).astype(BF16))
    for rows, y in zip(parts, ys):
        out = jnp.dot(y, wout_ref[...], preferred_element_type=F32)
        out_ref[rows, :] = x_ref[rows, :] + _rms(out, gpost_ref[...])


def _gate_output(x2, o2, gpre, gpost, w_in, w_out, *, tm=GATE_TILE):
    t = x2.shape[0]
    const = lambda *shape: pl.BlockSpec(shape, lambda i: (0,) * len(shape), pipeline_mode=pl.Buffered(1))
    row = pl.BlockSpec((tm, D_MODEL), lambda i: (i, 0))
    return pl.pallas_call(
        _out_kernel,
        grid=(t // tm,),
        in_specs=[row, row, const(1, D_MODEL), const(1, D_MODEL),
                  pl.BlockSpec(memory_space=pl.ANY), pl.BlockSpec(memory_space=pl.ANY)],
        out_specs=row,
        out_shape=jax.ShapeDtypeStruct((t, D_MODEL), F32),
        scratch_shapes=[pltpu.VMEM((D_MODEL, B_WIDTH), BF16), pltpu.VMEM((B_WIDTH, D_MODEL), BF16),
                        pltpu.VMEM((CAST_SLOTS, CAST_ROWS, D_MODEL), F32),
                        pltpu.SemaphoreType.DMA((CAST_SLOTS,))],
        compiler_params=pltpu.CompilerParams(dimension_semantics=("arbitrary",),
                                             vmem_limit_bytes=VMEM_LIMIT),
        name="gate_output",
    )(x2, o2, gpre, gpost, w_in, w_out)


def kernel(x, norm_pre, norm_post, a_w_in, a_w_s, a_b_s, a_vnorm_g, a_vnorm_b, a_w_out, b_w_in, b_w_out, rel_bias):
    bsz, s, dm = x.shape
    assert dm == D_MODEL and norm_pre.shape[0] == 2 and a_w_in.shape[0] == 1 and b_w_in.shape[0] == 1

    b_s = jnp.broadcast_to(a_b_s[0][:, :, None], (A_HEADS, A_CHUNK, A_HEAD_DIM))
    x2 = _gmlp_layer(x.reshape(bsz * s, dm), norm_pre[0:1], norm_post[0:1],
                     a_w_in, a_w_s[0], b_s, a_vnorm_g, a_vnorm_b, a_w_out)

    zs = _qkv_projection(x2.reshape(bsz, s, dm), norm_pre[1:2], b_w_in)
    o = _window_attention([z.reshape(bsz, s, B_GROUP_COLS) for z in zs], rel_bias)
    out = _gate_output(x2, o.reshape(bsz * s, B_WIDTH), norm_pre[1:2], norm_post[1:2], b_w_in, b_w_out)
    return out.reshape(bsz, s, dm)
```

```python
import functools
import math

import jax
import jax.numpy as jnp
import numpy as np
from jax import lax
from jax.experimental import pallas as pl
from jax.experimental.pallas import tpu as pltpu

F32 = jnp.float32
BF16 = jnp.bfloat16

EPS = 1e-6
NEG_INF = -1e30

D_MODEL = 1024
A_WIDTH = 2048
A_CHUNK = 128
A_HEADS = 16
A_HEAD_DIM = 128

B_GROUPS = ((128, 1), (512, 4), (2048, 16))
B_HEADS = 16
B_HEAD_DIM = 64
B_WIDTH = B_HEADS * B_HEAD_DIM
B_GROUP_COLS = 3 * B_WIDTH
B_QKV = len(B_GROUPS) * B_GROUP_COLS
REL_BUCKETS = 32
REL_EXACT = 8
REL_MAX_DISTANCE = 1024

LOG2E = math.log2(math.e)
LOGIT_SCALE = B_HEAD_DIM ** -0.5 * LOG2E

LANES = 128
GMLP_TILE = 512
QKV_TILE = 512
GATE_TILE = 1024
COL_CHUNK = 512
GATE_ROW_PARTS = 2
CAST_SLOTS = 4
CAST_ROWS = 256
CAST_ROWS_WIDE = 32
HEAD_PAIRS = B_WIDTH // LANES
Q_BLOCK = 128
K_WINDOW = 256
STATE_INTERLEAVE = 4
ATTN_UNROLL = 16
VMEM_LIMIT = 58 * 1024 * 1024


def _rms(x, g):
    return x * lax.rsqrt(jnp.mean(x * x, axis=-1, keepdims=True) + EPS) * g


def _cast_rows_once(src, dst, stage, sem, *, chunk, col_blocks=None):
    col_blocks = col_blocks or [(0, 0, dst.shape[1])]
    nchunk, nslot = dst.shape[0] // chunk, stage.shape[0]
    assert dst.shape[0] % chunk == 0 and nchunk >= nslot

    def copy(i):
        return pltpu.make_async_copy(src.at[pl.ds(i * chunk, chunk), :], stage.at[i % nslot], sem.at[i % nslot])

    for i in range(nslot - 1):
        copy(i).start()
    for i in range(nchunk):
        if i + nslot - 1 < nchunk:
            copy(i + nslot - 1).start()
        copy(i).wait()
        for s0, d0, width in col_blocks:
            dst[pl.ds(i * chunk, chunk), d0:d0 + width] = stage[i % nslot, :, s0:s0 + width].astype(BF16)


_GELU_B = -2.0 * math.sqrt(2.0 / math.pi) * LOG2E
_GELU_A = 0.044715 * _GELU_B


def _gelu_denominator(x):
    return 1.0 + jnp.exp2(x * (_GELU_A * (x * x) + _GELU_B))


def _silu_denominator(x):
    return 1.0 + jnp.exp2(x * -LOG2E)


def _gmlp_kernel(x_ref, xp_ref, gpre_ref, gpost_ref, win_hbm, ws_ref, bs_ref, vng_ref, vnb_ref, wout_hbm,
                 o_ref, win_ref, wout_ref, stage_in, stage_out, sem_in, sem_out, vn_scr, ug_scr, y_scr, *, tm):
    @pl.when(pl.program_id(0) == 0)
    def _():
        half = COL_CHUNK // 2
        pairs = [blk for c in range(A_WIDTH // half)
                 for blk in ((c * half, A_WIDTH + c * COL_CHUNK, half),
                             (2 * A_WIDTH + c * half, A_WIDTH + c * COL_CHUNK + half, half))]
        _cast_rows_once(win_hbm.at[0], win_ref, stage_in, sem_in, chunk=stage_in.shape[1],
                        col_blocks=[(A_WIDTH, 0, A_WIDTH)] + pairs)
        _cast_rows_once(wout_hbm.at[0], wout_ref, stage_out, sem_out, chunk=stage_out.shape[1])

    nchunk = tm // A_CHUNK
    step, last = pl.program_id(0), pl.num_programs(0) - 1

    def finish_previous():
        out = jnp.dot(y_scr[...], wout_ref[...], preferred_element_type=F32)
        o_ref[...] = xp_ref[...] + _rms(out, gpost_ref[...])

    def v_path(h):
        v = jnp.dot(h, win_ref[:, :A_WIDTH], preferred_element_type=F32)
        v = v / _gelu_denominator(v)
        mu = jnp.mean(v, axis=-1, keepdims=True)
        vc = v - mu
        vn = vc * lax.rsqrt(jnp.mean(vc * vc, axis=-1, keepdims=True) + EPS)
        vn_scr[...] = (vn * vng_ref[...] + vnb_ref[...]).astype(BF16)

    def gate_path(h):
        half = COL_CHUNK // 2
        for cb in range(A_WIDTH // half):
            z = jnp.dot(h, win_ref[:, A_WIDTH + cb * COL_CHUNK:A_WIDTH + (cb + 1) * COL_CHUNK],
                        preferred_element_type=F32)
            u, g = z[:, :half], z[:, half:]
            ug_scr[:, cb * half:(cb + 1) * half] = (u * g) / (_gelu_denominator(u) * _silu_denominator(g))

        for hd in range(A_HEADS):
            c0 = hd * A_HEAD_DIM
            rhs = jnp.concatenate(
                [vn_scr[c * A_CHUNK:(c + 1) * A_CHUNK, c0:c0 + A_HEAD_DIM] for c in range(nchunk)], axis=1)
            sg = jnp.dot(ws_ref[hd].astype(BF16), rhs, preferred_element_type=F32)
            for c in range(nchunk):
                rows = slice(c * A_CHUNK, (c + 1) * A_CHUNK)
                sgc = sg[:, c * A_CHUNK:(c + 1) * A_CHUNK] + bs_ref[hd]
                y_scr[rows, c0:c0 + A_HEAD_DIM] = (ug_scr[rows, c0:c0 + A_HEAD_DIM] * sgc).astype(BF16)

    @pl.when(step == 0)
    def _():
        h = _rms(x_ref[...], gpre_ref[...]).astype(BF16)
        v_path(h)
        gate_path(h)

    @pl.when((step > 0) & (step < last))
    def _():
        h = _rms(x_ref[...], gpre_ref[...]).astype(BF16)
        v_path(h)
        finish_previous()
        gate_path(h)

    @pl.when(step == last)
    def _():
        finish_previous()


def _gmlp_layer(x2, gpre, gpost, w_in, w_s, b_s, vn_g, vn_b, w_out, *, tm=GMLP_TILE):
    t = x2.shape[0]
    ntile = t // tm
    const = lambda *shape: pl.BlockSpec(shape, lambda i: (0,) * len(shape), pipeline_mode=pl.Buffered(1))
    return pl.pallas_call(
        functools.partial(_gmlp_kernel, tm=tm),
        grid=(ntile + 1,),
        in_specs=[
            pl.BlockSpec((tm, D_MODEL), lambda i: (jnp.minimum(i, ntile - 1), 0)),
            pl.BlockSpec((tm, D_MODEL), lambda i: (jnp.maximum(i - 1, 0), 0)),
            const(1, D_MODEL), const(1, D_MODEL),
            pl.BlockSpec(memory_space=pl.ANY),
            const(A_HEADS, A_CHUNK, A_CHUNK),
            const(A_HEADS, A_CHUNK, A_HEAD_DIM),
            const(1, A_WIDTH), const(1, A_WIDTH),
            pl.BlockSpec(memory_space=pl.ANY),
        ],
        out_specs=pl.BlockSpec((tm, D_MODEL), lambda i: (jnp.maximum(i - 1, 0), 0)),
        out_shape=jax.ShapeDtypeStruct((t, D_MODEL), F32),
        scratch_shapes=[pltpu.VMEM((D_MODEL, 3 * A_WIDTH), BF16), pltpu.VMEM((A_WIDTH, D_MODEL), BF16),
                        pltpu.VMEM((CAST_SLOTS, CAST_ROWS_WIDE, 3 * A_WIDTH), F32),
                        pltpu.VMEM((CAST_SLOTS, CAST_ROWS, D_MODEL), F32),
                        pltpu.SemaphoreType.DMA((CAST_SLOTS,)), pltpu.SemaphoreType.DMA((CAST_SLOTS,)),
                        pltpu.VMEM((tm, A_WIDTH), BF16), pltpu.VMEM((tm, A_WIDTH), F32),
                        pltpu.VMEM((tm, A_WIDTH), BF16)],
        compiler_params=pltpu.CompilerParams(dimension_semantics=("arbitrary",),
                                             vmem_limit_bytes=VMEM_LIMIT),
        name="gmlp_layer",
    )(x2, x2, gpre, gpost, w_in, w_s, b_s, vn_g, vn_b, w_out)


def _qkv_kernel(x_ref, gpre_ref, w_hbm, *refs, tm, dils):
    out_refs = refs[:len(dils)]
    h_scr, w_ref, stage, sem = refs[len(dils):]
    nslab = D_MODEL // LANES

    @pl.when((pl.program_id(0) == 0) & (pl.program_id(1) == 0))
    def _():
        _cast_rows_once(w_hbm.at[0, :, pl.ds(0, B_QKV)], w_ref, stage, sem, chunk=stage.shape[1])

    h = _rms(x_ref[...], gpre_ref[...])
    for k in range(nslab):
        h_scr[k] = h[:, k * LANES:(k + 1) * LANES]
    nc = COL_CHUNK

    def gathered(starts, rows, stride):
        return jnp.concatenate(
            [jnp.concatenate([h_scr[k, pl.ds(st, rows, stride=stride), :] for st in starts], axis=0)
             for k in range(nslab)], axis=1).astype(BF16)

    for gi, d in enumerate(dils):
        rows = tm // d
        hp = gathered(range(d), rows, d)
        if d == 1:
            per = Q_BLOCK // STATE_INTERLEAVE
            hq = gathered([blk * Q_BLOCK + c for blk in range(tm // Q_BLOCK) for c in range(STATE_INTERLEAVE)],
                          per, STATE_INTERLEAVE)
        else:
            hq = hp
        for c in range(B_GROUP_COLS // nc):
            c0 = gi * B_GROUP_COLS + c * nc
            is_q = (c + 1) * nc <= B_WIDTH
            z = jnp.dot(hq if is_q else hp, w_ref[:, c0:c0 + nc], preferred_element_type=F32)
            if is_q:
                z = z * LOGIT_SCALE
            out_refs[gi][:, :, c * nc:(c + 1) * nc] = z.reshape(d, rows, nc).astype(BF16)


def _qkv_projection(x3, gpre, w_qkv, *, tm=QKV_TILE):
    bsz, s, _ = x3.shape
    dils = tuple(d for _, d in B_GROUPS)
    return pl.pallas_call(
        functools.partial(_qkv_kernel, tm=tm, dils=dils),
        grid=(bsz, s // tm),
        in_specs=[
            pl.BlockSpec((None, tm, D_MODEL), lambda b, i: (b, i, 0)),
            pl.BlockSpec((1, D_MODEL), lambda b, i: (0, 0)),
            pl.BlockSpec(memory_space=pl.ANY),
        ],
        out_specs=[pl.BlockSpec((None, d, tm // d, B_GROUP_COLS), lambda b, i: (b, 0, i, 0)) for d in dils],
        out_shape=[jax.ShapeDtypeStruct((bsz, d, s // d, B_GROUP_COLS), BF16) for d in dils],
        scratch_shapes=[pltpu.VMEM((D_MODEL // LANES, tm, LANES), F32),
                        pltpu.VMEM((D_MODEL, B_QKV), BF16),
                        pltpu.VMEM((CAST_SLOTS, CAST_ROWS_WIDE, B_QKV), F32),
                        pltpu.SemaphoreType.DMA((CAST_SLOTS,))],
        compiler_params=pltpu.CompilerParams(dimension_semantics=("arbitrary", "arbitrary"),
                                             vmem_limit_bytes=VMEM_LIMIT),
        name="qkv_projection",
    )(x3, gpre, w_qkv)


def _t5_bucket_np(rel):
    half = REL_BUCKETS // 2
    ret = np.where(rel > 0, half, 0)
    n = np.abs(rel)
    nf = np.maximum(n, 1).astype(np.float32)
    large = REL_EXACT + (np.log(nf / np.float32(REL_EXACT)) / np.float32(math.log(REL_MAX_DISTANCE / REL_EXACT))
                         * np.float32(half - REL_EXACT)).astype(np.int32)
    large = np.minimum(large, half - 1)
    return (ret + np.where(n < REL_EXACT, n, large)).astype(np.int32)


def _bucket_row(dilation, half_w):
    rel = np.arange(K_WINDOW, dtype=np.int32) - half_w
    row = np.where(np.abs(rel) <= half_w, _t5_bucket_np(rel * dilation), -1).astype(np.int32)
    return np.broadcast_to(row, (8, K_WINDOW))


def _attn_kernel(tbl_ref, bkt_ref, *refs, seq, groups, unroll):
    ng = len(groups)
    z_refs, o_ref = refs[:ng], refs[ng]
    buf, sem, acc_scr, m_scr, den_scr, bm_scr, p_scr, pm_scr = refs[ng + 1:]
    per = Q_BLOCK // STATE_INTERLEAVE
    hp, b = pl.program_id(0), pl.program_id(1)
    nb = pl.num_programs(1)
    step = hp * nb + b
    nsteps = pl.num_programs(0) * nb

    def slab_copies(g, hp_, b_):
        return [pltpu.make_async_copy(
            z_refs[g].at[b_, :, pl.ds(pl.multiple_of((j * HEAD_PAIRS + hp_) * LANES, LANES), LANES)],
            buf.at[g, j], sem.at[g, j]) for j in range(3)]

    @pl.when(step == 0)
    def _():
        for cp in slab_copies(0, hp, b):
            cp.start()

    @pl.when(b == 0)
    def _():
        key = lax.broadcasted_iota(jnp.int32, (Q_BLOCK, K_WINDOW), 1)
        for g, (d, half_w, col) in enumerate(groups):
            for hh in range(2):
                def fill(bk, row, g=g, hh=hh, col=col):
                    return jnp.where(bkt_ref[g] == bk, tbl_ref[bk, col + 2 * hp + hh], row)
                row = lax.fori_loop(0, REL_BUCKETS, fill, jnp.full((8, K_WINDOW), NEG_INF, F32)) * LOG2E
                if d == 1:
                    mid = jnp.concatenate(
                        [pltpu.roll(jnp.broadcast_to(row[0:1], (per, K_WINDOW)), c, 1,
                                    stride=STATE_INTERLEAVE, stride_axis=0) for c in range(STATE_INTERLEAVE)], axis=0)
                else:
                    mid = pltpu.roll(jnp.broadcast_to(row[0:1], (Q_BLOCK, K_WINDOW)), 0, 1, stride=1, stride_axis=0)
                bm_scr[g, hh, 1] = mid
                bm_scr[g, hh, 0] = jnp.where(key < K_WINDOW - half_w,
                                             pltpu.roll(mid, K_WINDOW - half_w, 1), NEG_INF)
                bm_scr[g, hh, 2] = jnp.where(key >= half_w, pltpu.roll(mid, half_w, 1), NEG_INF)

    first_head = lax.broadcasted_iota(jnp.int32, (Q_BLOCK, LANES), 1) < B_HEAD_DIM
    ones = jnp.ones((K_WINDOW, LANES), BF16)

    def run_group(g):
        d, half_w, _ = groups[g]
        length = seq // d
        nblk = length // Q_BLOCK
        ntrips = d * nblk // unroll
        q_ref, k_ref, v_ref = buf.at[g, 0], buf.at[g, 1], buf.at[g, 2]

        def blocks(it):
            for u in range(unroll):
                if unroll >= nblk:
                    r, i = it * (unroll // nblk) + u // nblk, u % nblk
                else:
                    chunks = nblk // unroll
                    r = it // chunks if d > 1 else 0
                    i = (it - r * chunks) * unroll + u
                k0 = pl.multiple_of(r * length + jnp.clip(i * Q_BLOCK - half_w, 0, length - K_WINDOW), half_w)
                yield u, r, i, k0

        def logits_stage(it, slot):
            for u, r, i, k0 in blocks(it):
                q0 = pl.multiple_of(r * length + i * Q_BLOCK, Q_BLOCK)
                var = jnp.where(i == 0, 0, jnp.where(i == nblk - 1, 2, 1))
                qb = q_ref[pl.ds(q0, Q_BLOCK), :]
                kw = k_ref[pl.ds(k0, K_WINDOW), :]
                zero = jnp.zeros_like(qb)
                q2 = jnp.concatenate([jnp.where(first_head, qb, zero), jnp.where(first_head, zero, qb)], axis=0)
                s = lax.dot_general(q2, kw, (((1,), (1,)), ((), ())), preferred_element_type=F32)
                s = s + jnp.concatenate([bm_scr[g, 0, var], bm_scr[g, 1, var]], axis=0)
                m2 = jnp.max(s, axis=-1, keepdims=True)
                p_scr[slot, u] = jnp.exp2(s - m2).astype(BF16)
                pm_scr[slot, u] = jnp.where(first_head, m2[:Q_BLOCK], m2[Q_BLOCK:])

        def values_stage(it, slot):
            for u, r, i, k0 in blocks(it):
                vw = v_ref[pl.ds(k0, K_WINDOW), :]
                ov = jnp.dot(p_scr[slot, u], jnp.concatenate([vw, ones], axis=1), preferred_element_type=F32)
                acc = jnp.where(first_head, ov[:Q_BLOCK, :LANES], ov[Q_BLOCK:, :LANES])
                den = jnp.where(first_head, ov[:Q_BLOCK, LANES:], ov[Q_BLOCK:, LANES:])
                m = pm_scr[slot, u]
                if d == 1:
                    pieces = [(0, Q_BLOCK, pl.ds(i * Q_BLOCK, Q_BLOCK))]
                else:
                    n = Q_BLOCK // d
                    base = r // STATE_INTERLEAVE + per * (r % STATE_INTERLEAVE)
                    pieces = [(k * n, n, pl.ds((i * d + k) * Q_BLOCK + base, n, stride=d // STATE_INTERLEAVE))
                              for k in range(d)]
                gather = lambda ref: jnp.concatenate([ref[idx, :] for _, _, idx in pieces], axis=0)
                if g > 0:
                    m_old = gather(m_scr)
                    m_new = jnp.maximum(m_old, m)
                    w_old, w_cur = jnp.exp2(m_old - m_new), jnp.exp2(m - m_new)
                    acc = w_old * gather(acc_scr) + w_cur * acc
                    den = w_old * gather(den_scr) + w_cur * den
                    m = m_new
                if g == ng - 1:
                    o = acc / den
                    for c in range(STATE_INTERLEAVE):
                        o_ref[pl.ds(i * Q_BLOCK + c, per, stride=STATE_INTERLEAVE), :] = o[c * per:(c + 1) * per]
                else:
                    for ref, val in ((acc_scr, acc), (m_scr, m), (den_scr, den)):
                        for r0, n, idx in pieces:
                            ref[idx, :] = val[r0:r0 + n]

        logits_stage(0, 0)

        def pair(j, carry):
            logits_stage(2 * j + 1, 1)
            values_stage(2 * j, 0)
            logits_stage(2 * j + 2, 0)
            values_stage(2 * j + 1, 1)
            return carry

        lax.fori_loop(0, ntrips // 2 - 1, pair, 0)
        logits_stage(ntrips - 1, 1)
        values_stage(ntrips - 2, 0)
        values_stage(ntrips - 1, 1)

    for g in range(ng):
        if g + 1 < ng:
            for cp in slab_copies(g + 1, hp, b):
                cp.start()
        else:
            @pl.when(step + 1 < nsteps)
            def _():
                nxt = step + 1
                for cp in slab_copies(0, nxt // nb, nxt % nb):
                    cp.start()
        for cp in slab_copies(g, hp, b):
            cp.wait()
        run_group(g)


def _window_attention(zs, rel_bias):
    bsz, s, _ = zs[0].shape
    order = sorted(range(len(B_GROUPS)), key=lambda gi: -B_GROUPS[gi][1])
    groups = tuple((B_GROUPS[gi][1], B_GROUPS[gi][0] // (2 * B_GROUPS[gi][1]), gi * B_HEADS) for gi in order)
    zs = [zs[gi] for gi in order]
    assert groups[-1][0] == 1
    for d, half_w, _ in groups:
        nblk = s // d // Q_BLOCK
        assert K_WINDOW == Q_BLOCK + 2 * half_w and s // d >= K_WINDOW and s % (d * Q_BLOCK) == 0
        assert (d * nblk) % (2 * ATTN_UNROLL) == 0 and (ATTN_UNROLL % nblk == 0 or nblk % ATTN_UNROLL == 0)
        assert d == 1 or (d % STATE_INTERLEAVE == 0 and Q_BLOCK % d == 0)
    bkt = jnp.asarray(np.stack([_bucket_row(d, half_w) for d, half_w, _ in groups]))
    return pl.pallas_call(
        functools.partial(_attn_kernel, seq=s, groups=groups, unroll=ATTN_UNROLL),
        grid=(HEAD_PAIRS, bsz),
        in_specs=[pl.BlockSpec(memory_space=pltpu.SMEM),
                  pl.BlockSpec(bkt.shape, lambda hp, b: (0, 0, 0))]
                 + [pl.BlockSpec(memory_space=pl.ANY)] * len(groups),
        out_specs=pl.BlockSpec((None, s, LANES), lambda hp, b: (b, 0, hp)),
        out_shape=jax.ShapeDtypeStruct((bsz, s, B_WIDTH), F32),
        scratch_shapes=[pltpu.VMEM((len(groups), 3, s, LANES), BF16),
                        pltpu.SemaphoreType.DMA((len(groups), 3)),
                        pltpu.VMEM((s, LANES), F32),
                        pltpu.VMEM((s, LANES), F32),
                        pltpu.VMEM((s, LANES), F32),
                        pltpu.VMEM((len(groups), 2, 3, Q_BLOCK, K_WINDOW), F32),
                        pltpu.VMEM((2, ATTN_UNROLL, 2 * Q_BLOCK, K_WINDOW), BF16),
                        pltpu.VMEM((2, ATTN_UNROLL, Q_BLOCK, LANES), F32)],
        compiler_params=pltpu.CompilerParams(dimension_semantics=("arbitrary", "arbitrary"),
                                             vmem_limit_bytes=VMEM_LIMIT),
        name="window_attention",
    )(rel_bias.astype(F32), bkt, *zs)


def _out_kernel(x_ref, o_ref, gpre_ref, gpost_ref, win_hbm, wout_hbm, out_ref, wg_ref, wout_ref, stage, sem):
    @pl.when(pl.program_id(0) == 0)
    def _():
        _cast_rows_once(win_hbm.at[0, :, pl.ds(B_QKV, B_WIDTH)], wg_ref, stage, sem, chunk=stage.shape[1])
        _cast_rows_once(wout_hbm.at[0], wout_ref, stage, sem, chunk=stage.shape[1])

    tm = x_ref.shape[0]
    parts = [slice(k * (tm // GATE_ROW_PARTS), (k + 1) * (tm // GATE_ROW_PARTS)) for k in range(GATE_ROW_PARTS)]
    ys = []
    for rows in parts:
        h = _rms(x_ref[rows, :], gpre_ref[...]).astype(BF16)
        gate = jnp.dot(h, wg_ref[...], preferred_element_type=F32)
        ys.append((o_ref[rows, :] * gate / _silu_denominator(gate)).astype(BF16))
    for rows, y in zip(parts, ys):
        out = jnp.dot(y, wout_ref[...], preferred_element_type=F32)
        out_ref[rows, :] = x_ref[rows, :] + _rms(out, gpost_ref[...])


def _gate_output(x2, o2, gpre, gpost, w_in, w_out, *, tm=GATE_TILE):
    t = x2.shape[0]
    const = lambda *shape: pl.BlockSpec(shape, lambda i: (0,) * len(shape), pipeline_mode=pl.Buffered(1))
    row = pl.BlockSpec((tm, D_MODEL), lambda i: (i, 0))
    return pl.pallas_call(
        _out_kernel,
        grid=(t // tm,),
        in_specs=[row, row, const(1, D_MODEL), const(1, D_MODEL),
                  pl.BlockSpec(memory_space=pl.ANY), pl.BlockSpec(memory_space=pl.ANY)],
        out_specs=row,
        out_shape=jax.ShapeDtypeStruct((t, D_MODEL), F32),
        scratch_shapes=[pltpu.VMEM((D_MODEL, B_WIDTH), BF16), pltpu.VMEM((B_WIDTH, D_MODEL), BF16),
                        pltpu.VMEM((CAST_SLOTS, CAST_ROWS, D_MODEL), F32),
                        pltpu.SemaphoreType.DMA((CAST_SLOTS,))],
        compiler_params=pltpu.CompilerParams(dimension_semantics=("arbitrary",),
                                             vmem_limit_bytes=VMEM_LIMIT),
        name="gate_output",
    )(x2, o2, gpre, gpost, w_in, w_out)


def kernel(x, norm_pre, norm_post, a_w_in, a_w_s, a_b_s, a_vnorm_g, a_vnorm_b, a_w_out, b_w_in, b_w_out, rel_bias):
    bsz, s, dm = x.shape
    assert dm == D_MODEL and norm_pre.shape[0] == 2 and a_w_in.shape[0] == 1 and b_w_in.shape[0] == 1

    b_s = jnp.broadcast_to(a_b_s[0][:, :, None], (A_HEADS, A_CHUNK, A_HEAD_DIM))
    x2 = _gmlp_layer(x.reshape(bsz * s, dm), norm_pre[0:1], norm_post[0:1],
                     a_w_in, a_w_s[0], b_s, a_vnorm_g, a_vnorm_b, a_w_out)

    zs = _qkv_projection(x2.reshape(bsz, s, dm), norm_pre[1:2], b_w_in)
    o = _window_attention([z.reshape(bsz, s, B_GROUP_COLS) for z in zs], rel_bias)
    out = _gate_output(x2, o.reshape(bsz * s, B_WIDTH), norm_pre[1:2], norm_post[1:2], b_w_in, b_w_out)
    return out.reshape(bsz, s, dm)
```

```python
import functools
import math

import jax
import jax.numpy as jnp
import numpy as np
from jax import lax
from jax.experimental import pallas as pl
from jax.experimental.pallas import tpu as pltpu

F32 = jnp.float32
BF16 = jnp.bfloat16

EPS = 1e-6
NEG_INF = -1e30

D_MODEL = 1024
A_WIDTH = 2048
A_CHUNK = 128
A_HEADS = 16
A_HEAD_DIM = 128

B_GROUPS = ((128, 1), (512, 4), (2048, 16))
B_HEADS = 16
B_HEAD_DIM = 64
B_WIDTH = B_HEADS * B_HEAD_DIM
B_GROUP_COLS = 3 * B_WIDTH
B_QKV = len(B_GROUPS) * B_GROUP_COLS
REL_BUCKETS = 32
REL_EXACT = 8
REL_MAX_DISTANCE = 1024

LOG2E = math.log2(math.e)
LOGIT_SCALE = B_HEAD_DIM ** -0.5 * LOG2E

LANES = 128
GMLP_TILE = 512
QKV_TILE = 512
GATE_TILE = 1024
COL_CHUNK = 512
GATE_ROW_PARTS = 2
CAST_SLOTS = 4
CAST_ROWS = 256
CAST_ROWS_WIDE = 32
HEAD_PAIRS = B_WIDTH // LANES
Q_BLOCK = 128
K_WINDOW = 256
STATE_INTERLEAVE = 4
ATTN_UNROLL = 16
VMEM_LIMIT = 58 * 1024 * 1024


def _rms(x, g):
    return x * lax.rsqrt(jnp.mean(x * x, axis=-1, keepdims=True) + EPS) * g


def _cast_rows_once(src, dst, stage, sem, *, chunk, col_blocks=None):
    col_blocks = col_blocks or [(0, 0, dst.shape[1])]
    nchunk, nslot = dst.shape[0] // chunk, stage.shape[0]
    assert dst.shape[0] % chunk == 0 and nchunk >= nslot

    def copy(i):
        return pltpu.make_async_copy(src.at[pl.ds(i * chunk, chunk), :], stage.at[i % nslot], sem.at[i % nslot])

    for i in range(nslot - 1):
        copy(i).start()
    for i in range(nchunk):
        if i + nslot - 1 < nchunk:
            copy(i + nslot - 1).start()
        copy(i).wait()
        for s0, d0, width in col_blocks:
            dst[pl.ds(i * chunk, chunk), d0:d0 + width] = stage[i % nslot, :, s0:s0 + width].astype(BF16)


_GELU_B = -2.0 * math.sqrt(2.0 / math.pi) * LOG2E
_GELU_A = 0.044715 * _GELU_B


def _gelu_denominator(x):
    return 1.0 + jnp.exp2(x * (_GELU_A * (x * x) + _GELU_B))


def _silu_denominator(x):
    return 1.0 + jnp.exp2(x * -LOG2E)


def _gmlp_kernel(x_ref, gpre_ref, gpost_ref, win_hbm, ws_ref, bs_ref, vng_ref, vnb_ref, wout_hbm,
                 o_ref, win_ref, wout_ref, stage_in, stage_out, sem_in, sem_out, vn_scr, ug_scr, y_scr, *, tm):
    @pl.when(pl.program_id(0) == 0)
    def _():
        half = COL_CHUNK // 2
        pairs = [blk for c in range(A_WIDTH // half)
                 for blk in ((c * half, A_WIDTH + c * COL_CHUNK, half),
                             (2 * A_WIDTH + c * half, A_WIDTH + c * COL_CHUNK + half, half))]
        _cast_rows_once(win_hbm.at[0], win_ref, stage_in, sem_in, chunk=stage_in.shape[1],
                        col_blocks=[(A_WIDTH, 0, A_WIDTH)] + pairs)
        _cast_rows_once(wout_hbm.at[0], wout_ref, stage_out, sem_out, chunk=stage_out.shape[1])

    nchunk = tm // A_CHUNK
    x = x_ref[...]
    h = _rms(x, gpre_ref[...]).astype(BF16)

    v = jnp.dot(h, win_ref[:, :A_WIDTH], preferred_element_type=F32)
    v = v / _gelu_denominator(v)
    mu = jnp.mean(v, axis=-1, keepdims=True)
    vc = v - mu
    vn = vc * lax.rsqrt(jnp.mean(vc * vc, axis=-1, keepdims=True) + EPS)
    vn_scr[...] = (vn * vng_ref[...] + vnb_ref[...]).astype(BF16)

    half = COL_CHUNK // 2
    for cb in range(A_WIDTH // half):
        z = jnp.dot(h, win_ref[:, A_WIDTH + cb * COL_CHUNK:A_WIDTH + (cb + 1) * COL_CHUNK],
                    preferred_element_type=F32)
        u, g = z[:, :half], z[:, half:]
        ug_scr[:, cb * half:(cb + 1) * half] = (u / _gelu_denominator(u)) * (g / _silu_denominator(g))

    for hd in range(A_HEADS):
        c0 = hd * A_HEAD_DIM
        rhs = jnp.concatenate(
            [vn_scr[c * A_CHUNK:(c + 1) * A_CHUNK, c0:c0 + A_HEAD_DIM] for c in range(nchunk)], axis=1)
        sg = jnp.dot(ws_ref[hd].astype(BF16), rhs, preferred_element_type=F32)
        for c in range(nchunk):
            rows = slice(c * A_CHUNK, (c + 1) * A_CHUNK)
            sgc = sg[:, c * A_CHUNK:(c + 1) * A_CHUNK] + bs_ref[hd]
            y_scr[rows, c0:c0 + A_HEAD_DIM] = (ug_scr[rows, c0:c0 + A_HEAD_DIM] * sgc).astype(BF16)

    out = jnp.dot(y_scr[...], wout_ref[...], preferred_element_type=F32)
    o_ref[...] = x + _rms(out, gpost_ref[...])


def _gmlp_layer(x2, gpre, gpost, w_in, w_s, b_s, vn_g, vn_b, w_out, *, tm=GMLP_TILE):
    t = x2.shape[0]
    const = lambda *shape: pl.BlockSpec(shape, lambda i: (0,) * len(shape), pipeline_mode=pl.Buffered(1))
    return pl.pallas_call(
        functools.partial(_gmlp_kernel, tm=tm),
        grid=(t // tm,),
        in_specs=[
            pl.BlockSpec((tm, D_MODEL), lambda i: (i, 0)),
            const(1, D_MODEL), const(1, D_MODEL),
            pl.BlockSpec(memory_space=pl.ANY),
            const(A_HEADS, A_CHUNK, A_CHUNK),
            const(A_HEADS, A_CHUNK, A_HEAD_DIM),
            const(1, A_WIDTH), const(1, A_WIDTH),
            pl.BlockSpec(memory_space=pl.ANY),
        ],
        out_specs=pl.BlockSpec((tm, D_MODEL), lambda i: (i, 0)),
        out_shape=jax.ShapeDtypeStruct((t, D_MODEL), F32),
        scratch_shapes=[pltpu.VMEM((D_MODEL, 3 * A_WIDTH), BF16), pltpu.VMEM((A_WIDTH, D_MODEL), BF16),
                        pltpu.VMEM((CAST_SLOTS, CAST_ROWS_WIDE, 3 * A_WIDTH), F32),
                        pltpu.VMEM((CAST_SLOTS, CAST_ROWS, D_MODEL), F32),
                        pltpu.SemaphoreType.DMA((CAST_SLOTS,)), pltpu.SemaphoreType.DMA((CAST_SLOTS,)),
                        pltpu.VMEM((tm, A_WIDTH), BF16), pltpu.VMEM((tm, A_WIDTH), F32),
                        pltpu.VMEM((tm, A_WIDTH), BF16)],
        compiler_params=pltpu.CompilerParams(dimension_semantics=("arbitrary",),
                                             vmem_limit_bytes=VMEM_LIMIT),
        name="gmlp_layer",
    )(x2, gpre, gpost, w_in, w_s, b_s, vn_g, vn_b, w_out)


def _qkv_kernel(x_ref, gpre_ref, w_hbm, *refs, tm, dils):
    out_refs = refs[:len(dils)]
    h_scr, w_ref, stage, sem = refs[len(dils):]
    nslab = D_MODEL // LANES

    @pl.when((pl.program_id(0) == 0) & (pl.program_id(1) == 0))
    def _():
        _cast_rows_once(w_hbm.at[0, :, pl.ds(0, B_QKV)], w_ref, stage, sem, chunk=stage.shape[1])

    h = _rms(x_ref[...], gpre_ref[...])
    for k in range(nslab):
        h_scr[k] = h[:, k * LANES:(k + 1) * LANES]
    nc = COL_CHUNK

    def gathered(starts, rows, stride):
        return jnp.concatenate(
            [jnp.concatenate([h_scr[k, pl.ds(st, rows, stride=stride), :] for st in starts], axis=0)
             for k in range(nslab)], axis=1).astype(BF16)

    for gi, d in enumerate(dils):
        rows = tm // d
        hp = gathered(range(d), rows, d)
        if d == 1:
            per = Q_BLOCK // STATE_INTERLEAVE
            hq = gathered([blk * Q_BLOCK + c for blk in range(tm // Q_BLOCK) for c in range(STATE_INTERLEAVE)],
                          per, STATE_INTERLEAVE)
        else:
            hq = hp
        for c in range(B_GROUP_COLS // nc):
            c0 = gi * B_GROUP_COLS + c * nc
            is_q = (c + 1) * nc <= B_WIDTH
            z = jnp.dot(hq if is_q else hp, w_ref[:, c0:c0 + nc], preferred_element_type=F32)
            if is_q:
                z = z * LOGIT_SCALE
            out_refs[gi][:, :, c * nc:(c + 1) * nc] = z.reshape(d, rows, nc).astype(BF16)


def _qkv_projection(x3, gpre, w_qkv, *, tm=QKV_TILE):
    bsz, s, _ = x3.shape
    dils = tuple(d for _, d in B_GROUPS)
    return pl.pallas_call(
        functools.partial(_qkv_kernel, tm=tm, dils=dils),
        grid=(bsz, s // tm),
        in_specs=[
            pl.BlockSpec((None, tm, D_MODEL), lambda b, i: (b, i, 0)),
            pl.BlockSpec((1, D_MODEL), lambda b, i: (0, 0)),
            pl.BlockSpec(memory_space=pl.ANY),
        ],
        out_specs=[pl.BlockSpec((None, d, tm // d, B_GROUP_COLS), lambda b, i: (b, 0, i, 0)) for d in dils],
        out_shape=[jax.ShapeDtypeStruct((bsz, d, s // d, B_GROUP_COLS), BF16) for d in dils],
        scratch_shapes=[pltpu.VMEM((D_MODEL // LANES, tm, LANES), F32),
                        pltpu.VMEM((D_MODEL, B_QKV), BF16),
                        pltpu.VMEM((CAST_SLOTS, CAST_ROWS_WIDE, B_QKV), F32),
                        pltpu.SemaphoreType.DMA((CAST_SLOTS,))],
        compiler_params=pltpu.CompilerParams(dimension_semantics=("arbitrary", "arbitrary"),
                                             vmem_limit_bytes=VMEM_LIMIT),
        name="qkv_projection",
    )(x3, gpre, w_qkv)


def _t5_bucket_np(rel):
    half = REL_BUCKETS // 2
    ret = np.where(rel > 0, half, 0)
    n = np.abs(rel)
    nf = np.maximum(n, 1).astype(np.float32)
    large = REL_EXACT + (np.log(nf / np.float32(REL_EXACT)) / np.float32(math.log(REL_MAX_DISTANCE / REL_EXACT))
                         * np.float32(half - REL_EXACT)).astype(np.int32)
    large = np.minimum(large, half - 1)
    return (ret + np.where(n < REL_EXACT, n, large)).astype(np.int32)


def _bucket_row(dilation, half_w):
    rel = np.arange(K_WINDOW, dtype=np.int32) - half_w
    row = np.where(np.abs(rel) <= half_w, _t5_bucket_np(rel * dilation), -1).astype(np.int32)
    return np.broadcast_to(row, (8, K_WINDOW))


def _attn_kernel(tbl_ref, bkt_ref, *refs, seq, groups, unroll):
    ng = len(groups)
    z_refs, o_ref = refs[:ng], refs[ng]
    buf, sem, acc_scr, m_scr, den_scr, bm_scr, p_scr, pm_scr = refs[ng + 1:]
    per = Q_BLOCK // STATE_INTERLEAVE
    hp, b = pl.program_id(0), pl.program_id(1)
    nb = pl.num_programs(1)
    step = hp * nb + b
    nsteps = pl.num_programs(0) * nb

    def slab_copies(g, hp_, b_):
        return [pltpu.make_async_copy(
            z_refs[g].at[b_, :, pl.ds(pl.multiple_of((j * HEAD_PAIRS + hp_) * LANES, LANES), LANES)],
            buf.at[g, j], sem.at[g, j]) for j in range(3)]

    @pl.when(step == 0)
    def _():
        for cp in slab_copies(0, hp, b):
            cp.start()

    @pl.when(b == 0)
    def _():
        key = lax.broadcasted_iota(jnp.int32, (Q_BLOCK, K_WINDOW), 1)
        for g, (d, half_w, col) in enumerate(groups):
            for hh in range(2):
                def fill(bk, row, g=g, hh=hh, col=col):
                    return jnp.where(bkt_ref[g] == bk, tbl_ref[bk, col + 2 * hp + hh], row)
                row = lax.fori_loop(0, REL_BUCKETS, fill, jnp.full((8, K_WINDOW), NEG_INF, F32)) * LOG2E
                if d == 1:
                    mid = jnp.concatenate(
                        [pltpu.roll(jnp.broadcast_to(row[0:1], (per, K_WINDOW)), c, 1,
                                    stride=STATE_INTERLEAVE, stride_axis=0) for c in range(STATE_INTERLEAVE)], axis=0)
                else:
                    mid = pltpu.roll(jnp.broadcast_to(row[0:1], (Q_BLOCK, K_WINDOW)), 0, 1, stride=1, stride_axis=0)
                bm_scr[g, hh, 1] = mid
                bm_scr[g, hh, 0] = jnp.where(key < K_WINDOW - half_w,
                                             pltpu.roll(mid, K_WINDOW - half_w, 1), NEG_INF)
                bm_scr[g, hh, 2] = jnp.where(key >= half_w, pltpu.roll(mid, half_w, 1), NEG_INF)

    first_head = lax.broadcasted_iota(jnp.int32, (Q_BLOCK, LANES), 1) < B_HEAD_DIM
    ones = jnp.ones((K_WINDOW, LANES), BF16)

    def run_group(g):
        d, half_w, _ = groups[g]
        length = seq // d
        nblk = length // Q_BLOCK
        ntrips = d * nblk // unroll
        q_ref, k_ref, v_ref = buf.at[g, 0], buf.at[g, 1], buf.at[g, 2]

        def blocks(it):
            for u in range(unroll):
                if unroll >= nblk:
                    r, i = it * (unroll // nblk) + u // nblk, u % nblk
                else:
                    chunks = nblk // unroll
                    r = it // chunks if d > 1 else 0
                    i = (it - r * chunks) * unroll + u
                k0 = pl.multiple_of(r * length + jnp.clip(i * Q_BLOCK - half_w, 0, length - K_WINDOW), half_w)
                yield u, r, i, k0

        def logits_stage(it, slot):
            for u, r, i, k0 in blocks(it):
                q0 = pl.multiple_of(r * length + i * Q_BLOCK, Q_BLOCK)
                var = jnp.where(i == 0, 0, jnp.where(i == nblk - 1, 2, 1))
                qb = q_ref[pl.ds(q0, Q_BLOCK), :]
                kw = k_ref[pl.ds(k0, K_WINDOW), :]
                zero = jnp.zeros_like(qb)
                q2 = jnp.concatenate([jnp.where(first_head, qb, zero), jnp.where(first_head, zero, qb)], axis=0)
                s = lax.dot_general(q2, kw, (((1,), (1,)), ((), ())), preferred_element_type=F32)
                s = s + jnp.concatenate([bm_scr[g, 0, var], bm_scr[g, 1, var]], axis=0)
                m2 = jnp.max(s, axis=-1, keepdims=True)
                p_scr[slot, u] = jnp.exp2(s - m2).astype(BF16)
                pm_scr[slot, u] = jnp.where(first_head, m2[:Q_BLOCK], m2[Q_BLOCK:])

        def values_stage(it, slot):
            for u, r, i, k0 in blocks(it):
                vw = v_ref[pl.ds(k0, K_WINDOW), :]
                ov = jnp.dot(p_scr[slot, u], jnp.concatenate([vw, ones], axis=1), preferred_element_type=F32)
                acc = jnp.where(first_head, ov[:Q_BLOCK, :LANES], ov[Q_BLOCK:, :LANES])
                den = jnp.where(first_head, ov[:Q_BLOCK, LANES:], ov[Q_BLOCK:, LANES:])
                m = pm_scr[slot, u]
                if d == 1:
                    pieces = [(0, Q_BLOCK, pl.ds(i * Q_BLOCK, Q_BLOCK))]
                else:
                    n = Q_BLOCK // d
                    base = r // STATE_INTERLEAVE + per * (r % STATE_INTERLEAVE)
                    pieces = [(k * n, n, pl.ds((i * d + k) * Q_BLOCK + base, n, stride=d // STATE_INTERLEAVE))
                              for k in range(d)]
                gather = lambda ref: jnp.concatenate([ref[idx, :] for _, _, idx in pieces], axis=0)
                if g > 0:
                    m_old = gather(m_scr)
                    m_new = jnp.maximum(m_old, m)
                    w_old, w_cur = jnp.exp2(m_old - m_new), jnp.exp2(m - m_new)
                    acc = w_old * gather(acc_scr) + w_cur * acc
                    den = w_old * gather(den_scr) + w_cur * den
                    m = m_new
                if g == ng - 1:
                    o = acc / den
                    for c in range(STATE_INTERLEAVE):
                        o_ref[pl.ds(i * Q_BLOCK + c, per, stride=STATE_INTERLEAVE), :] = o[c * per:(c + 1) * per]
                else:
                    for ref, val in ((acc_scr, acc), (m_scr, m), (den_scr, den)):
                        for r0, n, idx in pieces:
                            ref[idx, :] = val[r0:r0 + n]

        logits_stage(0, 0)

        def pair(j, carry):
            logits_stage(2 * j + 1, 1)
            values_stage(2 * j, 0)
            logits_stage(2 * j + 2, 0)
            values_stage(2 * j + 1, 1)
            return carry

        lax.fori_loop(0, ntrips // 2 - 1, pair, 0)
        logits_stage(ntrips - 1, 1)
        values_stage(ntrips - 2, 0)
        values_stage(ntrips - 1, 1)

    for g in range(ng):
        if g + 1 < ng:
            for cp in slab_copies(g + 1, hp, b):
                cp.start()
        else:
            @pl.when(step + 1 < nsteps)
            def _():
                nxt = step + 1
                for cp in slab_copies(0, nxt // nb, nxt % nb):
                    cp.start()
        for cp in slab_copies(g, hp, b):
            cp.wait()
        run_group(g)


def _window_attention(zs, rel_bias):
    bsz, s, _ = zs[0].shape
    order = sorted(range(len(B_GROUPS)), key=lambda gi: -B_GROUPS[gi][1])
    groups = tuple((B_GROUPS[gi][1], B_GROUPS[gi][0] // (2 * B_GROUPS[gi][1]), gi * B_HEADS) for gi in order)
    zs = [zs[gi] for gi in order]
    assert groups[-1][0] == 1
    for d, half_w, _ in groups:
        nblk = s // d // Q_BLOCK
        assert K_WINDOW == Q_BLOCK + 2 * half_w and s // d >= K_WINDOW and s % (d * Q_BLOCK) == 0
        assert (d * nblk) % (2 * ATTN_UNROLL) == 0 and (ATTN_UNROLL % nblk == 0 or nblk % ATTN_UNROLL == 0)
        assert d == 1 or (d % STATE_INTERLEAVE == 0 and Q_BLOCK % d == 0)
    bkt = jnp.asarray(np.stack([_bucket_row(d, half_w) for d, half_w, _ in groups]))
    return pl.pallas_call(
        functools.partial(_attn_kernel, seq=s, groups=groups, unroll=ATTN_UNROLL),
        grid=(HEAD_PAIRS, bsz),
        in_specs=[pl.BlockSpec(memory_space=pltpu.SMEM),
                  pl.BlockSpec(bkt.shape, lambda hp, b: (0, 0, 0))]
                 + [pl.BlockSpec(memory_space=pl.ANY)] * len(groups),
        out_specs=pl.BlockSpec((None, s, LANES), lambda hp, b: (b, 0, hp)),
        out_shape=jax.ShapeDtypeStruct((bsz, s, B_WIDTH), F32),
        scratch_shapes=[pltpu.VMEM((len(groups), 3, s, LANES), BF16),
                        pltpu.SemaphoreType.DMA((len(groups), 3)),
                        pltpu.VMEM((s, LANES), F32),
                        pltpu.VMEM((s, LANES), F32),
                        pltpu.VMEM((s, LANES), F32),
                        pltpu.VMEM((len(groups), 2, 3, Q_BLOCK, K_WINDOW), F32),
                        pltpu.VMEM((2, ATTN_UNROLL, 2 * Q_BLOCK, K_WINDOW), BF16),
                        pltpu.VMEM((2, ATTN_UNROLL, Q_BLOCK, LANES), F32)],
        compiler_params=pltpu.CompilerParams(dimension_semantics=("arbitrary", "arbitrary"),
                                             vmem_limit_bytes=VMEM_LIMIT),
        name="window_attention",
    )(rel_bias.astype(F32), bkt, *zs)


def _out_kernel(x_ref, o_ref, gpre_ref, gpost_ref, win_hbm, wout_hbm, out_ref, wg_ref, wout_ref, stage, sem):
    @pl.when(pl.program_id(0) == 0)
    def _():
        _cast_rows_once(win_hbm.at[0, :, pl.ds(B_QKV, B_WIDTH)], wg_ref, stage, sem, chunk=stage.shape[1])
        _cast_rows_once(wout_hbm.at[0], wout_ref, stage, sem, chunk=stage.shape[1])

    tm = x_ref.shape[0]
    parts = [slice(k * (tm // GATE_ROW_PARTS), (k + 1) * (tm // GATE_ROW_PARTS)) for k in range(GATE_ROW_PARTS)]
    ys = []
    for rows in parts:
        h = _rms(x_ref[rows, :], gpre_ref[...]).astype(BF16)
        gate = jnp.dot(h, wg_ref[...], preferred_element_type=F32)
        ys.append((o_ref[rows, :] * (gate / _silu_denominator(gate))).astype(BF16))
    for rows, y in zip(parts, ys):
        out = jnp.dot(y, wout_ref[...], preferred_element_type=F32)
        out_ref[rows, :] = x_ref[rows, :] + _rms(out, gpost_ref[...])


def _gate_output(x2, o2, gpre, gpost, w_in, w_out, *, tm=GATE_TILE):
    t = x2.shape[0]
    const = lambda *shape: pl.BlockSpec(shape, lambda i: (0,) * len(shape), pipeline_mode=pl.Buffered(1))
    row = pl.BlockSpec((tm, D_MODEL), lambda i: (i, 0))
    return pl.pallas_call(
        _out_kernel,
        grid=(t // tm,),
        in_specs=[row, row, const(1, D_MODEL), const(1, D_MODEL),
                  pl.BlockSpec(memory_space=pl.ANY), pl.BlockSpec(memory_space=pl.ANY)],
        out_specs=row,
        out_shape=jax.ShapeDtypeStruct((t, D_MODEL), F32),
        scratch_shapes=[pltpu.VMEM((D_MODEL, B_WIDTH), BF16), pltpu.VMEM((B_WIDTH, D_MODEL), BF16),
                        pltpu.VMEM((CAST_SLOTS, CAST_ROWS, D_MODEL), F32),
                        pltpu.SemaphoreType.DMA((CAST_SLOTS,))],
        compiler_params=pltpu.CompilerParams(dimension_semantics=("arbitrary",),
                                             vmem_limit_bytes=VMEM_LIMIT),
        name="gate_output",
    )(x2, o2, gpre, gpost, w_in, w_out)


def kernel(x, norm_pre, norm_post, a_w_in, a_w_s, a_b_s, a_vnorm_g, a_vnorm_b, a_w_out, b_w_in, b_w_out, rel_bias):
    bsz, s, dm = x.shape
    assert dm == D_MODEL and norm_pre.shape[0] == 2 and a_w_in.shape[0] == 1 and b_w_in.shape[0] == 1

    b_s = jnp.broadcast_to(a_b_s[0][:, :, None], (A_HEADS, A_CHUNK, A_HEAD_DIM))
    x2 = _gmlp_layer(x.reshape(bsz * s, dm), norm_pre[0:1], norm_post[0:1],
                     a_w_in, a_w_s[0], b_s, a_vnorm_g, a_vnorm_b, a_w_out)

    zs = _qkv_projection(x2.reshape(bsz, s, dm), norm_pre[1:2], b_w_in)
    o = _window_attention([z.reshape(bsz, s, B_GROUP_COLS) for z in zs], rel_bias)
    out = _gate_output(x2, o.reshape(bsz * s, B_WIDTH), norm_pre[1:2], norm_post[1:2], b_w_in, b_w_out)
    return out.reshape(bsz, s, dm)
```
